```python
import jax, jax.numpy as jnp
from jax import lax
import numpy as np

D_MODEL = 1024
BATCH = 2
SEQ = 8192
DEPTH = 2

POOL_WIDTH = 512
POOL_GROUPS = 4
POOL_GROUP_DIM = POOL_WIDTH // POOL_GROUPS
POOL_WINDOWS = (2, 4, 8, 16)
MLA_HEADS = 8
QK_NOPE_DIM = 64
QK_ROPE_DIM = 32
V_HEAD_DIM = 64
Q_LORA_RANK = 256
KV_LORA_RANK = 128
MLA_WIDTH = MLA_HEADS * V_HEAD_DIM
ROPE_THETA = 10000.0
ATTN_BLOCK = 128
GMLP_WIDTH = 512
GMLP_CHUNK = 128
GMLP_GROUPS = 8
GMLP_GROUP_DIM = GMLP_WIDTH // GMLP_GROUPS
N_BRANCHES = 3
EPS = 1e-6
IN_SPLITS = (POOL_WIDTH, POOL_WIDTH, Q_LORA_RANK, KV_LORA_RANK, QK_ROPE_DIM, MLA_WIDTH,
             GMLP_WIDTH, GMLP_WIDTH, GMLP_WIDTH, N_BRANCHES * D_MODEL)
IN_WIDTH = 6560

kernel_name = 'hybrid_gated_pool_mla_gmlp'


def rmsnorm(x, g):
    xf = x.astype(jnp.float32)
    y = xf * lax.rsqrt(jnp.mean(xf * xf, axis=-1, keepdims=True) + EPS) * g.astype(jnp.float32)
    return y.astype(x.dtype)


def layernorm(x, g, b):
    xf = x.astype(jnp.float32)
    mu = jnp.mean(xf, axis=-1, keepdims=True)
    xc = xf - mu
    var = jnp.mean(xc * xc, axis=-1, keepdims=True)
    y = xc * lax.rsqrt(var + EPS) * g.astype(jnp.float32) + b.astype(jnp.float32)
    return y.astype(x.dtype)


def rope_cos_sin(positions, dtype):
    inv_freq = ROPE_THETA ** (-jnp.arange(0, QK_ROPE_DIM, 2, dtype=jnp.float32) / QK_ROPE_DIM)
    ang = positions.astype(jnp.float32)[..., None] * inv_freq
    return jnp.cos(ang).astype(dtype), jnp.sin(ang).astype(dtype)


def apply_rope(x, cos, sin):
    x1, x2 = jnp.split(x, 2, axis=-1)
    return jnp.concatenate([x1 * cos - x2 * sin, x2 * cos + x1 * sin], axis=-1)


def pool_mixer(a, pool_w, pool_scale):
    B, S, _ = a.shape
    a4 = a.reshape(B, S, POOL_GROUPS, POOL_GROUP_DIM)
    cs = jnp.cumsum(a4.astype(jnp.float32), axis=1)
    t = jnp.arange(S)
    pooled = []
    for g, w in enumerate(POOL_WINDOWS):
        csg = cs[:, :, g]
        shifted = jnp.pad(csg, ((0, 0), (w, 0), (0, 0)))[:, :S]
        count = jnp.minimum(t + 1, w).astype(jnp.float32)[None, :, None]
        pooled.append((csg - shifted) / count)
    pooled = jnp.stack(pooled, axis=2)
    mixed = (pooled - a4.astype(jnp.float32)).astype(a.dtype)
    y = jnp.einsum('bsgc,gcd->bsgd', mixed, pool_w)
    return y.reshape(B, S, POOL_WIDTH) * pool_scale


def mla_mixer(cq, ckv, kr, positions, q_norm, w_uq, kv_norm, w_ukv):
    B, S, _ = cq.shape
    q = (rmsnorm(cq, q_norm) @ w_uq).reshape(B, S, MLA_HEADS, QK_NOPE_DIM + QK_ROPE_DIM)
    q_nope, q_rope = q[..., :QK_NOPE_DIM], q[..., QK_NOPE_DIM:]
    kv = (rmsnorm(ckv, kv_norm) @ w_ukv).reshape(B, S, MLA_HEADS, QK_NOPE_DIM + V_HEAD_DIM)
    k_nope, v = kv[..., :QK_NOPE_DIM], kv[..., QK_NOPE_DIM:]
    cos, sin = rope_cos_sin(positions, cq.dtype)
    q_rope = apply_rope(q_rope, cos[:, :, None, :], sin[:, :, None, :])
    k_rope = apply_rope(kr, cos, sin)
    scale = (QK_NOPE_DIM + QK_ROPE_DIM) ** -0.5
    nb = S // ATTN_BLOCK
    qn_b = q_nope.reshape(B, nb, ATTN_BLOCK, MLA_HEADS, QK_NOPE_DIM).transpose(1, 0, 2, 3, 4)
    qr_b = q_rope.reshape(B, nb, ATTN_BLOCK, MLA_HEADS, QK_ROPE_DIM).transpose(1, 0, 2, 3, 4)
    k_idx = jnp.arange(S)

    def attend_block(args):
        qn, qr, i = args
        s = (jnp.einsum('bqhd,bkhd->bhqk', qn, k_nope)
             + jnp.einsum('bqhd,bkd->bhqk', qr, k_rope)).astype(jnp.float32) * scale
        q_idx = i * ATTN_BLOCK + jnp.arange(ATTN_BLOCK)
        mask = k_idx[None, :] <= q_idx[:, None]
        s = jnp.where(mask[None, None], s, -jnp.inf)
        p = jax.nn.softmax(s, axis=-1).astype(v.dtype)
        return jnp.einsum('bhqk,bkhd->bqhd', p, v)

    o = lax.map(attend_block, (qn_b, qr_b, jnp.arange(nb)))
    return o.transpose(1, 0, 2, 3, 4).reshape(B, S, MLA_WIDTH)


def gmlp_mixer(u, v, v_norm_g, v_norm_b, w_s, b_s):
    B, S, _ = v.shape
    v = layernorm(v, v_norm_g, v_norm_b)
    nc = S // GMLP_CHUNK
    vc = v.reshape(B, nc, GMLP_CHUNK, GMLP_GROUPS, GMLP_GROUP_DIM)
    tril = jnp.tril(jnp.ones((GMLP_CHUNK, GMLP_CHUNK), dtype=bool))
    ws = jnp.where(tril[None], w_s, 0)
    sv = jnp.einsum('gts,bnsgc->bntgc', ws, vc) + b_s.T[None, None, :, :, None]
    return u * sv.reshape(B, S, GMLP_WIDTH)


def hybrid_layer(x, positions, norm_pre, norm_post, w_in, b_gate, pool_w, pool_scale,
                 q_norm, w_uq, kv_norm, w_ukv, v_norm_g, v_norm_b, w_s, b_s,
                 w_oa, w_ob, w_oc, w_out):
    B, S, _ = x.shape
    h = rmsnorm(x, norm_pre)
    proj = h @ w_in
    offsets = tuple(int(o) for o in np.cumsum(IN_SPLITS)[:-1])
    (a_in, a_gate, cq, ckv, kr, b_gpath, u, v, c_gate, gates) = jnp.split(proj, offsets, axis=-1)
    y_a = pool_mixer(a_in, pool_w, pool_scale) * jax.nn.silu(a_gate)
    y_b = mla_mixer(cq, ckv, kr, positions, q_norm, w_uq, kv_norm, w_ukv) * jax.nn.silu(b_gpath)
    y_c = gmlp_mixer(u, v, v_norm_g, v_norm_b, w_s, b_s) * jax.nn.silu(c_gate)
    g = jax.nn.sigmoid(gates + b_gate).reshape(B, S, N_BRANCHES, D_MODEL)
    merged = g[:, :, 0] * (y_a @ w_oa) + g[:, :, 1] * (y_b @ w_ob) + g[:, :, 2] * (y_c @ w_oc)
    return x + rmsnorm(merged @ w_out, norm_post)


def setup_inputs(seed: int = 0) -> dict:
    key = jax.random.key(seed)
    ks = jax.random.split(key, 24)
    L = DEPTH

    def nrm(k, shape, scale):
        return jax.random.normal(k, shape, jnp.float32) * scale

    x = nrm(ks[0], (BATCH, SEQ, D_MODEL), 1.0)
    positions = (jax.random.randint(ks[1], (BATCH, 1), 0, 1024, dtype=jnp.int32)
                 + jnp.arange(SEQ, dtype=jnp.int32)[None, :])
    return {
        'x': x,
        'positions': positions,
        'norm_pre': 1.0 + nrm(ks[2], (L, D_MODEL), 0.02),
        'norm_post': 1.0 + nrm(ks[3], (L, D_MODEL), 0.02),
        'w_in': nrm(ks[4], (L, D_MODEL, IN_WIDTH), D_MODEL ** -0.5),
        'b_gate': nrm(ks[5], (L, N_BRANCHES * D_MODEL), 0.01),
        'pool_w': nrm(ks[6], (L, POOL_GROUPS, POOL_GROUP_DIM, POOL_GROUP_DIM), POOL_GROUP_DIM ** -0.5),
        'pool_scale': 1.0 + nrm(ks[7], (L, POOL_WIDTH), 0.02),
        'q_norm': 1.0 + nrm(ks[8], (L, Q_LORA_RANK), 0.02),
        'w_uq': nrm(ks[9], (L, Q_LORA_RANK, MLA_HEADS * (QK_NOPE_DIM + QK_ROPE_DIM)), Q_LORA_RANK ** -0.5),
        'kv_norm': 1.0 + nrm(ks[10], (L, KV_LORA_RANK), 0.02),
        'w_ukv': nrm(ks[11], (L, KV_LORA_RANK, MLA_HEADS * (QK_NOPE_DIM + V_HEAD_DIM)), KV_LORA_RANK ** -0.5),
        'v_norm_g': 1.0 + nrm(ks[12], (L, GMLP_WIDTH), 0.02),
        'v_norm_b': nrm(ks[13], (L, GMLP_WIDTH), 0.02),
        'w_s': nrm(ks[14], (L, GMLP_GROUPS, GMLP_CHUNK, GMLP_CHUNK), GMLP_CHUNK ** -0.5),
        'b_s': 1.0 + nrm(ks[15], (L, GMLP_GROUPS, GMLP_CHUNK), 0.02),
        'w_oa': nrm(ks[16], (L, POOL_WIDTH, D_MODEL), POOL_WIDTH ** -0.5),
        'w_ob': nrm(ks[17], (L, MLA_WIDTH, D_MODEL), MLA_WIDTH ** -0.5),
        'w_oc': nrm(ks[18], (L, GMLP_WIDTH, D_MODEL), GMLP_WIDTH ** -0.5),
        'w_out': nrm(ks[19], (L, D_MODEL, D_MODEL), D_MODEL ** -0.5),
    }


def reference(x, positions, norm_pre, norm_post, w_in, b_gate, pool_w, pool_scale,
              q_norm, w_uq, kv_norm, w_ukv, v_norm_g, v_norm_b, w_s, b_s,
              w_oa, w_ob, w_oc, w_out):
    for l in range(DEPTH):
        x = hybrid_layer(x, positions, norm_pre[l], norm_post[l], w_in[l], b_gate[l],
                         pool_w[l], pool_scale[l], q_norm[l], w_uq[l], kv_norm[l], w_ukv[l],
                         v_norm_g[l], v_norm_b[l], w_s[l], b_s[l],
                         w_oa[l], w_ob[l], w_oc[l], w_out[l])
    return x
```

```python
import functools
import math

import jax
import jax.numpy as jnp
import numpy as np
from jax import lax
from jax.experimental import pallas as pl
from jax.experimental.pallas import tpu as pltpu

D_MODEL = 1024
POOL_WIDTH = 512
POOL_GROUPS = 4
POOL_GROUP_DIM = POOL_WIDTH // POOL_GROUPS
POOL_WINDOWS = (2, 4, 8, 16)
MLA_HEADS = 8
QK_NOPE_DIM = 64
QK_ROPE_DIM = 32
HALF_ROPE = QK_ROPE_DIM // 2
V_HEAD_DIM = 64
Q_LORA_RANK = 256
KV_LORA_RANK = 128
MLA_WIDTH = MLA_HEADS * V_HEAD_DIM
ROPE_THETA = 10000.0
GMLP_WIDTH = 512
GMLP_CHUNK = 128
GMLP_GROUPS = 8
GMLP_GROUP_DIM = GMLP_WIDTH // GMLP_GROUPS
N_BRANCHES = 3
EPS = 1e-6
IN_SPLITS = (POOL_WIDTH, POOL_WIDTH, Q_LORA_RANK, KV_LORA_RANK, QK_ROPE_DIM, MLA_WIDTH,
             GMLP_WIDTH, GMLP_WIDTH, GMLP_WIDTH, N_BRANCHES * D_MODEL)

V7X_LANES = 128
V7X_VMEM_BYTES = 64 * 1024 * 1024

HEAD_PAD = V7X_LANES
ROPE_LANE0 = QK_NOPE_DIM
ONES_LANE = V_HEAD_DIM
HALO = max(POOL_WINDOWS)

COL_A_IN = 0
COL_A_GATE = COL_A_IN + POOL_WIDTH
COL_CQ = COL_A_GATE + POOL_WIDTH
COL_CKV = COL_CQ + Q_LORA_RANK
COL_KR = COL_CKV + KV_LORA_RANK
COL_B_GATE = COL_KR + HEAD_PAD
COL_U = COL_B_GATE + MLA_WIDTH
COL_V = COL_U + GMLP_WIDTH
COL_C_GATE = COL_V + GMLP_WIDTH
COL_GATES = COL_C_GATE + GMLP_WIDTH
IN_WIDTH_PAD = COL_GATES + N_BRANCHES * D_MODEL

SOFTMAX_SCALE = (QK_NOPE_DIM + QK_ROPE_DIM) ** -0.5
LOG2E = math.log2(math.e)

FRONT_BLOCK = 256
ATTN_BLOCK = 512
BACK_BLOCK = 512
ROPE_BLOCK = 1024
HEADS_PER_STEP = 2

bf16 = jnp.bfloat16
f32 = jnp.float32


def _sigmoid(z):
    return 0.5 * jnp.tanh(0.5 * z) + 0.5


def _silu(z):
    return z * _sigmoid(z)


def _rms(x, g):
    return x * lax.rsqrt(jnp.mean(x * x, axis=-1, keepdims=True) + EPS) * g


def _dot(a, b):
    return jnp.dot(a, b, preferred_element_type=f32)


def _rope_table_kernel(pos_ref, invf_ref, cos_ref, sin_ref):
    lane = lax.broadcasted_iota(jnp.int32, cos_ref.shape, 1)
    ang = pos_ref[...].astype(f32) * invf_ref[...]
    c = jnp.cos(ang)
    s = jnp.sin(ang)
    rope = (lane >= ROPE_LANE0) & (lane < ROPE_LANE0 + QK_ROPE_DIM)
    first_half = lane < ROPE_LANE0 + HALF_ROPE
    cos_ref[...] = jnp.where(lane < ROPE_LANE0, 1.0, jnp.where(rope, c, 0.0))
    sin_ref[...] = jnp.where(rope, jnp.where(first_half, -s, s), 0.0)


def _rope_tables(positions):
    n = positions.size
    inv_freq = ROPE_THETA ** (-jnp.arange(0, QK_ROPE_DIM, 2, dtype=f32) / QK_ROPE_DIM)
    invf = jnp.zeros((1, HEAD_PAD), f32)
    invf = invf.at[0, ROPE_LANE0:ROPE_LANE0 + HALF_ROPE].set(inv_freq)
    invf = invf.at[0, ROPE_LANE0 + HALF_ROPE:ROPE_LANE0 + QK_ROPE_DIM].set(inv_freq)
    pos = positions.reshape(n, 1)
    out = jax.ShapeDtypeStruct((n, HEAD_PAD), f32)
    return pl.pallas_call(
        _rope_table_kernel,
        grid=(n // ROPE_BLOCK,),
        in_specs=[pl.BlockSpec((ROPE_BLOCK, 1), lambda i: (i, 0)),
                  pl.BlockSpec((1, HEAD_PAD), lambda i: (0, 0))],
        out_specs=[pl.BlockSpec((ROPE_BLOCK, HEAD_PAD), lambda i: (i, 0))] * 2,
        out_shape=[out, out],
        name="rope_tables",
    )(pos, invf)


def _front_kernel(x_ref, cos_ref, sin_ref, norm_pre_ref, w_in_ref, b_gate_ref, pool_w_ref,
                  pool_scale_ref, q_norm_ref, w_uq_ref, kv_norm_ref, w_ukv_ref, vn_g_ref,
                  vn_b_ref, w_s_ref, bs_ref, w_oa_ref, w_oc_ref,
                  q_ref, k_ref, v_ref, part_ref, g1_ref, sb_ref, abuf):
    tb = x_ref.shape[0]
    j = pl.program_id(1)

    hb = _rms(x_ref[...], norm_pre_ref[...]).astype(bf16)

    def proj(c0, width):
        return _dot(hb, w_in_ref[:, c0:c0 + width])

    @pl.when(j == 0)
    def _():
        abuf[0:HALO, :] = jnp.zeros((HALO, POOL_WIDTH), f32)

    a_in = proj(COL_A_IN, POOL_WIDTH)
    abuf[HALO:HALO + tb, :] = a_in
    t = j * tb + lax.broadcasted_iota(jnp.int32, (tb, 1), 0)
    ys = []
    for g, w in enumerate(POOL_WINDOWS):
        c0 = g * POOL_GROUP_DIM
        win = abuf[HALO:HALO + tb, c0:c0 + POOL_GROUP_DIM]
        for i in range(1, w):
            win = win + abuf[HALO - i:HALO - i + tb, c0:c0 + POOL_GROUP_DIM]
        count = jnp.minimum(t + 1, w).astype(f32)
        mixed = win / count - a_in[:, c0:c0 + POOL_GROUP_DIM]
        ys.append(_dot(mixed.astype(bf16), pool_w_ref[g]))
    abuf[0:HALO, :] = abuf[tb:tb + HALO, :]
    y_a = jnp.concatenate(ys, axis=1) * pool_scale_ref[...] * _silu(proj(COL_A_GATE, POOL_WIDTH))

    cos_t = cos_ref[...]
    sin_t = sin_ref[...]
    lane = lax.broadcasted_iota(jnp.int32, (tb, HEAD_PAD), 1)

    cqn = _rms(proj(COL_CQ, Q_LORA_RANK), q_norm_ref[...]).astype(bf16)
    q2 = _dot(cqn, w_uq_ref[...])
    swap0 = MLA_HEADS * HEAD_PAD
    for h in range(MLA_HEADS):
        plain = q2[:, h * HEAD_PAD:(h + 1) * HEAD_PAD]
        swapped = q2[:, swap0 + h * HEAD_PAD:swap0 + (h + 1) * HEAD_PAD]
        q_ref[h] = ((plain * cos_t + swapped * sin_t) * (SOFTMAX_SCALE * LOG2E)).astype(bf16)

    kr = proj(COL_KR, HEAD_PAD)
    kr_swapped = jnp.where(lane < ROPE_LANE0 + HALF_ROPE,
                           pltpu.roll(kr, HEAD_PAD - HALF_ROPE, 1), pltpu.roll(kr, HALF_ROPE, 1))
    k_rope = kr * cos_t + kr_swapped * sin_t

    ckvn = _rms(proj(COL_CKV, KV_LORA_RANK), kv_norm_ref[...]).astype(bf16)
    kv2 = _dot(ckvn, w_ukv_ref[...])
    ones_lane = jnp.where(lane == ONES_LANE, 1.0, 0.0)
    for h in range(MLA_HEADS):
        k_ref[h] = (kv2[:, h * HEAD_PAD:(h + 1) * HEAD_PAD] + k_rope).astype(bf16)
        v_ref[h] = (kv2[:, swap0 + h * HEAD_PAD:swap0 + (h + 1) * HEAD_PAD] + ones_lane).astype(bf16)

    sb_ref[...] = _silu(proj(COL_B_GATE, MLA_WIDTH))

    vv = proj(COL_V, GMLP_WIDTH)
    vc = vv - jnp.mean(vv, axis=-1, keepdims=True)
    var = jnp.mean(vc * vc, axis=-1, keepdims=True)
    vb = (vc * lax.rsqrt(var + EPS) * vn_g_ref[...] + vn_b_ref[...]).astype(bf16)
    row = lax.broadcasted_iota(jnp.int32, (GMLP_CHUNK, GMLP_CHUNK), 0)
    col = lax.broadcasted_iota(jnp.int32, (GMLP_CHUNK, GMLP_CHUNK), 1)
    ws = [jnp.where(col <= row, w_s_ref[g], jnp.zeros((), bf16)) for g in range(GMLP_GROUPS)]
    groups_per_tile = V7X_LANES // GMLP_GROUP_DIM
    assert groups_per_tile == 2
    chunks = []
    for c in range(tb // GMLP_CHUNK):
        tiles = []
        for tile in range(GMLP_WIDTH // V7X_LANES):
            rhs = vb[c * GMLP_CHUNK:(c + 1) * GMLP_CHUNK, tile * V7X_LANES:(tile + 1) * V7X_LANES]
            lo = _dot(ws[2 * tile], rhs)
            hi = _dot(ws[2 * tile + 1], rhs)
            tiles.append(jnp.where(col < GMLP_GROUP_DIM, lo, hi))
        chunks.append(jnp.concatenate(tiles, axis=1) + bs_ref[...])
    sv = jnp.concatenate(chunks, axis=0)
    y_c = proj(COL_U, GMLP_WIDTH) * sv * _silu(proj(COL_C_GATE, GMLP_WIDTH))

    def gate(i):
        return _sigmoid(proj(COL_GATES + i * D_MODEL, D_MODEL)
                        + b_gate_ref[:, i * D_MODEL:(i + 1) * D_MODEL])

    g1_ref[...] = gate(1)
    part_ref[...] = (gate(0) * _dot(y_a.astype(bf16), w_oa_ref[...])
                     + gate(2) * _dot(y_c.astype(bf16), w_oc_ref[...]))


def _const_spec(shape):
    zeros = (0,) * len(shape)
    return pl.BlockSpec(shape, lambda b, j: zeros, pipeline_mode=pl.Buffered(1))


def _front(x, cos_t, sin_t, p):
    B, S, _ = x.shape
    tb = FRONT_BLOCK
    tok = lambda width: pl.BlockSpec((None, tb, width), lambda b, j: (b, j, 0))
    heads = pl.BlockSpec((None, MLA_HEADS, tb, HEAD_PAD), lambda b, j: (b, 0, j, 0))
    consts = [p["norm_pre"], p["w_in"], p["b_gate"], p["pool_w"], p["pool_scale"], p["q_norm"],
              p["w_uq"], p["kv_norm"], p["w_ukv"], p["v_norm_g"], p["v_norm_b"], p["w_s"],
              p["bs_full"], p["w_oa"], p["w_oc"]]
    head_shape = jax.ShapeDtypeStruct((B, MLA_HEADS, S, HEAD_PAD), bf16)
    return pl.pallas_call(
        _front_kernel,
        grid=(B, S // tb),
        in_specs=[tok(D_MODEL), tok(HEAD_PAD), tok(HEAD_PAD)] + [_const_spec(c.shape) for c in consts],
        out_specs=[heads, heads, heads, tok(D_MODEL), tok(D_MODEL), tok(MLA_WIDTH)],
        out_shape=[head_shape, head_shape, head_shape,
                   jax.ShapeDtypeStruct((B, S, D_MODEL), f32),
                   jax.ShapeDtypeStruct((B, S, D_MODEL), f32),
                   jax.ShapeDtypeStruct((B, S, MLA_WIDTH), f32)],
        scratch_shapes=[pltpu.VMEM((HALO + tb, POOL_WIDTH), f32)],
        compiler_params=pltpu.CompilerParams(
            dimension_semantics=("arbitrary", "arbitrary"),
            vmem_limit_bytes=V7X_VMEM_BYTES * 7 // 8),
        name="layer_front",
    )(x, cos_t, sin_t, *consts)


def _attn_kernel(q_ref, k_ref, v_ref, o_ref):
    tq = q_ref.shape[1]
    i = pl.program_id(2)
    row = lax.broadcasted_iota(jnp.int32, (tq, tq), 0)
    col = lax.broadcasted_iota(jnp.int32, (tq, tq), 1)
    lane = lax.broadcasted_iota(jnp.int32, (tq, HEAD_PAD), 1)

    def head(hh):
        q = q_ref[hh]

        def block(j, carry, diagonal):
            m, acc = carry
            start = pl.multiple_of(j * tq, tq)
            kb = k_ref[hh, pl.ds(start, tq), :]
            vb = v_ref[hh, pl.ds(start, tq), :]
            s = lax.dot_general(q, kb, (((1,), (1,)), ((), ())), preferred_element_type=f32)
            if diagonal:
                s = jnp.where(col <= row, s, -jnp.inf)
            m_new = jnp.maximum(m, jnp.max(s, axis=-1, keepdims=True))
            p = jnp.exp2(s - m_new)
            acc = jnp.exp2(m - m_new) * acc + _dot(p.astype(bf16), vb)
            return m_new, acc

        carry = (jnp.full((tq, 1), -jnp.inf, f32), jnp.zeros((tq, HEAD_PAD), f32))
        carry = lax.fori_loop(0, i, lambda j, c: block(j, c, False), carry)
        _, acc = block(i, carry, True)
        return acc / acc[:, ONES_LANE:ONES_LANE + 1]

    o_ref[...] = jnp.where(lane < V_HEAD_DIM, head(0), pltpu.roll(head(1), V_HEAD_DIM, 1))


def _attention(q, k, v):
    B, H, S, _ = q.shape
    tq = ATTN_BLOCK
    hp = HEADS_PER_STEP
    assert hp * V_HEAD_DIM == V7X_LANES
    return pl.pallas_call(
        _attn_kernel,
        grid=(B, H // hp, S // tq),
        in_specs=[pl.BlockSpec((None, hp, tq, HEAD_PAD), lambda b, g, i: (b, g, i, 0)),
                  pl.BlockSpec((None, hp, S, HEAD_PAD), lambda b, g, i: (b, g, 0, 0)),
                  pl.BlockSpec((None, hp, S, HEAD_PAD), lambda b, g, i: (b, g, 0, 0))],
        out_specs=pl.BlockSpec((None, tq, hp * V_HEAD_DIM), lambda b, g, i: (b, i, g)),
        out_shape=jax.ShapeDtypeStruct((B, S, H * V_HEAD_DIM), f32),
        compiler_params=pltpu.CompilerParams(
            dimension_semantics=("arbitrary", "arbitrary", "arbitrary"),
            vmem_limit_bytes=V7X_VMEM_BYTES * 3 // 4),
        name="mla_attention",
    )(q, k, v)


def _back_kernel(x_ref, o_ref, sb_ref, g1_ref, part_ref, w_ob_ref, w_out_ref, norm_post_ref,
                 out_ref):
    y_b = (o_ref[...] * sb_ref[...]).astype(bf16)
    merged = part_ref[...] + g1_ref[...] * _dot(y_b, w_ob_ref[...])
    z = _dot(merged.astype(bf16), w_out_ref[...])
    out_ref[...] = x_ref[...] + _rms(z, norm_post_ref[...])


def _back(x, o, sb, g1, part, p):
    B, S, _ = x.shape
    tb = BACK_BLOCK
    tok = lambda width: pl.BlockSpec((None, tb, width), lambda b, j: (b, j, 0))
    consts = [p["w_ob"], p["w_out"], p["norm_post"]]
    return pl.pallas_call(
        _back_kernel,
        grid=(B, S // tb),
        in_specs=[tok(D_MODEL), tok(MLA_WIDTH), tok(MLA_WIDTH), tok(D_MODEL), tok(D_MODEL)]
        + [_const_spec(c.shape) for c in consts],
        out_specs=tok(D_MODEL),
        out_shape=jax.ShapeDtypeStruct((B, S, D_MODEL), f32),
        compiler_params=pltpu.CompilerParams(
            dimension_semantics=("arbitrary", "arbitrary"),
            vmem_limit_bytes=V7X_VMEM_BYTES // 2),
        name="layer_back",
    )(x, o, sb, g1, part, *consts)


def _layer_params(l, norm_pre, norm_post, w_in, b_gate, pool_w, pool_scale, q_norm, w_uq,
                  kv_norm, w_ukv, v_norm_g, v_norm_b, w_s, b_s, w_oa, w_ob, w_oc, w_out):
    offs = np.cumsum((0,) + IN_SPLITS)
    seg = [w_in[l][:, offs[i]:offs[i + 1]] for i in range(len(IN_SPLITS))]
    kr_cols = jnp.zeros((D_MODEL, HEAD_PAD), f32).at[:, ROPE_LANE0:ROPE_LANE0 + QK_ROPE_DIM].set(seg[4])
    w_in_p = jnp.concatenate(seg[:4] + [kr_cols] + seg[5:], axis=1).astype(bf16)
    assert w_in_p.shape == (D_MODEL, IN_WIDTH_PAD)

    uq = w_uq[l].reshape(Q_LORA_RANK, MLA_HEADS, QK_NOPE_DIM + QK_ROPE_DIM)
    nope, x1, x2 = uq[..., :QK_NOPE_DIM], uq[..., QK_NOPE_DIM:QK_NOPE_DIM + HALF_ROPE], uq[..., QK_NOPE_DIM + HALF_ROPE:]
    pad = jnp.zeros((Q_LORA_RANK, MLA_HEADS, HEAD_PAD - QK_NOPE_DIM - QK_ROPE_DIM), f32)
    plain = jnp.concatenate([nope, x1, x2, pad], axis=-1).reshape(Q_LORA_RANK, MLA_HEADS * HEAD_PAD)
    swapped = jnp.concatenate([jnp.zeros_like(nope), x2, x1, pad], axis=-1).reshape(Q_LORA_RANK, MLA_HEADS * HEAD_PAD)
    w_uq_p = jnp.concatenate([plain, swapped], axis=1).astype(bf16)

    ukv = w_ukv[l].reshape(KV_LORA_RANK, MLA_HEADS, QK_NOPE_DIM + V_HEAD_DIM)
    kpad = jnp.zeros((KV_LORA_RANK, MLA_HEADS, HEAD_PAD - QK_NOPE_DIM), f32)
    vpad = jnp.zeros((KV_LORA_RANK, MLA_HEADS, HEAD_PAD - V_HEAD_DIM), f32)
    k_cols = jnp.concatenate([ukv[..., :QK_NOPE_DIM], kpad], axis=-1).reshape(KV_LORA_RANK, MLA_HEADS * HEAD_PAD)
    v_cols = jnp.concatenate([ukv[..., QK_NOPE_DIM:], vpad], axis=-1).reshape(KV_LORA_RANK, MLA_HEADS * HEAD_PAD)
    w_ukv_p = jnp.concatenate([k_cols, v_cols], axis=1).astype(bf16)

    row = lambda a: a[l].reshape(1, -1)
    return {
        "norm_pre": row(norm_pre), "norm_post": row(norm_post), "w_in": w_in_p,
        "b_gate": row(b_gate), "pool_w": pool_w[l].astype(bf16), "pool_scale": row(pool_scale),
        "q_norm": row(q_norm), "w_uq": w_uq_p, "kv_norm": row(kv_norm), "w_ukv": w_ukv_p,
        "v_norm_g": row(v_norm_g), "v_norm_b": row(v_norm_b), "w_s": w_s[l].astype(bf16),
        "bs_full": jnp.repeat(b_s[l].T, GMLP_GROUP_DIM, axis=1),
        "w_oa": w_oa[l].astype(bf16), "w_ob": w_ob[l].astype(bf16), "w_oc": w_oc[l].astype(bf16),
        "w_out": w_out[l].astype(bf16),
    }


def kernel(x, positions, norm_pre, norm_post, w_in, b_gate, pool_w, pool_scale, q_norm, w_uq, kv_norm, w_ukv, v_norm_g, v_norm_b, w_s, b_s, w_oa, w_ob, w_oc, w_out):
    B, S, _ = x.shape
    cos_t, sin_t = _rope_tables(positions)
    cos_t = cos_t.reshape(B, S, HEAD_PAD)
    sin_t = sin_t.reshape(B, S, HEAD_PAD)
    for l in range(norm_pre.shape[0]):
        p = _layer_params(l, norm_pre, norm_post, w_in, b_gate, pool_w, pool_scale, q_norm, w_uq,
                          kv_norm, w_ukv, v_norm_g, v_norm_b, w_s, b_s, w_oa, w_ob, w_oc, w_out)
        q, k, v, part, g1, sb = _front(x, cos_t, sin_t, p)
        o = _attention(q, k, v)
        x = _back(x, o, sb, g1, part, p)
    return x
```

```python
import functools
import math

import jax
import jax.numpy as jnp
import numpy as np
from jax import lax
from jax.experimental import pallas as pl
from jax.experimental.pallas import tpu as pltpu

D_MODEL = 1024
POOL_WIDTH = 512
POOL_GROUPS = 4
POOL_GROUP_DIM = POOL_WIDTH // POOL_GROUPS
POOL_WINDOWS = (2, 4, 8, 16)
MLA_HEADS = 8
QK_NOPE_DIM = 64
QK_ROPE_DIM = 32
HALF_ROPE = QK_ROPE_DIM // 2
V_HEAD_DIM = 64
Q_LORA_RANK = 256
KV_LORA_RANK = 128
MLA_WIDTH = MLA_HEADS * V_HEAD_DIM
ROPE_THETA = 10000.0
GMLP_WIDTH = 512
GMLP_CHUNK = 128
GMLP_GROUPS = 8
GMLP_GROUP_DIM = GMLP_WIDTH // GMLP_GROUPS
N_BRANCHES = 3
EPS = 1e-6
IN_SPLITS = (POOL_WIDTH, POOL_WIDTH, Q_LORA_RANK, KV_LORA_RANK, QK_ROPE_DIM, MLA_WIDTH,
             GMLP_WIDTH, GMLP_WIDTH, GMLP_WIDTH, N_BRANCHES * D_MODEL)

V7X_LANES = 128
V7X_VMEM_BYTES = 64 * 1024 * 1024

HEAD_PAD = V7X_LANES
ROPE_LANE0 = QK_NOPE_DIM
ONES_LANE = V_HEAD_DIM
HALO = max(POOL_WINDOWS)

COL_A_IN = 0
COL_A_GATE = COL_A_IN + POOL_WIDTH
COL_CQ = COL_A_GATE + POOL_WIDTH
COL_CKV = COL_CQ + Q_LORA_RANK
COL_KR = COL_CKV + KV_LORA_RANK
COL_B_GATE = COL_KR + HEAD_PAD
COL_U = COL_B_GATE + MLA_WIDTH
COL_V = COL_U + GMLP_WIDTH
COL_C_GATE = COL_V + GMLP_WIDTH
COL_GATES = COL_C_GATE + GMLP_WIDTH
IN_WIDTH_PAD = COL_GATES + N_BRANCHES * D_MODEL

SOFTMAX_SCALE = (QK_NOPE_DIM + QK_ROPE_DIM) ** -0.5
LOG2E = math.log2(math.e)

FRONT_BLOCK = 256
ATTN_BLOCK = 1024
BACK_BLOCK = 512
ROPE_BLOCK = 1024
HEADS_PER_STEP = 2

bf16 = jnp.bfloat16
f32 = jnp.float32


def _sigmoid(z):
    return 0.5 * jnp.tanh(0.5 * z) + 0.5


def _silu(z):
    return z * _sigmoid(z)


def _rms(x, g):
    return x * lax.rsqrt(jnp.mean(x * x, axis=-1, keepdims=True) + EPS) * g


def _dot(a, b):
    return jnp.dot(a, b, preferred_element_type=f32)


def _rope_table_kernel(pos_ref, invf_ref, cos_ref, sin_ref):
    lane = lax.broadcasted_iota(jnp.int32, cos_ref.shape, 1)
    ang = pos_ref[...].astype(f32) * invf_ref[...]
    c = jnp.cos(ang)
    s = jnp.sin(ang)
    rope = (lane >= ROPE_LANE0) & (lane < ROPE_LANE0 + QK_ROPE_DIM)
    first_half = lane < ROPE_LANE0 + HALF_ROPE
    cos_ref[...] = jnp.where(lane < ROPE_LANE0, 1.0, jnp.where(rope, c, 0.0))
    sin_ref[...] = jnp.where(rope, jnp.where(first_half, -s, s), 0.0)


def _rope_tables(positions):
    n = positions.size
    inv_freq = ROPE_THETA ** (-jnp.arange(0, QK_ROPE_DIM, 2, dtype=f32) / QK_ROPE_DIM)
    invf = jnp.zeros((1, HEAD_PAD), f32)
    invf = invf.at[0, ROPE_LANE0:ROPE_LANE0 + HALF_ROPE].set(inv_freq)
    invf = invf.at[0, ROPE_LANE0 + HALF_ROPE:ROPE_LANE0 + QK_ROPE_DIM].set(inv_freq)
    pos = positions.reshape(n, 1)
    out = jax.ShapeDtypeStruct((n, HEAD_PAD), f32)
    return pl.pallas_call(
        _rope_table_kernel,
        grid=(n // ROPE_BLOCK,),
        in_specs=[pl.BlockSpec((ROPE_BLOCK, 1), lambda i: (i, 0)),
                  pl.BlockSpec((1, HEAD_PAD), lambda i: (0, 0))],
        out_specs=[pl.BlockSpec((ROPE_BLOCK, HEAD_PAD), lambda i: (i, 0))] * 2,
        out_shape=[out, out],
        name="rope_tables",
    )(pos, invf)


def _front_kernel(x_ref, cos_ref, sin_ref, norm_pre_ref, w_in_ref, b_gate_ref, pool_w_ref,
                  pool_scale_ref, q_norm_ref, w_uq_ref, kv_norm_ref, w_ukv_ref, vn_g_ref,
                  vn_b_ref, w_s_ref, bs_ref, w_oa_ref, w_oc_ref,
                  q_ref, k_ref, v_ref, part_ref, g1_ref, sb_ref, abuf):
    tb = x_ref.shape[0]
    j = pl.program_id(1)

    hb = _rms(x_ref[...], norm_pre_ref[...]).astype(bf16)

    def proj(c0, width):
        return _dot(hb, w_in_ref[:, c0:c0 + width])

    @pl.when(j == 0)
    def _():
        abuf[0:HALO, :] = jnp.zeros((HALO, POOL_WIDTH), f32)

    a_in = proj(COL_A_IN, POOL_WIDTH)
    abuf[HALO:HALO + tb, :] = a_in
    t = j * tb + lax.broadcasted_iota(jnp.int32, (tb, 1), 0)
    ys = []
    for g, w in enumerate(POOL_WINDOWS):
        c0 = g * POOL_GROUP_DIM
        win = abuf[HALO:HALO + tb, c0:c0 + POOL_GROUP_DIM]
        for i in range(1, w):
            win = win + abuf[HALO - i:HALO - i + tb, c0:c0 + POOL_GROUP_DIM]
        count = jnp.minimum(t + 1, w).astype(f32)
        mixed = win / count - a_in[:, c0:c0 + POOL_GROUP_DIM]
        ys.append(_dot(mixed.astype(bf16), pool_w_ref[g]))
    abuf[0:HALO, :] = abuf[tb:tb + HALO, :]
    y_a = jnp.concatenate(ys, axis=1) * pool_scale_ref[...] * _silu(proj(COL_A_GATE, POOL_WIDTH))

    cos_t = cos_ref[...]
    sin_t = sin_ref[...]
    lane = lax.broadcasted_iota(jnp.int32, (tb, HEAD_PAD), 1)

    cqn = _rms(proj(COL_CQ, Q_LORA_RANK), q_norm_ref[...]).astype(bf16)
    q2 = _dot(cqn, w_uq_ref[...])
    swap0 = MLA_HEADS * HEAD_PAD
    for h in range(MLA_HEADS):
        plain = q2[:, h * HEAD_PAD:(h + 1) * HEAD_PAD]
        swapped = q2[:, swap0 + h * HEAD_PAD:swap0 + (h + 1) * HEAD_PAD]
        q_ref[h] = ((plain * cos_t + swapped * sin_t) * (SOFTMAX_SCALE * LOG2E)).astype(bf16)

    kr = proj(COL_KR, HEAD_PAD)
    kr_swapped = jnp.where(lane < ROPE_LANE0 + HALF_ROPE,
                           pltpu.roll(kr, HEAD_PAD - HALF_ROPE, 1), pltpu.roll(kr, HALF_ROPE, 1))
    k_rope = kr * cos_t + kr_swapped * sin_t

    ckvn = _rms(proj(COL_CKV, KV_LORA_RANK), kv_norm_ref[...]).astype(bf16)
    kv2 = _dot(ckvn, w_ukv_ref[...])
    ones_lane = jnp.where(lane == ONES_LANE, 1.0, 0.0)
    for h in range(MLA_HEADS):
        k_ref[h] = (kv2[:, h * HEAD_PAD:(h + 1) * HEAD_PAD] + k_rope).astype(bf16)
        v_ref[h] = (kv2[:, swap0 + h * HEAD_PAD:swap0 + (h + 1) * HEAD_PAD] + ones_lane).astype(bf16)

    sb_ref[...] = _silu(proj(COL_B_GATE, MLA_WIDTH))

    vv = proj(COL_V, GMLP_WIDTH)
    vc = vv - jnp.mean(vv, axis=-1, keepdims=True)
    var = jnp.mean(vc * vc, axis=-1, keepdims=True)
    vb = (vc * lax.rsqrt(var + EPS) * vn_g_ref[...] + vn_b_ref[...]).astype(bf16)
    row = lax.broadcasted_iota(jnp.int32, (GMLP_CHUNK, GMLP_CHUNK), 0)
    col = lax.broadcasted_iota(jnp.int32, (GMLP_CHUNK, GMLP_CHUNK), 1)
    ws = [jnp.where(col <= row, w_s_ref[g], jnp.zeros((), bf16)) for g in range(GMLP_GROUPS)]
    groups_per_tile = V7X_LANES // GMLP_GROUP_DIM
    assert groups_per_tile == 2
    chunks = []
    for c in range(tb // GMLP_CHUNK):
        tiles = []
        for tile in range(GMLP_WIDTH // V7X_LANES):
            rhs = vb[c * GMLP_CHUNK:(c + 1) * GMLP_CHUNK, tile * V7X_LANES:(tile + 1) * V7X_LANES]
            lo = _dot(ws[2 * tile], rhs)
            hi = _dot(ws[2 * tile + 1], rhs)
            tiles.append(jnp.where(col < GMLP_GROUP_DIM, lo, hi))
        chunks.append(jnp.concatenate(tiles, axis=1) + bs_ref[...])
    sv = jnp.concatenate(chunks, axis=0)
    y_c = proj(COL_U, GMLP_WIDTH) * sv * _silu(proj(COL_C_GATE, GMLP_WIDTH))

    def gate(i):
        return _sigmoid(proj(COL_GATES + i * D_MODEL, D_MODEL)
                        + b_gate_ref[:, i * D_MODEL:(i + 1) * D_MODEL])

    g1_ref[...] = gate(1)
    part_ref[...] = (gate(0) * _dot(y_a.astype(bf16), w_oa_ref[...])
                     + gate(2) * _dot(y_c.astype(bf16), w_oc_ref[...]))


def _const_spec(shape):
    zeros = (0,) * len(shape)
    return pl.BlockSpec(shape, lambda b, j: zeros, pipeline_mode=pl.Buffered(1))


def _front(x, cos_t, sin_t, p):
    B, S, _ = x.shape
    tb = FRONT_BLOCK
    tok = lambda width: pl.BlockSpec((None, tb, width), lambda b, j: (b, j, 0))
    heads = pl.BlockSpec((None, MLA_HEADS, tb, HEAD_PAD), lambda b, j: (b, 0, j, 0))
    consts = [p["norm_pre"], p["w_in"], p["b_gate"], p["pool_w"], p["pool_scale"], p["q_norm"],
              p["w_uq"], p["kv_norm"], p["w_ukv"], p["v_norm_g"], p["v_norm_b"], p["w_s"],
              p["bs_full"], p["w_oa"], p["w_oc"]]
    head_shape = jax.ShapeDtypeStruct((B, MLA_HEADS, S, HEAD_PAD), bf16)
    return pl.pallas_call(
        _front_kernel,
        grid=(B, S // tb),
        in_specs=[tok(D_MODEL), tok(HEAD_PAD), tok(HEAD_PAD)] + [_const_spec(c.shape) for c in consts],
        out_specs=[heads, heads, heads, tok(D_MODEL), tok(D_MODEL), tok(MLA_WIDTH)],
        out_shape=[head_shape, head_shape, head_shape,
                   jax.ShapeDtypeStruct((B, S, D_MODEL), f32),
                   jax.ShapeDtypeStruct((B, S, D_MODEL), f32),
                   jax.ShapeDtypeStruct((B, S, MLA_WIDTH), f32)],
        scratch_shapes=[pltpu.VMEM((HALO + tb, POOL_WIDTH), f32)],
        compiler_params=pltpu.CompilerParams(
            dimension_semantics=("arbitrary", "arbitrary"),
            vmem_limit_bytes=V7X_VMEM_BYTES * 7 // 8),
        name="layer_front",
    )(x, cos_t, sin_t, *consts)


def _attn_kernel(q_ref, k_ref, v_ref, o_ref, s_ref, p_ref, m_ref, acc_ref):
    hp, tq, _ = q_ref.shape
    i = pl.program_id(2)
    lane = lax.broadcasted_iota(jnp.int32, (tq, HEAD_PAD), 1)

    m_ref[...] = jnp.full(m_ref.shape, -jnp.inf, f32)
    acc_ref[...] = jnp.zeros(acc_ref.shape, f32)

    def block(hh, j, diagonal):
        start = pl.multiple_of(j * tq, tq)
        kb = k_ref[hh, pl.ds(start, tq), :]
        s_ref[hh] = lax.dot_general(q_ref[hh], kb, (((1,), (1,)), ((), ())), preferred_element_type=f32)
        if diagonal:
            row = lax.broadcasted_iota(jnp.int32, (tq, tq), 0)
            col = lax.broadcasted_iota(jnp.int32, (tq, tq), 1)
            s_ref[hh] = jnp.where(col <= row, s_ref[hh], -jnp.inf)
        m_old = m_ref[hh]
        m_new = jnp.maximum(m_old, jnp.max(s_ref[hh], axis=-1, keepdims=True))
        m_ref[hh] = m_new
        for c in range(tq // V7X_LANES):
            cols = slice(c * V7X_LANES, (c + 1) * V7X_LANES)
            p_ref[hh, :, cols] = jnp.exp2(s_ref[hh, :, cols] - m_new).astype(bf16)
        vb = v_ref[hh, pl.ds(start, tq), :]
        acc_ref[hh] = jnp.exp2(m_old - m_new) * acc_ref[hh] + _dot(p_ref[hh], vb)

    def blocks(j, diagonal):
        for hh in range(hp):
            block(hh, j, diagonal)

    def body(j, carry):
        blocks(j, False)
        return carry

    lax.fori_loop(0, i, body, 0)
    blocks(i, True)
    outs = [acc_ref[hh] / acc_ref[hh, :, ONES_LANE:ONES_LANE + 1] for hh in range(hp)]
    o_ref[...] = jnp.where(lane < V_HEAD_DIM, outs[0], pltpu.roll(outs[1], V_HEAD_DIM, 1))


def _attention(q, k, v):
    B, H, S, _ = q.shape
    tq = ATTN_BLOCK
    hp = HEADS_PER_STEP
    assert hp * V_HEAD_DIM == V7X_LANES
    return pl.pallas_call(
        _attn_kernel,
        grid=(B, H // hp, S // tq),
        in_specs=[pl.BlockSpec((None, hp, tq, HEAD_PAD), lambda b, g, i: (b, g, i, 0)),
                  pl.BlockSpec((None, hp, S, HEAD_PAD), lambda b, g, i: (b, g, 0, 0)),
                  pl.BlockSpec((None, hp, S, HEAD_PAD), lambda b, g, i: (b, g, 0, 0))],
        out_specs=pl.BlockSpec((None, tq, hp * V_HEAD_DIM), lambda b, g, i: (b, i, g)),
        out_shape=jax.ShapeDtypeStruct((B, S, H * V_HEAD_DIM), f32),
        scratch_shapes=[pltpu.VMEM((hp, tq, tq), f32),
                        pltpu.VMEM((hp, tq, tq), bf16),
                        pltpu.VMEM((hp, tq, V7X_LANES), f32),
                        pltpu.VMEM((hp, tq, HEAD_PAD), f32)],
        compiler_params=pltpu.CompilerParams(
            dimension_semantics=("arbitrary", "arbitrary", "arbitrary"),
            vmem_limit_bytes=V7X_VMEM_BYTES * 3 // 4),
        name="mla_attention",
    )(q, k, v)


def _back_kernel(x_ref, o_ref, sb_ref, g1_ref, part_ref, w_ob_ref, w_out_ref, norm_post_ref,
                 out_ref):
    y_b = (o_ref[...] * sb_ref[...]).astype(bf16)
    merged = part_ref[...] + g1_ref[...] * _dot(y_b, w_ob_ref[...])
    z = _dot(merged.astype(bf16), w_out_ref[...])
    out_ref[...] = x_ref[...] + _rms(z, norm_post_ref[...])


def _back(x, o, sb, g1, part, p):
    B, S, _ = x.shape
    tb = BACK_BLOCK
    tok = lambda width: pl.BlockSpec((None, tb, width), lambda b, j: (b, j, 0))
    consts = [p["w_ob"], p["w_out"], p["norm_post"]]
    return pl.pallas_call(
        _back_kernel,
        grid=(B, S // tb),
        in_specs=[tok(D_MODEL), tok(MLA_WIDTH), tok(MLA_WIDTH), tok(D_MODEL), tok(D_MODEL)]
        + [_const_spec(c.shape) for c in consts],
        out_specs=tok(D_MODEL),
        out_shape=jax.ShapeDtypeStruct((B, S, D_MODEL), f32),
        compiler_params=pltpu.CompilerParams(
            dimension_semantics=("arbitrary", "arbitrary"),
            vmem_limit_bytes=V7X_VMEM_BYTES // 2),
        name="layer_back",
    )(x, o, sb, g1, part, *consts)


def _layer_params(l, norm_pre, norm_post, w_in, b_gate, pool_w, pool_scale, q_norm, w_uq,
                  kv_norm, w_ukv, v_norm_g, v_norm_b, w_s, b_s, w_oa, w_ob, w_oc, w_out):
    offs = np.cumsum((0,) + IN_SPLITS)
    seg = [w_in[l][:, offs[i]:offs[i + 1]] for i in range(len(IN_SPLITS))]
    kr_cols = jnp.zeros((D_MODEL, HEAD_PAD), f32).at[:, ROPE_LANE0:ROPE_LANE0 + QK_ROPE_DIM].set(seg[4])
    w_in_p = jnp.concatenate(seg[:4] + [kr_cols] + seg[5:], axis=1).astype(bf16)
    assert w_in_p.shape == (D_MODEL, IN_WIDTH_PAD)

    uq = w_uq[l].reshape(Q_LORA_RANK, MLA_HEADS, QK_NOPE_DIM + QK_ROPE_DIM)
    nope, x1, x2 = uq[..., :QK_NOPE_DIM], uq[..., QK_NOPE_DIM:QK_NOPE_DIM + HALF_ROPE], uq[..., QK_NOPE_DIM + HALF_ROPE:]
    pad = jnp.zeros((Q_LORA_RANK, MLA_HEADS, HEAD_PAD - QK_NOPE_DIM - QK_ROPE_DIM), f32)
    plain = jnp.concatenate([nope, x1, x2, pad], axis=-1).reshape(Q_LORA_RANK, MLA_HEADS * HEAD_PAD)
    swapped = jnp.concatenate([jnp.zeros_like(nope), x2, x1, pad], axis=-1).reshape(Q_LORA_RANK, MLA_HEADS * HEAD_PAD)
    w_uq_p = jnp.concatenate([plain, swapped], axis=1).astype(bf16)

    ukv = w_ukv[l].reshape(KV_LORA_RANK, MLA_HEADS, QK_NOPE_DIM + V_HEAD_DIM)
    kpad = jnp.zeros((KV_LORA_RANK, MLA_HEADS, HEAD_PAD - QK_NOPE_DIM), f32)
    vpad = jnp.zeros((KV_LORA_RANK, MLA_HEADS, HEAD_PAD - V_HEAD_DIM), f32)
    k_cols = jnp.concatenate([ukv[..., :QK_NOPE_DIM], kpad], axis=-1).reshape(KV_LORA_RANK, MLA_HEADS * HEAD_PAD)
    v_cols = jnp.concatenate([ukv[..., QK_NOPE_DIM:], vpad], axis=-1).reshape(KV_LORA_RANK, MLA_HEADS * HEAD_PAD)
    w_ukv_p = jnp.concatenate([k_cols, v_cols], axis=1).astype(bf16)

    row = lambda a: a[l].reshape(1, -1)
    return {
        "norm_pre": row(norm_pre), "norm_post": row(norm_post), "w_in": w_in_p,
        "b_gate": row(b_gate), "pool_w": pool_w[l].astype(bf16), "pool_scale": row(pool_scale),
        "q_norm": row(q_norm), "w_uq": w_uq_p, "kv_norm": row(kv_norm), "w_ukv": w_ukv_p,
        "v_norm_g": row(v_norm_g), "v_norm_b": row(v_norm_b), "w_s": w_s[l].astype(bf16),
        "bs_full": jnp.repeat(b_s[l].T, GMLP_GROUP_DIM, axis=1),
        "w_oa": w_oa[l].astype(bf16), "w_ob": w_ob[l].astype(bf16), "w_oc": w_oc[l].astype(bf16),
        "w_out": w_out[l].astype(bf16),
    }


def kernel(x, positions, norm_pre, norm_post, w_in, b_gate, pool_w, pool_scale, q_norm, w_uq, kv_norm, w_ukv, v_norm_g, v_norm_b, w_s, b_s, w_oa, w_ob, w_oc, w_out):
    B, S, _ = x.shape
    cos_t, sin_t = _rope_tables(positions)
    cos_t = cos_t.reshape(B, S, HEAD_PAD)
    sin_t = sin_t.reshape(B, S, HEAD_PAD)
    for l in range(norm_pre.shape[0]):
        p = _layer_params(l, norm_pre, norm_post, w_in, b_gate, pool_w, pool_scale, q_norm, w_uq,
                          kv_norm, w_ukv, v_norm_g, v_norm_b, w_s, b_s, w_oa, w_ob, w_oc, w_out)
        q, k, v, part, g1, sb = _front(x, cos_t, sin_t, p)
        o = _attention(q, k, v)
        x = _back(x, o, sb, g1, part, p)
    return x
```

```python
import functools
import math

import jax
import jax.numpy as jnp
import numpy as np
from jax import lax
from jax.experimental import pallas as pl
from jax.experimental.pallas import tpu as pltpu

D_MODEL = 1024
POOL_WIDTH = 512
POOL_GROUPS = 4
POOL_GROUP_DIM = POOL_WIDTH // POOL_GROUPS
POOL_WINDOWS = (2, 4, 8, 16)
MLA_HEADS = 8
QK_NOPE_DIM = 64
QK_ROPE_DIM = 32
HALF_ROPE = QK_ROPE_DIM // 2
V_HEAD_DIM = 64
Q_LORA_RANK = 256
KV_LORA_RANK = 128
MLA_WIDTH = MLA_HEADS * V_HEAD_DIM
ROPE_THETA = 10000.0
GMLP_WIDTH = 512
GMLP_CHUNK = 128
GMLP_GROUPS = 8
GMLP_GROUP_DIM = GMLP_WIDTH // GMLP_GROUPS
N_BRANCHES = 3
EPS = 1e-6
IN_SPLITS = (POOL_WIDTH, POOL_WIDTH, Q_LORA_RANK, KV_LORA_RANK, QK_ROPE_DIM, MLA_WIDTH,
             GMLP_WIDTH, GMLP_WIDTH, GMLP_WIDTH, N_BRANCHES * D_MODEL)

V7X_LANES = 128
V7X_VMEM_BYTES = 64 * 1024 * 1024

HEAD_PAD = V7X_LANES
ROPE_LANE0 = QK_NOPE_DIM
ONES_LANE = V_HEAD_DIM
HALO = max(POOL_WINDOWS)

COL_A_IN = 0
COL_A_GATE = COL_A_IN + POOL_WIDTH
COL_CQ = COL_A_GATE + POOL_WIDTH
COL_CKV = COL_CQ + Q_LORA_RANK
COL_KR = COL_CKV + KV_LORA_RANK
COL_B_GATE = COL_KR + HEAD_PAD
COL_U = COL_B_GATE + MLA_WIDTH
COL_V = COL_U + GMLP_WIDTH
COL_C_GATE = COL_V + GMLP_WIDTH
COL_GATES = COL_C_GATE + GMLP_WIDTH
IN_WIDTH_PAD = COL_GATES + N_BRANCHES * D_MODEL

SOFTMAX_SCALE = (QK_NOPE_DIM + QK_ROPE_DIM) ** -0.5
LOG2E = math.log2(math.e)

FRONT_BLOCK = 512
ATTN_BLOCK = 1024
BACK_BLOCK = 512
ROPE_BLOCK = 1024
HEADS_PER_STEP = 2

bf16 = jnp.bfloat16
f32 = jnp.float32


def _sigmoid(z):
    return 0.5 * jnp.tanh(0.5 * z) + 0.5


def _silu(z):
    return z * _sigmoid(z)


def _rms(x, g):
    return x * lax.rsqrt(jnp.mean(x * x, axis=-1, keepdims=True) + EPS) * g


def _dot(a, b):
    return jnp.dot(a, b, preferred_element_type=f32)


def _rope_table_kernel(pos_ref, invf_ref, cos_ref, sin_ref):
    lane = lax.broadcasted_iota(jnp.int32, cos_ref.shape, 1)
    ang = pos_ref[...].astype(f32) * invf_ref[...]
    c = jnp.cos(ang)
    s = jnp.sin(ang)
    rope = (lane >= ROPE_LANE0) & (lane < ROPE_LANE0 + QK_ROPE_DIM)
    first_half = lane < ROPE_LANE0 + HALF_ROPE
    cos_ref[...] = jnp.where(lane < ROPE_LANE0, 1.0, jnp.where(rope, c, 0.0))
    sin_ref[...] = jnp.where(rope, jnp.where(first_half, -s, s), 0.0)


def _rope_tables(positions):
    n = positions.size
    inv_freq = ROPE_THETA ** (-jnp.arange(0, QK_ROPE_DIM, 2, dtype=f32) / QK_ROPE_DIM)
    invf = jnp.zeros((1, HEAD_PAD), f32)
    invf = invf.at[0, ROPE_LANE0:ROPE_LANE0 + HALF_ROPE].set(inv_freq)
    invf = invf.at[0, ROPE_LANE0 + HALF_ROPE:ROPE_LANE0 + QK_ROPE_DIM].set(inv_freq)
    pos = positions.reshape(n, 1)
    out = jax.ShapeDtypeStruct((n, HEAD_PAD), f32)
    return pl.pallas_call(
        _rope_table_kernel,
        grid=(n // ROPE_BLOCK,),
        in_specs=[pl.BlockSpec((ROPE_BLOCK, 1), lambda i: (i, 0)),
                  pl.BlockSpec((1, HEAD_PAD), lambda i: (0, 0))],
        out_specs=[pl.BlockSpec((ROPE_BLOCK, HEAD_PAD), lambda i: (i, 0))] * 2,
        out_shape=[out, out],
        name="rope_tables",
    )(pos, invf)


def _front_kernel(x_ref, cos_ref, sin_ref, norm_pre_ref, w_in_ref, b_gate_ref, pool_w_ref,
                  pool_scale_ref, q_norm_ref, w_uq_ref, kv_norm_ref, w_ukv_ref, vn_g_ref,
                  vn_b_ref, w_s_ref, bs_ref, w_oa_ref, w_oc_ref,
                  q_ref, k_ref, v_ref, part_ref, g1_ref, sb_ref, abuf):
    tb = x_ref.shape[0]
    j = pl.program_id(1)

    hb = _rms(x_ref[...], norm_pre_ref[...]).astype(bf16)

    def proj(c0, width):
        return _dot(hb, w_in_ref[:, c0:c0 + width])

    @pl.when(j == 0)
    def _():
        abuf[0:HALO, :] = jnp.zeros((HALO, POOL_WIDTH), f32)

    a_in = proj(COL_A_IN, POOL_WIDTH)
    abuf[HALO:HALO + tb, :] = a_in
    t = j * tb + lax.broadcasted_iota(jnp.int32, (tb, 1), 0)
    ys = []
    for g, w in enumerate(POOL_WINDOWS):
        c0 = g * POOL_GROUP_DIM
        win = abuf[HALO:HALO + tb, c0:c0 + POOL_GROUP_DIM]
        for i in range(1, w):
            win = win + abuf[HALO - i:HALO - i + tb, c0:c0 + POOL_GROUP_DIM]
        count = jnp.minimum(t + 1, w).astype(f32)
        mixed = win / count - a_in[:, c0:c0 + POOL_GROUP_DIM]
        ys.append(_dot(mixed.astype(bf16), pool_w_ref[g]))
    abuf[0:HALO, :] = abuf[tb:tb + HALO, :]
    y_a = jnp.concatenate(ys, axis=1) * pool_scale_ref[...] * _silu(proj(COL_A_GATE, POOL_WIDTH))

    cos_t = cos_ref[...]
    sin_t = sin_ref[...]
    lane = lax.broadcasted_iota(jnp.int32, (tb, HEAD_PAD), 1)

    cqn = _rms(proj(COL_CQ, Q_LORA_RANK), q_norm_ref[...]).astype(bf16)
    q2 = _dot(cqn, w_uq_ref[...])
    swap0 = MLA_HEADS * HEAD_PAD
    for h in range(MLA_HEADS):
        plain = q2[:, h * HEAD_PAD:(h + 1) * HEAD_PAD]
        swapped = q2[:, swap0 + h * HEAD_PAD:swap0 + (h + 1) * HEAD_PAD]
        q_ref[h] = ((plain * cos_t + swapped * sin_t) * (SOFTMAX_SCALE * LOG2E)).astype(bf16)

    kr = proj(COL_KR, HEAD_PAD)
    kr_swapped = jnp.where(lane < ROPE_LANE0 + HALF_ROPE,
                           pltpu.roll(kr, HEAD_PAD - HALF_ROPE, 1), pltpu.roll(kr, HALF_ROPE, 1))
    k_rope = kr * cos_t + kr_swapped * sin_t

    ckvn = _rms(proj(COL_CKV, KV_LORA_RANK), kv_norm_ref[...]).astype(bf16)
    kv2 = _dot(ckvn, w_ukv_ref[...])
    ones_lane = jnp.where(lane == ONES_LANE, 1.0, 0.0)
    for h in range(MLA_HEADS):
        k_ref[h] = (kv2[:, h * HEAD_PAD:(h + 1) * HEAD_PAD] + k_rope).astype(bf16)
        v_ref[h] = (kv2[:, swap0 + h * HEAD_PAD:swap0 + (h + 1) * HEAD_PAD] + ones_lane).astype(bf16)

    sb_ref[...] = _silu(proj(COL_B_GATE, MLA_WIDTH)).astype(sb_ref.dtype)

    vv = proj(COL_V, GMLP_WIDTH)
    vc = vv - jnp.mean(vv, axis=-1, keepdims=True)
    var = jnp.mean(vc * vc, axis=-1, keepdims=True)
    vb = (vc * lax.rsqrt(var + EPS) * vn_g_ref[...] + vn_b_ref[...]).astype(bf16)
    row = lax.broadcasted_iota(jnp.int32, (GMLP_CHUNK, GMLP_CHUNK), 0)
    col = lax.broadcasted_iota(jnp.int32, (GMLP_CHUNK, GMLP_CHUNK), 1)
    ws = [jnp.where(col <= row, w_s_ref[g], jnp.zeros((), bf16)) for g in range(GMLP_GROUPS)]
    groups_per_tile = V7X_LANES // GMLP_GROUP_DIM
    assert groups_per_tile == 2
    chunks = []
    for c in range(tb // GMLP_CHUNK):
        tiles = []
        for tile in range(GMLP_WIDTH // V7X_LANES):
            rhs = vb[c * GMLP_CHUNK:(c + 1) * GMLP_CHUNK, tile * V7X_LANES:(tile + 1) * V7X_LANES]
            lo = _dot(ws[2 * tile], rhs)
            hi = _dot(ws[2 * tile + 1], rhs)
            tiles.append(jnp.where(col < GMLP_GROUP_DIM, lo, hi))
        chunks.append(jnp.concatenate(tiles, axis=1) + bs_ref[...])
    sv = jnp.concatenate(chunks, axis=0)
    y_c = proj(COL_U, GMLP_WIDTH) * sv * _silu(proj(COL_C_GATE, GMLP_WIDTH))

    def gate(i):
        return _sigmoid(proj(COL_GATES + i * D_MODEL, D_MODEL)
                        + b_gate_ref[:, i * D_MODEL:(i + 1) * D_MODEL])

    g1_ref[...] = gate(1).astype(g1_ref.dtype)
    part_ref[...] = (gate(0) * _dot(y_a.astype(bf16), w_oa_ref[...])
                     + gate(2) * _dot(y_c.astype(bf16), w_oc_ref[...])).astype(part_ref.dtype)


def _const_spec(shape):
    zeros = (0,) * len(shape)
    return pl.BlockSpec(shape, lambda b, j: zeros, pipeline_mode=pl.Buffered(1))


def _front(x, cos_t, sin_t, p):
    B, S, _ = x.shape
    tb = FRONT_BLOCK
    tok = lambda width: pl.BlockSpec((None, tb, width), lambda b, j: (b, j, 0))
    heads = pl.BlockSpec((None, MLA_HEADS, tb, HEAD_PAD), lambda b, j: (b, 0, j, 0))
    consts = [p["norm_pre"], p["w_in"], p["b_gate"], p["pool_w"], p["pool_scale"], p["q_norm"],
              p["w_uq"], p["kv_norm"], p["w_ukv"], p["v_norm_g"], p["v_norm_b"], p["w_s"],
              p["bs_full"], p["w_oa"], p["w_oc"]]
    head_shape = jax.ShapeDtypeStruct((B, MLA_HEADS, S, HEAD_PAD), bf16)
    return pl.pallas_call(
        _front_kernel,
        grid=(B, S // tb),
        in_specs=[tok(D_MODEL), tok(HEAD_PAD), tok(HEAD_PAD)] + [_const_spec(c.shape) for c in consts],
        out_specs=[heads, heads, heads, tok(D_MODEL), tok(D_MODEL), tok(MLA_WIDTH)],
        out_shape=[head_shape, head_shape, head_shape,
                   jax.ShapeDtypeStruct((B, S, D_MODEL), bf16),
                   jax.ShapeDtypeStruct((B, S, D_MODEL), bf16),
                   jax.ShapeDtypeStruct((B, S, MLA_WIDTH), bf16)],
        scratch_shapes=[pltpu.VMEM((HALO + tb, POOL_WIDTH), f32)],
        compiler_params=pltpu.CompilerParams(
            dimension_semantics=("arbitrary", "arbitrary"),
            vmem_limit_bytes=V7X_VMEM_BYTES * 7 // 8),
        name="layer_front",
    )(x, cos_t, sin_t, *consts)


def _attn_kernel(q_ref, k_ref, v_ref, o_ref, s_ref, p_ref, m_ref, acc_ref):
    hp, tq, _ = q_ref.shape
    i = pl.program_id(2)
    lane = lax.broadcasted_iota(jnp.int32, (tq, HEAD_PAD), 1)

    m_ref[...] = jnp.full(m_ref.shape, -jnp.inf, f32)
    acc_ref[...] = jnp.zeros(acc_ref.shape, f32)

    def scores(hh, j, slot):
        start = pl.multiple_of(j * tq, tq)
        kb = k_ref[hh, pl.ds(start, tq), :]
        s_ref[slot, hh] = lax.dot_general(q_ref[hh], kb, (((1,), (1,)), ((), ())),
                                          preferred_element_type=f32)

    def softmax_pv(hh, j, slot, diagonal):
        def tile(c):
            cols = slice(c * V7X_LANES, (c + 1) * V7X_LANES)
            s = s_ref[slot, hh, :, cols]
            if diagonal:
                row = lax.broadcasted_iota(jnp.int32, (tq, V7X_LANES), 0)
                s = jnp.where(c * V7X_LANES + lane <= row, s, -jnp.inf)
            return s

        n_tiles = tq // V7X_LANES
        m_blk = tile(0)
        for c in range(1, n_tiles):
            m_blk = jnp.maximum(m_blk, tile(c))
        m_old = m_ref[hh]
        m_new = jnp.maximum(m_old, jnp.max(m_blk, axis=-1, keepdims=True))
        m_ref[hh] = m_new
        for c in range(n_tiles):
            p_ref[hh, :, c * V7X_LANES:(c + 1) * V7X_LANES] = jnp.exp2(tile(c) - m_new).astype(bf16)
        vb = v_ref[hh, pl.ds(pl.multiple_of(j * tq, tq), tq), :]
        acc_ref[hh] = jnp.exp2(m_old - m_new) * acc_ref[hh] + _dot(p_ref[hh], vb)

    for hh in range(hp):
        scores(hh, 0, 0)

    def step(j, slot, diagonal):
        for hh in range(hp):
            softmax_pv(hh, j, slot, diagonal)
            if not diagonal:
                scores(hh, j + 1, 1 - slot)

    def body(t, carry):
        step(2 * t, 0, False)
        step(2 * t + 1, 1, False)
        return carry

    lax.fori_loop(0, i // 2, body, 0)

    @pl.when(i % 2 == 1)
    def _():
        step(i - 1, 0, False)
        step(i, 1, True)

    @pl.when(i % 2 == 0)
    def _():
        step(i, 0, True)
    outs = [acc_ref[hh] / acc_ref[hh, :, ONES_LANE:ONES_LANE + 1] for hh in range(hp)]
    o_ref[...] = jnp.where(lane < V_HEAD_DIM, outs[0],
                           pltpu.roll(outs[1], V_HEAD_DIM, 1)).astype(o_ref.dtype)


def _attention(q, k, v):
    B, H, S, _ = q.shape
    tq = ATTN_BLOCK
    hp = HEADS_PER_STEP
    assert hp * V_HEAD_DIM == V7X_LANES
    return pl.pallas_call(
        _attn_kernel,
        grid=(B, H // hp, S // tq),
        in_specs=[pl.BlockSpec((None, hp, tq, HEAD_PAD), lambda b, g, i: (b, g, i, 0)),
                  pl.BlockSpec((None, hp, S, HEAD_PAD), lambda b, g, i: (b, g, 0, 0)),
                  pl.BlockSpec((None, hp, S, HEAD_PAD), lambda b, g, i: (b, g, 0, 0))],
        out_specs=pl.BlockSpec((None, tq, hp * V_HEAD_DIM), lambda b, g, i: (b, i, g)),
        out_shape=jax.ShapeDtypeStruct((B, S, H * V_HEAD_DIM), bf16),
        scratch_shapes=[pltpu.VMEM((2, hp, tq, tq), f32),
                        pltpu.VMEM((hp, tq, tq), bf16),
                        pltpu.VMEM((hp, tq, V7X_LANES), f32),
                        pltpu.VMEM((hp, tq, HEAD_PAD), f32)],
        compiler_params=pltpu.CompilerParams(
            dimension_semantics=("arbitrary", "arbitrary", "arbitrary"),
            vmem_limit_bytes=V7X_VMEM_BYTES * 3 // 4),
        name="mla_attention",
    )(q, k, v)


def _back_kernel(x_ref, o_ref, sb_ref, g1_ref, part_ref, w_ob_ref, w_out_ref, norm_post_ref,
                 out_ref):
    y_b = o_ref[...] * sb_ref[...]
    merged = part_ref[...].astype(f32) + g1_ref[...].astype(f32) * _dot(y_b, w_ob_ref[...])
    z = _dot(merged.astype(bf16), w_out_ref[...])
    out_ref[...] = x_ref[...] + _rms(z, norm_post_ref[...])


def _back(x, o, sb, g1, part, p):
    B, S, _ = x.shape
    tb = BACK_BLOCK
    tok = lambda width: pl.BlockSpec((None, tb, width), lambda b, j: (b, j, 0))
    consts = [p["w_ob"], p["w_out"], p["norm_post"]]
    return pl.pallas_call(
        _back_kernel,
        grid=(B, S // tb),
        in_specs=[tok(D_MODEL), tok(MLA_WIDTH), tok(MLA_WIDTH), tok(D_MODEL), tok(D_MODEL)]
        + [_const_spec(c.shape) for c in consts],
        out_specs=tok(D_MODEL),
        out_shape=jax.ShapeDtypeStruct((B, S, D_MODEL), f32),
        compiler_params=pltpu.CompilerParams(
            dimension_semantics=("arbitrary", "arbitrary"),
            vmem_limit_bytes=V7X_VMEM_BYTES // 2),
        name="layer_back",
    )(x, o, sb, g1, part, *consts)


def _layer_params(l, norm_pre, norm_post, w_in, b_gate, pool_w, pool_scale, q_norm, w_uq,
                  kv_norm, w_ukv, v_norm_g, v_norm_b, w_s, b_s, w_oa, w_ob, w_oc, w_out):
    offs = np.cumsum((0,) + IN_SPLITS)
    seg = [w_in[l][:, offs[i]:offs[i + 1]] for i in range(len(IN_SPLITS))]
    kr_cols = jnp.zeros((D_MODEL, HEAD_PAD), f32).at[:, ROPE_LANE0:ROPE_LANE0 + QK_ROPE_DIM].set(seg[4])
    w_in_p = jnp.concatenate(seg[:4] + [kr_cols] + seg[5:], axis=1).astype(bf16)
    assert w_in_p.shape == (D_MODEL, IN_WIDTH_PAD)

    uq = w_uq[l].reshape(Q_LORA_RANK, MLA_HEADS, QK_NOPE_DIM + QK_ROPE_DIM)
    nope, x1, x2 = uq[..., :QK_NOPE_DIM], uq[..., QK_NOPE_DIM:QK_NOPE_DIM + HALF_ROPE], uq[..., QK_NOPE_DIM + HALF_ROPE:]
    pad = jnp.zeros((Q_LORA_RANK, MLA_HEADS, HEAD_PAD - QK_NOPE_DIM - QK_ROPE_DIM), f32)
    plain = jnp.concatenate([nope, x1, x2, pad], axis=-1).reshape(Q_LORA_RANK, MLA_HEADS * HEAD_PAD)
    swapped = jnp.concatenate([jnp.zeros_like(nope), x2, x1, pad], axis=-1).reshape(Q_LORA_RANK, MLA_HEADS * HEAD_PAD)
    w_uq_p = jnp.concatenate([plain, swapped], axis=1).astype(bf16)

    ukv = w_ukv[l].reshape(KV_LORA_RANK, MLA_HEADS, QK_NOPE_DIM + V_HEAD_DIM)
    kpad = jnp.zeros((KV_LORA_RANK, MLA_HEADS, HEAD_PAD - QK_NOPE_DIM), f32)
    vpad = jnp.zeros((KV_LORA_RANK, MLA_HEADS, HEAD_PAD - V_HEAD_DIM), f32)
    k_cols = jnp.concatenate([ukv[..., :QK_NOPE_DIM], kpad], axis=-1).reshape(KV_LORA_RANK, MLA_HEADS * HEAD_PAD)
    v_cols = jnp.concatenate([ukv[..., QK_NOPE_DIM:], vpad], axis=-1).reshape(KV_LORA_RANK, MLA_HEADS * HEAD_PAD)
    w_ukv_p = jnp.concatenate([k_cols, v_cols], axis=1).astype(bf16)

    row = lambda a: a[l].reshape(1, -1)
    return {
        "norm_pre": row(norm_pre), "norm_post": row(norm_post), "w_in": w_in_p,
        "b_gate": row(b_gate), "pool_w": pool_w[l].astype(bf16), "pool_scale": row(pool_scale),
        "q_norm": row(q_norm), "w_uq": w_uq_p, "kv_norm": row(kv_norm), "w_ukv": w_ukv_p,
        "v_norm_g": row(v_norm_g), "v_norm_b": row(v_norm_b), "w_s": w_s[l].astype(bf16),
        "bs_full": jnp.repeat(b_s[l].T, GMLP_GROUP_DIM, axis=1),
        "w_oa": w_oa[l].astype(bf16), "w_ob": w_ob[l].astype(bf16), "w_oc": w_oc[l].astype(bf16),
        "w_out": w_out[l].astype(bf16),
    }


def kernel(x, positions, norm_pre, norm_post, w_in, b_gate, pool_w, pool_scale, q_norm, w_uq, kv_norm, w_ukv, v_norm_g, v_norm_b, w_s, b_s, w_oa, w_ob, w_oc, w_out):
    B, S, _ = x.shape
    cos_t, sin_t = _rope_tables(positions)
    cos_t = cos_t.reshape(B, S, HEAD_PAD)
    sin_t = sin_t.reshape(B, S, HEAD_PAD)
    for l in range(norm_pre.shape[0]):
        p = _layer_params(l, norm_pre, norm_post, w_in, b_gate, pool_w, pool_scale, q_norm, w_uq,
                          kv_norm, w_ukv, v_norm_g, v_norm_b, w_s, b_s, w_oa, w_ob, w_oc, w_out)
        q, k, v, part, g1, sb = _front(x, cos_t, sin_t, p)
        o = _attention(q, k, v)
        x = _back(x, o, sb, g1, part, p)
    return x
```

```python
import functools
import math

import jax
import jax.numpy as jnp
import numpy as np
from jax import lax
from jax.experimental import pallas as pl
from jax.experimental.pallas import tpu as pltpu

D_MODEL = 1024
POOL_WIDTH = 512
POOL_GROUPS = 4
POOL_GROUP_DIM = POOL_WIDTH // POOL_GROUPS
POOL_WINDOWS = (2, 4, 8, 16)
MLA_HEADS = 8
QK_NOPE_DIM = 64
QK_ROPE_DIM = 32
HALF_ROPE = QK_ROPE_DIM // 2
V_HEAD_DIM = 64
Q_LORA_RANK = 256
KV_LORA_RANK = 128
MLA_WIDTH = MLA_HEADS * V_HEAD_DIM
ROPE_THETA = 10000.0
GMLP_WIDTH = 512
GMLP_CHUNK = 128
GMLP_GROUPS = 8
GMLP_GROUP_DIM = GMLP_WIDTH // GMLP_GROUPS
N_BRANCHES = 3
EPS = 1e-6
IN_SPLITS = (POOL_WIDTH, POOL_WIDTH, Q_LORA_RANK, KV_LORA_RANK, QK_ROPE_DIM, MLA_WIDTH,
             GMLP_WIDTH, GMLP_WIDTH, GMLP_WIDTH, N_BRANCHES * D_MODEL)

V7X_LANES = 128
V7X_VMEM_BYTES = 64 * 1024 * 1024

HEAD_PAD = V7X_LANES
ROPE_LANE0 = QK_NOPE_DIM
ONES_LANE = V_HEAD_DIM
HALO = max(POOL_WINDOWS)

COL_A_IN = 0
COL_A_GATE = COL_A_IN + POOL_WIDTH
COL_CQ = COL_A_GATE + POOL_WIDTH
COL_CKV = COL_CQ + Q_LORA_RANK
COL_KR = COL_CKV + KV_LORA_RANK
COL_B_GATE = COL_KR + HEAD_PAD
COL_U = COL_B_GATE + MLA_WIDTH
COL_V = COL_U + GMLP_WIDTH
COL_C_GATE = COL_V + GMLP_WIDTH
COL_GATES = COL_C_GATE + GMLP_WIDTH
IN_WIDTH_PAD = COL_GATES + N_BRANCHES * D_MODEL

SOFTMAX_SCALE = (QK_NOPE_DIM + QK_ROPE_DIM) ** -0.5
LOG2E = math.log2(math.e)

FRONT_BLOCK = 512
ATTN_BLOCK = 1024
BACK_BLOCK = 512
HEADS_PER_STEP = 2

bf16 = jnp.bfloat16
f32 = jnp.float32


def _sigmoid(z):
    return 0.5 * jnp.tanh(0.5 * z) + 0.5


def _silu(z):
    return z * _sigmoid(z)


def _rms(x, g):
    return x * lax.rsqrt(jnp.mean(x * x, axis=-1, keepdims=True) + EPS) * g


def _dot(a, b):
    return jnp.dot(a, b, preferred_element_type=f32)


def _rope_table_kernel(pos_ref, invf_ref, cos_ref, sin_ref):
    ang = pos_ref[...].astype(f32) * invf_ref[...]
    cos_ref[...] = jnp.cos(ang)
    sin_ref[...] = jnp.sin(ang)


def _rope_tables(positions):
    n = positions.size
    per_row = V7X_LANES // HALF_ROPE
    inv_freq = ROPE_THETA ** (-jnp.arange(0, QK_ROPE_DIM, 2, dtype=f32) / QK_ROPE_DIM)
    invf = jnp.tile(inv_freq, per_row).reshape(1, V7X_LANES)
    pos = jnp.repeat(positions.reshape(n // per_row, per_row), HALF_ROPE, axis=1)
    dense = jax.ShapeDtypeStruct((n // per_row, V7X_LANES), f32)
    cos_d, sin_d = pl.pallas_call(
        _rope_table_kernel,
        out_shape=[dense, dense],
        name="rope_tables",
    )(pos, invf)
    c = cos_d.reshape(n, HALF_ROPE)
    s = sin_d.reshape(n, HALF_ROPE)
    pad = jnp.zeros((n, HEAD_PAD - ROPE_LANE0 - QK_ROPE_DIM), f32)
    cos_t = jnp.concatenate([jnp.ones((n, ROPE_LANE0), f32), c, c, pad], axis=1)
    sin_t = jnp.concatenate([jnp.zeros((n, ROPE_LANE0), f32), -s, s, pad], axis=1)
    return cos_t, sin_t


def _front_kernel(x_ref, cos_ref, sin_ref, norm_pre_ref, w_in_ref, b_gate_ref, pool_w_ref,
                  pool_scale_ref, q_norm_ref, w_uq_ref, kv_norm_ref, w_ukv_ref, vn_g_ref,
                  vn_b_ref, w_s_ref, bs_ref, w_oa_ref, w_oc_ref,
                  q_ref, k_ref, v_ref, part_ref, g1_ref, sb_ref, abuf):
    tb = x_ref.shape[0]
    j = pl.program_id(1)

    hb = _rms(x_ref[...], norm_pre_ref[...]).astype(bf16)

    def proj(c0, width):
        return _dot(hb, w_in_ref[:, c0:c0 + width])

    @pl.when(j == 0)
    def _():
        abuf[0:HALO, :] = jnp.zeros((HALO, POOL_WIDTH), f32)

    a_in = proj(COL_A_IN, POOL_WIDTH)
    abuf[HALO:HALO + tb, :] = a_in
    t = j * tb + lax.broadcasted_iota(jnp.int32, (tb, 1), 0)
    ys = []
    for g, w in enumerate(POOL_WINDOWS):
        c0 = g * POOL_GROUP_DIM
        win = abuf[HALO:HALO + tb, c0:c0 + POOL_GROUP_DIM]
        for i in range(1, w):
            win = win + abuf[HALO - i:HALO - i + tb, c0:c0 + POOL_GROUP_DIM]
        count = jnp.minimum(t + 1, w).astype(f32)
        mixed = win / count - a_in[:, c0:c0 + POOL_GROUP_DIM]
        ys.append(_dot(mixed.astype(bf16), pool_w_ref[g]))
    abuf[0:HALO, :] = abuf[tb:tb + HALO, :]
    y_a = jnp.concatenate(ys, axis=1) * pool_scale_ref[...] * _silu(proj(COL_A_GATE, POOL_WIDTH))

    cos_t = cos_ref[...]
    sin_t = sin_ref[...]
    lane = lax.broadcasted_iota(jnp.int32, (tb, HEAD_PAD), 1)

    cqn = _rms(proj(COL_CQ, Q_LORA_RANK), q_norm_ref[...]).astype(bf16)
    q2 = _dot(cqn, w_uq_ref[...])
    swap0 = MLA_HEADS * HEAD_PAD
    for h in range(MLA_HEADS):
        plain = q2[:, h * HEAD_PAD:(h + 1) * HEAD_PAD]
        swapped = q2[:, swap0 + h * HEAD_PAD:swap0 + (h + 1) * HEAD_PAD]
        q_ref[h] = ((plain * cos_t + swapped * sin_t) * (SOFTMAX_SCALE * LOG2E)).astype(bf16)

    kr = proj(COL_KR, HEAD_PAD)
    kr_swapped = jnp.where(lane < ROPE_LANE0 + HALF_ROPE,
                           pltpu.roll(kr, HEAD_PAD - HALF_ROPE, 1), pltpu.roll(kr, HALF_ROPE, 1))
    k_rope = kr * cos_t + kr_swapped * sin_t

    ckvn = _rms(proj(COL_CKV, KV_LORA_RANK), kv_norm_ref[...]).astype(bf16)
    kv2 = _dot(ckvn, w_ukv_ref[...])
    ones_lane = jnp.where(lane == ONES_LANE, 1.0, 0.0)
    for h in range(MLA_HEADS):
        k_ref[h] = (kv2[:, h * HEAD_PAD:(h + 1) * HEAD_PAD] + k_rope).astype(bf16)
        v_ref[h] = (kv2[:, swap0 + h * HEAD_PAD:swap0 + (h + 1) * HEAD_PAD] + ones_lane).astype(bf16)

    sb_ref[...] = _silu(proj(COL_B_GATE, MLA_WIDTH)).astype(sb_ref.dtype)

    vv = proj(COL_V, GMLP_WIDTH)
    vc = vv - jnp.mean(vv, axis=-1, keepdims=True)
    var = jnp.mean(vc * vc, axis=-1, keepdims=True)
    vb = (vc * lax.rsqrt(var + EPS) * vn_g_ref[...] + vn_b_ref[...]).astype(bf16)
    row = lax.broadcasted_iota(jnp.int32, (GMLP_CHUNK, GMLP_CHUNK), 0)
    col = lax.broadcasted_iota(jnp.int32, (GMLP_CHUNK, GMLP_CHUNK), 1)
    ws = [jnp.where(col <= row, w_s_ref[g], jnp.zeros((), bf16)) for g in range(GMLP_GROUPS)]
    groups_per_tile = V7X_LANES // GMLP_GROUP_DIM
    assert groups_per_tile == 2
    chunks = []
    for c in range(tb // GMLP_CHUNK):
        tiles = []
        for tile in range(GMLP_WIDTH // V7X_LANES):
            rhs = vb[c * GMLP_CHUNK:(c + 1) * GMLP_CHUNK, tile * V7X_LANES:(tile + 1) * V7X_LANES]
            lo = _dot(ws[2 * tile], rhs)
            hi = _dot(ws[2 * tile + 1], rhs)
            tiles.append(jnp.where(col < GMLP_GROUP_DIM, lo, hi))
        chunks.append(jnp.concatenate(tiles, axis=1) + bs_ref[...])
    sv = jnp.concatenate(chunks, axis=0)
    y_c = proj(COL_U, GMLP_WIDTH) * sv * _silu(proj(COL_C_GATE, GMLP_WIDTH))

    def gate(i):
        return _sigmoid(proj(COL_GATES + i * D_MODEL, D_MODEL)
                        + b_gate_ref[:, i * D_MODEL:(i + 1) * D_MODEL])

    g1_ref[...] = gate(1).astype(g1_ref.dtype)
    part_ref[...] = (gate(0) * _dot(y_a.astype(bf16), w_oa_ref[...])
                     + gate(2) * _dot(y_c.astype(bf16), w_oc_ref[...])).astype(part_ref.dtype)


def _const_spec(stacked, l):
    index = (l,) + (0,) * (stacked.ndim - 1)
    return pl.BlockSpec((None,) + stacked.shape[1:], lambda b, j: index, pipeline_mode=pl.Buffered(1))


def _front(x, cos_t, sin_t, p, l):
    B, S, _ = x.shape
    tb = FRONT_BLOCK
    tok = lambda width: pl.BlockSpec((None, tb, width), lambda b, j: (b, j, 0))
    heads = pl.BlockSpec((None, MLA_HEADS, tb, HEAD_PAD), lambda b, j: (b, 0, j, 0))
    consts = [p["norm_pre"], p["w_in"], p["b_gate"], p["pool_w"], p["pool_scale"], p["q_norm"],
              p["w_uq"], p["kv_norm"], p["w_ukv"], p["v_norm_g"], p["v_norm_b"], p["w_s"],
              p["bs_full"], p["w_oa"], p["w_oc"]]
    head_shape = jax.ShapeDtypeStruct((B, MLA_HEADS, S, HEAD_PAD), bf16)
    return pl.pallas_call(
        _front_kernel,
        grid=(B, S // tb),
        in_specs=[tok(D_MODEL), tok(HEAD_PAD), tok(HEAD_PAD)] + [_const_spec(c, l) for c in consts],
        out_specs=[heads, heads, heads, tok(D_MODEL), tok(D_MODEL), tok(MLA_WIDTH)],
        out_shape=[head_shape, head_shape, head_shape,
                   jax.ShapeDtypeStruct((B, S, D_MODEL), bf16),
                   jax.ShapeDtypeStruct((B, S, D_MODEL), bf16),
                   jax.ShapeDtypeStruct((B, S, MLA_WIDTH), bf16)],
        scratch_shapes=[pltpu.VMEM((HALO + tb, POOL_WIDTH), f32)],
        compiler_params=pltpu.CompilerParams(
            dimension_semantics=("arbitrary", "arbitrary"),
            vmem_limit_bytes=V7X_VMEM_BYTES * 7 // 8),
        name="layer_front",
    )(x, cos_t, sin_t, *consts)


def _attn_kernel(q_ref, k_ref, v_ref, o_ref, s_ref, p_ref, m_ref, acc_ref):
    hp, tq, _ = q_ref.shape
    i = pl.program_id(2)
    lane = lax.broadcasted_iota(jnp.int32, (tq, HEAD_PAD), 1)

    m_ref[...] = jnp.full(m_ref.shape, -jnp.inf, f32)
    acc_ref[...] = jnp.zeros(acc_ref.shape, f32)

    def scores(hh, j, slot):
        start = pl.multiple_of(j * tq, tq)
        kb = k_ref[hh, pl.ds(start, tq), :]
        s_ref[slot, hh] = lax.dot_general(q_ref[hh], kb, (((1,), (1,)), ((), ())),
                                          preferred_element_type=f32)

    def softmax_pv(hh, j, slot, diagonal):
        def tile(c):
            cols = slice(c * V7X_LANES, (c + 1) * V7X_LANES)
            s = s_ref[slot, hh, :, cols]
            if diagonal:
                row = lax.broadcasted_iota(jnp.int32, (tq, V7X_LANES), 0)
                s = jnp.where(c * V7X_LANES + lane <= row, s, -jnp.inf)
            return s

        n_tiles = tq // V7X_LANES
        m_blk = tile(0)
        for c in range(1, n_tiles):
            m_blk = jnp.maximum(m_blk, tile(c))
        m_old = m_ref[hh]
        m_new = jnp.maximum(m_old, jnp.max(m_blk, axis=-1, keepdims=True))
        m_ref[hh] = m_new
        for c in range(n_tiles):
            p_ref[hh, :, c * V7X_LANES:(c + 1) * V7X_LANES] = jnp.exp2(tile(c) - m_new).astype(bf16)
        vb = v_ref[hh, pl.ds(pl.multiple_of(j * tq, tq), tq), :]
        acc_ref[hh] = jnp.exp2(m_old - m_new) * acc_ref[hh] + _dot(p_ref[hh], vb)

    for hh in range(hp):
        scores(hh, 0, 0)

    def step(j, slot, diagonal):
        for hh in range(hp):
            softmax_pv(hh, j, slot, diagonal)
            if not diagonal:
                scores(hh, j + 1, 1 - slot)

    def body(t, carry):
        step(2 * t, 0, False)
        step(2 * t + 1, 1, False)
        return carry

    lax.fori_loop(0, i // 2, body, 0)

    @pl.when(i % 2 == 1)
    def _():
        step(i - 1, 0, False)
        step(i, 1, True)

    @pl.when(i % 2 == 0)
    def _():
        step(i, 0, True)
    outs = [acc_ref[hh] / acc_ref[hh, :, ONES_LANE:ONES_LANE + 1] for hh in range(hp)]
    o_ref[...] = jnp.where(lane < V_HEAD_DIM, outs[0],
                           pltpu.roll(outs[1], V_HEAD_DIM, 1)).astype(o_ref.dtype)


def _attention(q, k, v):
    B, H, S, _ = q.shape
    tq = ATTN_BLOCK
    hp = HEADS_PER_STEP
    assert hp * V_HEAD_DIM == V7X_LANES
    return pl.pallas_call(
        _attn_kernel,
        grid=(B, H // hp, S // tq),
        in_specs=[pl.BlockSpec((None, hp, tq, HEAD_PAD), lambda b, g, i: (b, g, i, 0)),
                  pl.BlockSpec((None, hp, S, HEAD_PAD), lambda b, g, i: (b, g, 0, 0)),
                  pl.BlockSpec((None, hp, S, HEAD_PAD), lambda b, g, i: (b, g, 0, 0))],
        out_specs=pl.BlockSpec((None, tq, hp * V_HEAD_DIM), lambda b, g, i: (b, i, g)),
        out_shape=jax.ShapeDtypeStruct((B, S, H * V_HEAD_DIM), bf16),
        scratch_shapes=[pltpu.VMEM((2, hp, tq, tq), f32),
                        pltpu.VMEM((hp, tq, tq), bf16),
                        pltpu.VMEM((hp, tq, V7X_LANES), f32),
                        pltpu.VMEM((hp, tq, HEAD_PAD), f32)],
        compiler_params=pltpu.CompilerParams(
            dimension_semantics=("arbitrary", "arbitrary", "arbitrary"),
            vmem_limit_bytes=V7X_VMEM_BYTES * 3 // 4),
        name="mla_attention",
    )(q, k, v)


def _back_kernel(x_ref, o_ref, sb_ref, g1_ref, part_ref, w_ob_ref, w_out_ref, norm_post_ref,
                 out_ref):
    y_b = o_ref[...] * sb_ref[...]
    merged = part_ref[...].astype(f32) + g1_ref[...].astype(f32) * _dot(y_b, w_ob_ref[...])
    z = _dot(merged.astype(bf16), w_out_ref[...])
    out_ref[...] = x_ref[...] + _rms(z, norm_post_ref[...])


def _back(x, o, sb, g1, part, p, l):
    B, S, _ = x.shape
    tb = BACK_BLOCK
    tok = lambda width: pl.BlockSpec((None, tb, width), lambda b, j: (b, j, 0))
    consts = [p["w_ob"], p["w_out"], p["norm_post"]]
    return pl.pallas_call(
        _back_kernel,
        grid=(B, S // tb),
        in_specs=[tok(D_MODEL), tok(MLA_WIDTH), tok(MLA_WIDTH), tok(D_MODEL), tok(D_MODEL)]
        + [_const_spec(c, l) for c in consts],
        out_specs=tok(D_MODEL),
        out_shape=jax.ShapeDtypeStruct((B, S, D_MODEL), f32),
        compiler_params=pltpu.CompilerParams(
            dimension_semantics=("arbitrary", "arbitrary"),
            vmem_limit_bytes=V7X_VMEM_BYTES // 2),
        name="layer_back",
    )(x, o, sb, g1, part, *consts)


def _stacked_params(norm_pre, norm_post, w_in, b_gate, pool_w, pool_scale, q_norm, w_uq,
                    kv_norm, w_ukv, v_norm_g, v_norm_b, w_s, b_s, w_oa, w_ob, w_oc, w_out):
    L = w_in.shape[0]
    offs = np.cumsum((0,) + IN_SPLITS)
    seg = [w_in[..., offs[i]:offs[i + 1]].astype(bf16) for i in range(len(IN_SPLITS))]
    zeros = lambda *shape: jnp.zeros(shape, bf16)
    kr_cols = jnp.concatenate([zeros(L, D_MODEL, ROPE_LANE0), seg[4],
                               zeros(L, D_MODEL, HEAD_PAD - ROPE_LANE0 - QK_ROPE_DIM)], axis=-1)
    w_in_p = jnp.concatenate(seg[:4] + [kr_cols] + seg[5:], axis=-1)
    assert w_in_p.shape == (L, D_MODEL, IN_WIDTH_PAD)

    uq = w_uq.astype(bf16).reshape(L, Q_LORA_RANK, MLA_HEADS, QK_NOPE_DIM + QK_ROPE_DIM)
    nope, x1, x2 = uq[..., :QK_NOPE_DIM], uq[..., QK_NOPE_DIM:QK_NOPE_DIM + HALF_ROPE], uq[..., QK_NOPE_DIM + HALF_ROPE:]
    pad = zeros(L, Q_LORA_RANK, MLA_HEADS, HEAD_PAD - QK_NOPE_DIM - QK_ROPE_DIM)
    plain = jnp.concatenate([nope, x1, x2, pad], axis=-1).reshape(L, Q_LORA_RANK, MLA_HEADS * HEAD_PAD)
    swapped = jnp.concatenate([jnp.zeros_like(nope), x2, x1, pad], axis=-1).reshape(L, Q_LORA_RANK, MLA_HEADS * HEAD_PAD)
    w_uq_p = jnp.concatenate([plain, swapped], axis=-1)

    ukv = w_ukv.astype(bf16).reshape(L, KV_LORA_RANK, MLA_HEADS, QK_NOPE_DIM + V_HEAD_DIM)
    kpad = zeros(L, KV_LORA_RANK, MLA_HEADS, HEAD_PAD - QK_NOPE_DIM)
    vpad = zeros(L, KV_LORA_RANK, MLA_HEADS, HEAD_PAD - V_HEAD_DIM)
    k_cols = jnp.concatenate([ukv[..., :QK_NOPE_DIM], kpad], axis=-1).reshape(L, KV_LORA_RANK, MLA_HEADS * HEAD_PAD)
    v_cols = jnp.concatenate([ukv[..., QK_NOPE_DIM:], vpad], axis=-1).reshape(L, KV_LORA_RANK, MLA_HEADS * HEAD_PAD)
    w_ukv_p = jnp.concatenate([k_cols, v_cols], axis=-1)

    row = lambda a: a.reshape(L, 1, -1)
    return {
        "norm_pre": row(norm_pre), "norm_post": row(norm_post), "w_in": w_in_p,
        "b_gate": row(b_gate), "pool_w": pool_w.astype(bf16), "pool_scale": row(pool_scale),
        "q_norm": row(q_norm), "w_uq": w_uq_p, "kv_norm": row(kv_norm), "w_ukv": w_ukv_p,
        "v_norm_g": row(v_norm_g), "v_norm_b": row(v_norm_b), "w_s": w_s.astype(bf16),
        "bs_full": jnp.repeat(jnp.swapaxes(b_s, 1, 2), GMLP_GROUP_DIM, axis=2),
        "w_oa": w_oa.astype(bf16), "w_ob": w_ob.astype(bf16), "w_oc": w_oc.astype(bf16),
        "w_out": w_out.astype(bf16),
    }


def kernel(x, positions, norm_pre, norm_post, w_in, b_gate, pool_w, pool_scale, q_norm, w_uq, kv_norm, w_ukv, v_norm_g, v_norm_b, w_s, b_s, w_oa, w_ob, w_oc, w_out):
    B, S, _ = x.shape
    cos_t, sin_t = _rope_tables(positions)
    cos_t = cos_t.reshape(B, S, HEAD_PAD)
    sin_t = sin_t.reshape(B, S, HEAD_PAD)
    p = _stacked_params(norm_pre, norm_post, w_in, b_gate, pool_w, pool_scale, q_norm, w_uq,
                        kv_norm, w_ukv, v_norm_g, v_norm_b, w_s, b_s, w_oa, w_ob, w_oc, w_out)
    for l in range(norm_pre.shape[0]):
        q, k, v, part, g1, sb = _front(x, cos_t, sin_t, p, l)
        o = _attention(q, k, v)
        x = _back(x, o, sb, g1, part, p, l)
    return x
```

```python
import functools
import math

import jax
import jax.numpy as jnp
import numpy as np
from jax import lax
from jax.experimental import pallas as pl
from jax.experimental.pallas import tpu as pltpu

D_MODEL = 1024
POOL_WIDTH = 512
POOL_GROUPS = 4
POOL_GROUP_DIM = POOL_WIDTH // POOL_GROUPS
POOL_WINDOWS = (2, 4, 8, 16)
MLA_HEADS = 8
QK_NOPE_DIM = 64
QK_ROPE_DIM = 32
HALF_ROPE = QK_ROPE_DIM // 2
V_HEAD_DIM = 64
Q_LORA_RANK = 256
KV_LORA_RANK = 128
MLA_WIDTH = MLA_HEADS * V_HEAD_DIM
ROPE_THETA = 10000.0
GMLP_WIDTH = 512
GMLP_CHUNK = 128
GMLP_GROUPS = 8
GMLP_GROUP_DIM = GMLP_WIDTH // GMLP_GROUPS
N_BRANCHES = 3
EPS = 1e-6
IN_SPLITS = (POOL_WIDTH, POOL_WIDTH, Q_LORA_RANK, KV_LORA_RANK, QK_ROPE_DIM, MLA_WIDTH,
             GMLP_WIDTH, GMLP_WIDTH, GMLP_WIDTH, N_BRANCHES * D_MODEL)

V7X_LANES = 128
V7X_VMEM_BYTES = 64 * 1024 * 1024

HEAD_PAD = V7X_LANES
ROPE_LANE0 = QK_NOPE_DIM
ONES_LANE = V_HEAD_DIM
HALO = max(POOL_WINDOWS)

COL_A_IN = 0
COL_A_GATE = COL_A_IN + POOL_WIDTH
COL_CQ = COL_A_GATE + POOL_WIDTH
COL_CKV = COL_CQ + Q_LORA_RANK
COL_KR = COL_CKV + KV_LORA_RANK
COL_B_GATE = COL_KR + HEAD_PAD
COL_U = COL_B_GATE + MLA_WIDTH
COL_V = COL_U + GMLP_WIDTH
COL_C_GATE = COL_V + GMLP_WIDTH
COL_GATES = COL_C_GATE + GMLP_WIDTH
IN_WIDTH_PAD = COL_GATES + N_BRANCHES * D_MODEL

SOFTMAX_SCALE = (QK_NOPE_DIM + QK_ROPE_DIM) ** -0.5
LOG2E = math.log2(math.e)

FRONT_BLOCK = 512
ATTN_BLOCK = 1024
BACK_BLOCK = 512
HEADS_PER_STEP = 2
SOFTMAX_ROWS = 64

bf16 = jnp.bfloat16
f32 = jnp.float32


def _sigmoid(z):
    return 0.5 * jnp.tanh(0.5 * z) + 0.5


def _silu(z):
    return z * _sigmoid(z)


def _rms(x, g):
    return x * lax.rsqrt(jnp.mean(x * x, axis=-1, keepdims=True) + EPS) * g


def _dot(a, b):
    return jnp.dot(a, b, preferred_element_type=f32)


def _rope_table_kernel(pos_ref, invf_ref, cos_ref, sin_ref):
    ang = pos_ref[...].astype(f32) * invf_ref[...]
    cos_ref[...] = jnp.cos(ang)
    sin_ref[...] = jnp.sin(ang)


def _rope_tables(positions):
    n = positions.size
    per_row = V7X_LANES // HALF_ROPE
    inv_freq = ROPE_THETA ** (-jnp.arange(0, QK_ROPE_DIM, 2, dtype=f32) / QK_ROPE_DIM)
    invf = jnp.tile(inv_freq, per_row).reshape(1, V7X_LANES)
    pos = jnp.repeat(positions.reshape(n // per_row, per_row), HALF_ROPE, axis=1)
    dense = jax.ShapeDtypeStruct((n // per_row, V7X_LANES), f32)
    cos_d, sin_d = pl.pallas_call(
        _rope_table_kernel,
        out_shape=[dense, dense],
        name="rope_tables",
    )(pos, invf)
    c = cos_d.reshape(n, HALF_ROPE)
    s = sin_d.reshape(n, HALF_ROPE)
    pad = jnp.zeros((n, HEAD_PAD - ROPE_LANE0 - QK_ROPE_DIM), f32)
    cos_t = jnp.concatenate([jnp.ones((n, ROPE_LANE0), f32), c, c, pad], axis=1)
    sin_t = jnp.concatenate([jnp.zeros((n, ROPE_LANE0), f32), -s, s, pad], axis=1)
    return cos_t, sin_t


def _front_kernel(x_ref, cos_ref, sin_ref, norm_pre_ref, w_in_ref, b_gate_ref, pool_w_ref,
                  pool_scale_ref, q_norm_ref, w_uq_ref, kv_norm_ref, w_ukv_ref, vn_g_ref,
                  vn_b_ref, w_s_ref, bs_ref, w_oa_ref, w_oc_ref,
                  q_ref, k_ref, v_ref, part_ref, g1_ref, sb_ref, abuf):
    tb = x_ref.shape[0]
    j = pl.program_id(1)

    hb = _rms(x_ref[...], norm_pre_ref[...]).astype(bf16)

    def proj(c0, width):
        return _dot(hb, w_in_ref[:, c0:c0 + width])

    @pl.when(j == 0)
    def _():
        abuf[0:HALO, :] = jnp.zeros((HALO, POOL_WIDTH), f32)

    a_in = proj(COL_A_IN, POOL_WIDTH)
    abuf[HALO:HALO + tb, :] = a_in
    t = j * tb + lax.broadcasted_iota(jnp.int32, (tb, 1), 0)
    ys = []
    for g, w in enumerate(POOL_WINDOWS):
        c0 = g * POOL_GROUP_DIM
        win = abuf[HALO:HALO + tb, c0:c0 + POOL_GROUP_DIM]
        for i in range(1, w):
            win = win + abuf[HALO - i:HALO - i + tb, c0:c0 + POOL_GROUP_DIM]
        count = jnp.minimum(t + 1, w).astype(f32)
        mixed = win / count - a_in[:, c0:c0 + POOL_GROUP_DIM]
        ys.append(_dot(mixed.astype(bf16), pool_w_ref[g]))
    abuf[0:HALO, :] = abuf[tb:tb + HALO, :]
    y_a = jnp.concatenate(ys, axis=1) * pool_scale_ref[...] * _silu(proj(COL_A_GATE, POOL_WIDTH))

    cos_t = cos_ref[...]
    sin_t = sin_ref[...]
    lane = lax.broadcasted_iota(jnp.int32, (tb, HEAD_PAD), 1)

    cqn = _rms(proj(COL_CQ, Q_LORA_RANK), q_norm_ref[...]).astype(bf16)
    q2 = _dot(cqn, w_uq_ref[...])
    swap0 = MLA_HEADS * HEAD_PAD
    for h in range(MLA_HEADS):
        plain = q2[:, h * HEAD_PAD:(h + 1) * HEAD_PAD]
        swapped = q2[:, swap0 + h * HEAD_PAD:swap0 + (h + 1) * HEAD_PAD]
        q_ref[h] = ((plain * cos_t + swapped * sin_t) * (SOFTMAX_SCALE * LOG2E)).astype(bf16)

    kr = proj(COL_KR, HEAD_PAD)
    kr_swapped = jnp.where(lane < ROPE_LANE0 + HALF_ROPE,
                           pltpu.roll(kr, HEAD_PAD - HALF_ROPE, 1), pltpu.roll(kr, HALF_ROPE, 1))
    k_rope = kr * cos_t + kr_swapped * sin_t

    ckvn = _rms(proj(COL_CKV, KV_LORA_RANK), kv_norm_ref[...]).astype(bf16)
    kv2 = _dot(ckvn, w_ukv_ref[...])
    ones_lane = jnp.where(lane == ONES_LANE, 1.0, 0.0)
    for h in range(MLA_HEADS):
        k_ref[h] = (kv2[:, h * HEAD_PAD:(h + 1) * HEAD_PAD] + k_rope).astype(bf16)
        v_ref[h] = (kv2[:, swap0 + h * HEAD_PAD:swap0 + (h + 1) * HEAD_PAD] + ones_lane).astype(bf16)

    sb_ref[...] = _silu(proj(COL_B_GATE, MLA_WIDTH)).astype(sb_ref.dtype)

    vv = proj(COL_V, GMLP_WIDTH)
    vc = vv - jnp.mean(vv, axis=-1, keepdims=True)
    var = jnp.mean(vc * vc, axis=-1, keepdims=True)
    vb = (vc * lax.rsqrt(var + EPS) * vn_g_ref[...] + vn_b_ref[...]).astype(bf16)
    row = lax.broadcasted_iota(jnp.int32, (GMLP_CHUNK, GMLP_CHUNK), 0)
    col = lax.broadcasted_iota(jnp.int32, (GMLP_CHUNK, GMLP_CHUNK), 1)
    ws = [jnp.where(col <= row, w_s_ref[g], jnp.zeros((), bf16)) for g in range(GMLP_GROUPS)]
    groups_per_tile = V7X_LANES // GMLP_GROUP_DIM
    assert groups_per_tile == 2
    chunks = []
    for c in range(tb // GMLP_CHUNK):
        tiles = []
        for tile in range(GMLP_WIDTH // V7X_LANES):
            rhs = vb[c * GMLP_CHUNK:(c + 1) * GMLP_CHUNK, tile * V7X_LANES:(tile + 1) * V7X_LANES]
            lo = _dot(ws[2 * tile], rhs)
            hi = _dot(ws[2 * tile + 1], rhs)
            tiles.append(jnp.where(col < GMLP_GROUP_DIM, lo, hi))
        chunks.append(jnp.concatenate(tiles, axis=1) + bs_ref[...])
    sv = jnp.concatenate(chunks, axis=0)
    y_c = proj(COL_U, GMLP_WIDTH) * sv * _silu(proj(COL_C_GATE, GMLP_WIDTH))

    def gate(i):
        return _sigmoid(proj(COL_GATES + i * D_MODEL, D_MODEL)
                        + b_gate_ref[:, i * D_MODEL:(i + 1) * D_MODEL])

    g1_ref[...] = gate(1).astype(g1_ref.dtype)
    part_ref[...] = (gate(0) * _dot(y_a.astype(bf16), w_oa_ref[...])
                     + gate(2) * _dot(y_c.astype(bf16), w_oc_ref[...])).astype(part_ref.dtype)


def _const_spec(stacked, l):
    index = (l,) + (0,) * (stacked.ndim - 1)
    return pl.BlockSpec((None,) + stacked.shape[1:], lambda b, j: index, pipeline_mode=pl.Buffered(1))


def _front(x, cos_t, sin_t, p, l):
    B, S, _ = x.shape
    tb = FRONT_BLOCK
    tok = lambda width: pl.BlockSpec((None, tb, width), lambda b, j: (b, j, 0))
    heads = pl.BlockSpec((None, MLA_HEADS, tb, HEAD_PAD), lambda b, j: (b, 0, j, 0))
    consts = [p["norm_pre"], p["w_in"], p["b_gate"], p["pool_w"], p["pool_scale"], p["q_norm"],
              p["w_uq"], p["kv_norm"], p["w_ukv"], p["v_norm_g"], p["v_norm_b"], p["w_s"],
              p["bs_full"], p["w_oa"], p["w_oc"]]
    head_shape = jax.ShapeDtypeStruct((B, MLA_HEADS, S, HEAD_PAD), bf16)
    return pl.pallas_call(
        _front_kernel,
        grid=(B, S // tb),
        in_specs=[tok(D_MODEL), tok(HEAD_PAD), tok(HEAD_PAD)] + [_const_spec(c, l) for c in consts],
        out_specs=[heads, heads, heads, tok(D_MODEL), tok(D_MODEL), tok(MLA_WIDTH)],
        out_shape=[head_shape, head_shape, head_shape,
                   jax.ShapeDtypeStruct((B, S, D_MODEL), bf16),
                   jax.ShapeDtypeStruct((B, S, D_MODEL), bf16),
                   jax.ShapeDtypeStruct((B, S, MLA_WIDTH), bf16)],
        scratch_shapes=[pltpu.VMEM((HALO + tb, POOL_WIDTH), f32)],
        compiler_params=pltpu.CompilerParams(
            dimension_semantics=("arbitrary", "arbitrary"),
            vmem_limit_bytes=V7X_VMEM_BYTES * 7 // 8),
        name="layer_front",
    )(x, cos_t, sin_t, *consts)


def _attn_kernel(q_ref, k_ref, v_ref, o_ref, s_ref, p_ref, m_ref, acc_ref):
    hp, tq, _ = q_ref.shape
    i = pl.program_id(2)
    lane = lax.broadcasted_iota(jnp.int32, (tq, HEAD_PAD), 1)
    n_tiles = tq // V7X_LANES
    half = tq // 2

    m_ref[...] = jnp.full(m_ref.shape, -jnp.inf, f32)
    acc_ref[...] = jnp.zeros(acc_ref.shape, f32)

    nt_dims = (((1,), (1,)), ((), ()))

    def scores(hh, j, slot, diagonal=False):
        start = pl.multiple_of(j * tq, tq)
        if diagonal:
            s_ref[slot, hh, 0:half, 0:half] = lax.dot_general(
                q_ref[hh, 0:half], k_ref[hh, pl.ds(start, half), :], nt_dims, preferred_element_type=f32)
            s_ref[slot, hh, half:tq, :] = lax.dot_general(
                q_ref[hh, half:tq], k_ref[hh, pl.ds(start, tq), :], nt_dims, preferred_element_type=f32)
        else:
            s_ref[slot, hh] = lax.dot_general(q_ref[hh], k_ref[hh, pl.ds(start, tq), :], nt_dims,
                                              preferred_element_type=f32)

    def softmax_pv(hh, j, slot, diagonal):
        for r in range(tq // SOFTMAX_ROWS):
            r0 = r * SOFTMAX_ROWS
            rows = slice(r0, r0 + SOFTMAX_ROWS)
            live = (r0 + SOFTMAX_ROWS - 1) // V7X_LANES + 1 if diagonal else n_tiles
            read = (half if r0 < half else tq) // V7X_LANES if diagonal else n_tiles

            def tile(c):
                s = s_ref[slot, hh, rows, c * V7X_LANES:(c + 1) * V7X_LANES]
                if diagonal and (c + 1) * V7X_LANES - 1 > r0:
                    row = r0 + lax.broadcasted_iota(jnp.int32, s.shape, 0)
                    col = c * V7X_LANES + lax.broadcasted_iota(jnp.int32, s.shape, 1)
                    s = jnp.where(col <= row, s, -jnp.inf)
                return s

            m_blk = tile(0)
            for c in range(1, live):
                m_blk = jnp.maximum(m_blk, tile(c))
            m_old = m_ref[hh, rows]
            m_new = jnp.maximum(m_old, jnp.max(m_blk, axis=-1, keepdims=True))
            m_ref[hh, rows] = m_new
            acc_ref[hh, rows] = jnp.exp2(m_old - m_new) * acc_ref[hh, rows]
            for c in range(live):
                p_ref[hh, rows, c * V7X_LANES:(c + 1) * V7X_LANES] = jnp.exp2(tile(c) - m_new).astype(bf16)
            if live < read:
                p_ref[hh, rows, live * V7X_LANES:read * V7X_LANES] = jnp.zeros(
                    (SOFTMAX_ROWS, (read - live) * V7X_LANES), bf16)
        start = pl.multiple_of(j * tq, tq)
        if diagonal:
            acc_ref[hh, 0:half] += _dot(p_ref[hh, 0:half, 0:half], v_ref[hh, pl.ds(start, half), :])
            acc_ref[hh, half:tq] += _dot(p_ref[hh, half:tq, :], v_ref[hh, pl.ds(start, tq), :])
        else:
            acc_ref[hh] += _dot(p_ref[hh], v_ref[hh, pl.ds(start, tq), :])

    def step(j, slot, next_diagonal=False):
        for hh in range(hp):
            softmax_pv(hh, j, slot, False)
            scores(hh, j + 1, 1 - slot, next_diagonal)

    def last_step(slot):
        for hh in range(hp):
            softmax_pv(hh, i, slot, True)

    @pl.when(i == 0)
    def _():
        for hh in range(hp):
            scores(hh, 0, 0, True)
        last_step(0)

    @pl.when(i > 0)
    def _():
        for hh in range(hp):
            scores(hh, 0, 0)

    def body(t, carry):
        step(2 * t, 0)
        step(2 * t + 1, 1)
        return carry

    lax.fori_loop(0, jnp.maximum(i - 1, 0) // 2, body, 0)

    @pl.when(i % 2 == 1)
    def _():
        step(i - 1, 0, True)
        last_step(1)

    @pl.when((i % 2 == 0) & (i > 0))
    def _():
        step(i - 2, 0)
        step(i - 1, 1, True)
        last_step(0)

    outs = [acc_ref[hh] / acc_ref[hh, :, ONES_LANE:ONES_LANE + 1] for hh in range(hp)]
    o_ref[...] = jnp.where(lane < V_HEAD_DIM, outs[0],
                           pltpu.roll(outs[1], V_HEAD_DIM, 1)).astype(o_ref.dtype)


def _attention(q, k, v):
    B, H, S, _ = q.shape
    tq = ATTN_BLOCK
    hp = HEADS_PER_STEP
    assert hp * V_HEAD_DIM == V7X_LANES
    return pl.pallas_call(
        _attn_kernel,
        grid=(B, H // hp, S // tq),
        in_specs=[pl.BlockSpec((None, hp, tq, HEAD_PAD), lambda b, g, i: (b, g, i, 0)),
                  pl.BlockSpec((None, hp, S, HEAD_PAD), lambda b, g, i: (b, g, 0, 0)),
                  pl.BlockSpec((None, hp, S, HEAD_PAD), lambda b, g, i: (b, g, 0, 0))],
        out_specs=pl.BlockSpec((None, tq, hp * V_HEAD_DIM), lambda b, g, i: (b, i, g)),
        out_shape=jax.ShapeDtypeStruct((B, S, H * V_HEAD_DIM), bf16),
        scratch_shapes=[pltpu.VMEM((2, hp, tq, tq), f32),
                        pltpu.VMEM((hp, tq, tq), bf16),
                        pltpu.VMEM((hp, tq, V7X_LANES), f32),
                        pltpu.VMEM((hp, tq, HEAD_PAD), f32)],
        compiler_params=pltpu.CompilerParams(
            dimension_semantics=("arbitrary", "arbitrary", "arbitrary"),
            vmem_limit_bytes=V7X_VMEM_BYTES * 7 // 8),
        name="mla_attention",
    )(q, k, v)


def _back_kernel(x_ref, o_ref, sb_ref, g1_ref, part_ref, w_ob_ref, w_out_ref, norm_post_ref,
                 out_ref):
    y_b = o_ref[...] * sb_ref[...]
    merged = part_ref[...].astype(f32) + g1_ref[...].astype(f32) * _dot(y_b, w_ob_ref[...])
    z = _dot(merged.astype(bf16), w_out_ref[...])
    out_ref[...] = x_ref[...] + _rms(z, norm_post_ref[...])


def _back(x, o, sb, g1, part, p, l):
    B, S, _ = x.shape
    tb = BACK_BLOCK
    tok = lambda width: pl.BlockSpec((None, tb, width), lambda b, j: (b, j, 0))
    consts = [p["w_ob"], p["w_out"], p["norm_post"]]
    return pl.pallas_call(
        _back_kernel,
        grid=(B, S // tb),
        in_specs=[tok(D_MODEL), tok(MLA_WIDTH), tok(MLA_WIDTH), tok(D_MODEL), tok(D_MODEL)]
        + [_const_spec(c, l) for c in consts],
        out_specs=tok(D_MODEL),
        out_shape=jax.ShapeDtypeStruct((B, S, D_MODEL), f32),
        compiler_params=pltpu.CompilerParams(
            dimension_semantics=("arbitrary", "arbitrary"),
            vmem_limit_bytes=V7X_VMEM_BYTES // 2),
        name="layer_back",
    )(x, o, sb, g1, part, *consts)


def _stacked_params(norm_pre, norm_post, w_in, b_gate, pool_w, pool_scale, q_norm, w_uq,
                    kv_norm, w_ukv, v_norm_g, v_norm_b, w_s, b_s, w_oa, w_ob, w_oc, w_out):
    L = w_in.shape[0]
    offs = np.cumsum((0,) + IN_SPLITS)
    seg = [w_in[..., offs[i]:offs[i + 1]].astype(bf16) for i in range(len(IN_SPLITS))]
    zeros = lambda *shape: jnp.zeros(shape, bf16)
    kr_cols = jnp.concatenate([zeros(L, D_MODEL, ROPE_LANE0), seg[4],
                               zeros(L, D_MODEL, HEAD_PAD - ROPE_LANE0 - QK_ROPE_DIM)], axis=-1)
    w_in_p = jnp.concatenate(seg[:4] + [kr_cols] + seg[5:], axis=-1)
    assert w_in_p.shape == (L, D_MODEL, IN_WIDTH_PAD)

    uq = w_uq.astype(bf16).reshape(L, Q_LORA_RANK, MLA_HEADS, QK_NOPE_DIM + QK_ROPE_DIM)
    nope, x1, x2 = uq[..., :QK_NOPE_DIM], uq[..., QK_NOPE_DIM:QK_NOPE_DIM + HALF_ROPE], uq[..., QK_NOPE_DIM + HALF_ROPE:]
    pad = zeros(L, Q_LORA_RANK, MLA_HEADS, HEAD_PAD - QK_NOPE_DIM - QK_ROPE_DIM)
    plain = jnp.concatenate([nope, x1, x2, pad], axis=-1).reshape(L, Q_LORA_RANK, MLA_HEADS * HEAD_PAD)
    swapped = jnp.concatenate([jnp.zeros_like(nope), x2, x1, pad], axis=-1).reshape(L, Q_LORA_RANK, MLA_HEADS * HEAD_PAD)
    w_uq_p = jnp.concatenate([plain, swapped], axis=-1)

    ukv = w_ukv.astype(bf16).reshape(L, KV_LORA_RANK, MLA_HEADS, QK_NOPE_DIM + V_HEAD_DIM)
    kpad = zeros(L, KV_LORA_RANK, MLA_HEADS, HEAD_PAD - QK_NOPE_DIM)
    vpad = zeros(L, KV_LORA_RANK, MLA_HEADS, HEAD_PAD - V_HEAD_DIM)
    k_cols = jnp.concatenate([ukv[..., :QK_NOPE_DIM], kpad], axis=-1).reshape(L, KV_LORA_RANK, MLA_HEADS * HEAD_PAD)
    v_cols = jnp.concatenate([ukv[..., QK_NOPE_DIM:], vpad], axis=-1).reshape(L, KV_LORA_RANK, MLA_HEADS * HEAD_PAD)
    w_ukv_p = jnp.concatenate([k_cols, v_cols], axis=-1)

    row = lambda a: a.reshape(L, 1, -1)
    return {
        "norm_pre": row(norm_pre), "norm_post": row(norm_post), "w_in": w_in_p,
        "b_gate": row(b_gate), "pool_w": pool_w.astype(bf16), "pool_scale": row(pool_scale),
        "q_norm": row(q_norm), "w_uq": w_uq_p, "kv_norm": row(kv_norm), "w_ukv": w_ukv_p,
        "v_norm_g": row(v_norm_g), "v_norm_b": row(v_norm_b), "w_s": w_s.astype(bf16),
        "bs_full": jnp.repeat(jnp.swapaxes(b_s, 1, 2), GMLP_GROUP_DIM, axis=2),
        "w_oa": w_oa.astype(bf16), "w_ob": w_ob.astype(bf16), "w_oc": w_oc.astype(bf16),
        "w_out": w_out.astype(bf16),
    }


def kernel(x, positions, norm_pre, norm_post, w_in, b_gate, pool_w, pool_scale, q_norm, w_uq, kv_norm, w_ukv, v_norm_g, v_norm_b, w_s, b_s, w_oa, w_ob, w_oc, w_out):
    B, S, _ = x.shape
    cos_t, sin_t = _rope_tables(positions)
    cos_t = cos_t.reshape(B, S, HEAD_PAD)
    sin_t = sin_t.reshape(B, S, HEAD_PAD)
    p = _stacked_params(norm_pre, norm_post, w_in, b_gate, pool_w, pool_scale, q_norm, w_uq,
                        kv_norm, w_ukv, v_norm_g, v_norm_b, w_s, b_s, w_oa, w_ob, w_oc, w_out)
    for l in range(norm_pre.shape[0]):
        q, k, v, part, g1, sb = _front(x, cos_t, sin_t, p, l)
        o = _attention(q, k, v)
        x = _back(x, o, sb, g1, part, p, l)
    return x
```

```python
import functools
import math

import jax
import jax.numpy as jnp
import numpy as np
from jax import lax
from jax.experimental import pallas as pl
from jax.experimental.pallas import tpu as pltpu

D_MODEL = 1024
POOL_WIDTH = 512
POOL_GROUPS = 4
POOL_GROUP_DIM = POOL_WIDTH // POOL_GROUPS
POOL_WINDOWS = (2, 4, 8, 16)
MLA_HEADS = 8
QK_NOPE_DIM = 64
QK_ROPE_DIM = 32
HALF_ROPE = QK_ROPE_DIM // 2
V_HEAD_DIM = 64
Q_LORA_RANK = 256
KV_LORA_RANK = 128
MLA_WIDTH = MLA_HEADS * V_HEAD_DIM
ROPE_THETA = 10000.0
GMLP_WIDTH = 512
GMLP_CHUNK = 128
GMLP_GROUPS = 8
GMLP_GROUP_DIM = GMLP_WIDTH // GMLP_GROUPS
N_BRANCHES = 3
EPS = 1e-6
IN_SPLITS = (POOL_WIDTH, POOL_WIDTH, Q_LORA_RANK, KV_LORA_RANK, QK_ROPE_DIM, MLA_WIDTH,
             GMLP_WIDTH, GMLP_WIDTH, GMLP_WIDTH, N_BRANCHES * D_MODEL)

V7X_LANES = 128
V7X_VMEM_BYTES = 64 * 1024 * 1024

HEAD_PAD = V7X_LANES
ROPE_LANE0 = QK_NOPE_DIM
ONES_LANE = V_HEAD_DIM
HALO = max(POOL_WINDOWS)

COL_A_IN = 0
COL_A_GATE = COL_A_IN + POOL_WIDTH
COL_CQ = COL_A_GATE + POOL_WIDTH
COL_CKV = COL_CQ + Q_LORA_RANK
COL_KR = COL_CKV + KV_LORA_RANK
COL_B_GATE = COL_KR + HEAD_PAD
COL_U = COL_B_GATE + MLA_WIDTH
COL_V = COL_U + GMLP_WIDTH
COL_C_GATE = COL_V + GMLP_WIDTH
COL_GATES = COL_C_GATE + GMLP_WIDTH
IN_WIDTH_PAD = COL_GATES + N_BRANCHES * D_MODEL

SOFTMAX_SCALE = (QK_NOPE_DIM + QK_ROPE_DIM) ** -0.5
LOG2E = math.log2(math.e)

FRONT_BLOCK = 512
ATTN_BLOCK = 1024
BACK_BLOCK = 512
HEADS_PER_STEP = 2
SOFTMAX_ROWS = 64

bf16 = jnp.bfloat16
f32 = jnp.float32


def _sigmoid(z):
    return 0.5 * jnp.tanh(0.5 * z) + 0.5


def _silu(z):
    return z * _sigmoid(z)


def _rms(x, g):
    return x * lax.rsqrt(jnp.mean(x * x, axis=-1, keepdims=True) + EPS) * g


def _dot(a, b):
    return jnp.dot(a, b, preferred_element_type=f32)


def _rope_table_kernel(pos_ref, invf_ref, cos_ref, sin_ref):
    ang = pos_ref[...].astype(f32) * invf_ref[...]
    cos_ref[...] = jnp.cos(ang)
    sin_ref[...] = jnp.sin(ang)


def _rope_tables(positions):
    n = positions.size
    per_row = V7X_LANES // HALF_ROPE
    inv_freq = ROPE_THETA ** (-jnp.arange(0, QK_ROPE_DIM, 2, dtype=f32) / QK_ROPE_DIM)
    invf = jnp.tile(inv_freq, per_row).reshape(1, V7X_LANES)
    pos = jnp.repeat(positions.reshape(n // per_row, per_row), HALF_ROPE, axis=1)
    dense = jax.ShapeDtypeStruct((n // per_row, V7X_LANES), f32)
    cos_d, sin_d = pl.pallas_call(
        _rope_table_kernel,
        out_shape=[dense, dense],
        name="rope_tables",
    )(pos, invf)
    c = cos_d.reshape(n, HALF_ROPE)
    s = sin_d.reshape(n, HALF_ROPE)
    pad = jnp.zeros((n, HEAD_PAD - ROPE_LANE0 - QK_ROPE_DIM), f32)
    cos_t = jnp.concatenate([jnp.ones((n, ROPE_LANE0), f32), c, c, pad], axis=1)
    sin_t = jnp.concatenate([jnp.zeros((n, ROPE_LANE0), f32), -s, s, pad], axis=1)
    return cos_t, sin_t


def _front_kernel(x_ref, cos_ref, sin_ref, norm_pre_ref, w_in_ref, b_gate_ref, pool_w_ref,
                  pool_scale_ref, q_norm_ref, w_uq_ref, kv_norm_ref, w_ukv_ref, vn_g_ref,
                  vn_b_ref, w_s_ref, bs_ref, w_oa_ref, w_oc_ref,
                  q_ref, k_ref, v_ref, part_ref, g1_ref, sb_ref, abuf):
    tb = x_ref.shape[0]
    j = pl.program_id(1)

    hb = _rms(x_ref[...], norm_pre_ref[...]).astype(bf16)

    def proj(c0, width):
        return _dot(hb, w_in_ref[:, c0:c0 + width])

    @pl.when(j == 0)
    def _():
        abuf[0:HALO, :] = jnp.zeros((HALO, POOL_WIDTH), f32)

    a_in = proj(COL_A_IN, POOL_WIDTH)
    abuf[HALO:HALO + tb, :] = a_in
    cq = proj(COL_CQ, Q_LORA_RANK)
    ckv = proj(COL_CKV, KV_LORA_RANK)
    kr = proj(COL_KR, HEAD_PAD)
    vv = proj(COL_V, GMLP_WIDTH)
    a_gate = proj(COL_A_GATE, POOL_WIDTH)
    b_gate = proj(COL_B_GATE, MLA_WIDTH)
    u = proj(COL_U, GMLP_WIDTH)
    c_gate = proj(COL_C_GATE, GMLP_WIDTH)

    t = j * tb + lax.broadcasted_iota(jnp.int32, (tb, 1), 0)
    ys = []
    for g, w in enumerate(POOL_WINDOWS):
        c0 = g * POOL_GROUP_DIM
        win = abuf[HALO:HALO + tb, c0:c0 + POOL_GROUP_DIM]
        for i in range(1, w):
            win = win + abuf[HALO - i:HALO - i + tb, c0:c0 + POOL_GROUP_DIM]
        count = jnp.minimum(t + 1, w).astype(f32)
        mixed = win / count - a_in[:, c0:c0 + POOL_GROUP_DIM]
        ys.append(_dot(mixed.astype(bf16), pool_w_ref[g]))
    abuf[0:HALO, :] = abuf[tb:tb + HALO, :]

    cqn = _rms(cq, q_norm_ref[...]).astype(bf16)
    q2 = _dot(cqn, w_uq_ref[...])
    ckvn = _rms(ckv, kv_norm_ref[...]).astype(bf16)
    kv2 = _dot(ckvn, w_ukv_ref[...])

    vc = vv - jnp.mean(vv, axis=-1, keepdims=True)
    var = jnp.mean(vc * vc, axis=-1, keepdims=True)
    vb = (vc * lax.rsqrt(var + EPS) * vn_g_ref[...] + vn_b_ref[...]).astype(bf16)
    row = lax.broadcasted_iota(jnp.int32, (GMLP_CHUNK, GMLP_CHUNK), 0)
    col = lax.broadcasted_iota(jnp.int32, (GMLP_CHUNK, GMLP_CHUNK), 1)
    ws = [jnp.where(col <= row, w_s_ref[g], jnp.zeros((), bf16)) for g in range(GMLP_GROUPS)]
    groups_per_tile = V7X_LANES // GMLP_GROUP_DIM
    assert groups_per_tile == 2
    chunks = []
    for c in range(tb // GMLP_CHUNK):
        tiles = []
        for tile in range(GMLP_WIDTH // V7X_LANES):
            rhs = vb[c * GMLP_CHUNK:(c + 1) * GMLP_CHUNK, tile * V7X_LANES:(tile + 1) * V7X_LANES]
            lo = _dot(ws[2 * tile], rhs)
            hi = _dot(ws[2 * tile + 1], rhs)
            tiles.append(jnp.where(col < GMLP_GROUP_DIM, lo, hi))
        chunks.append(jnp.concatenate(tiles, axis=1) + bs_ref[...])
    sv = jnp.concatenate(chunks, axis=0)

    gate_pre = [proj(COL_GATES + i * D_MODEL, D_MODEL) for i in range(N_BRANCHES)]

    cos_t = cos_ref[...]
    sin_t = sin_ref[...]
    lane = lax.broadcasted_iota(jnp.int32, (tb, HEAD_PAD), 1)
    swap0 = MLA_HEADS * HEAD_PAD
    for h in range(MLA_HEADS):
        plain = q2[:, h * HEAD_PAD:(h + 1) * HEAD_PAD]
        swapped = q2[:, swap0 + h * HEAD_PAD:swap0 + (h + 1) * HEAD_PAD]
        q_ref[h] = ((plain * cos_t + swapped * sin_t) * (SOFTMAX_SCALE * LOG2E)).astype(bf16)
    kr_swapped = jnp.where(lane < ROPE_LANE0 + HALF_ROPE,
                           pltpu.roll(kr, HEAD_PAD - HALF_ROPE, 1), pltpu.roll(kr, HALF_ROPE, 1))
    k_rope = kr * cos_t + kr_swapped * sin_t
    ones_lane = jnp.where(lane == ONES_LANE, 1.0, 0.0)
    for h in range(MLA_HEADS):
        k_ref[h] = (kv2[:, h * HEAD_PAD:(h + 1) * HEAD_PAD] + k_rope).astype(bf16)
        v_ref[h] = (kv2[:, swap0 + h * HEAD_PAD:swap0 + (h + 1) * HEAD_PAD] + ones_lane).astype(bf16)
    sb_ref[...] = _silu(b_gate).astype(sb_ref.dtype)

    y_a = jnp.concatenate(ys, axis=1) * pool_scale_ref[...] * _silu(a_gate)
    y_c = u * sv * _silu(c_gate)

    def gate(i):
        return _sigmoid(gate_pre[i] + b_gate_ref[:, i * D_MODEL:(i + 1) * D_MODEL])

    g1_ref[...] = gate(1).astype(g1_ref.dtype)
    part_ref[...] = (gate(0) * _dot(y_a.astype(bf16), w_oa_ref[...])
                     + gate(2) * _dot(y_c.astype(bf16), w_oc_ref[...])).astype(part_ref.dtype)


def _const_spec(stacked, l):
    index = (l,) + (0,) * (stacked.ndim - 1)
    return pl.BlockSpec((None,) + stacked.shape[1:], lambda b, j: index, pipeline_mode=pl.Buffered(1))


def _front(x, cos_t, sin_t, p, l):
    B, S, _ = x.shape
    tb = FRONT_BLOCK
    tok = lambda width: pl.BlockSpec((None, tb, width), lambda b, j: (b, j, 0))
    heads = pl.BlockSpec((None, MLA_HEADS, tb, HEAD_PAD), lambda b, j: (b, 0, j, 0))
    consts = [p["norm_pre"], p["w_in"], p["b_gate"], p["pool_w"], p["pool_scale"], p["q_norm"],
              p["w_uq"], p["kv_norm"], p["w_ukv"], p["v_norm_g"], p["v_norm_b"], p["w_s"],
              p["bs_full"], p["w_oa"], p["w_oc"]]
    head_shape = jax.ShapeDtypeStruct((B, MLA_HEADS, S, HEAD_PAD), bf16)
    return pl.pallas_call(
        _front_kernel,
        grid=(B, S // tb),
        in_specs=[tok(D_MODEL), tok(HEAD_PAD), tok(HEAD_PAD)] + [_const_spec(c, l) for c in consts],
        out_specs=[heads, heads, heads, tok(D_MODEL), tok(D_MODEL), tok(MLA_WIDTH)],
        out_shape=[head_shape, head_shape, head_shape,
                   jax.ShapeDtypeStruct((B, S, D_MODEL), bf16),
                   jax.ShapeDtypeStruct((B, S, D_MODEL), bf16),
                   jax.ShapeDtypeStruct((B, S, MLA_WIDTH), bf16)],
        scratch_shapes=[pltpu.VMEM((HALO + tb, POOL_WIDTH), f32)],
        compiler_params=pltpu.CompilerParams(
            dimension_semantics=("arbitrary", "arbitrary"),
            vmem_limit_bytes=V7X_VMEM_BYTES * 7 // 8),
        name="layer_front",
    )(x, cos_t, sin_t, *consts)


def _attn_kernel(q_ref, k_ref, v_ref, o_ref, s_ref, p_ref, m_ref, acc_ref):
    hp, tq, _ = q_ref.shape
    i = pl.program_id(2)
    lane = lax.broadcasted_iota(jnp.int32, (tq, HEAD_PAD), 1)
    n_tiles = tq // V7X_LANES
    half = tq // 2

    m_ref[...] = jnp.full(m_ref.shape, -jnp.inf, f32)
    acc_ref[...] = jnp.zeros(acc_ref.shape, f32)

    nt_dims = (((1,), (1,)), ((), ()))

    def scores(hh, j, slot, diagonal=False):
        start = pl.multiple_of(j * tq, tq)
        if diagonal:
            s_ref[slot, hh, 0:half, 0:half] = lax.dot_general(
                q_ref[hh, 0:half], k_ref[hh, pl.ds(start, half), :], nt_dims, preferred_element_type=f32)
            s_ref[slot, hh, half:tq, :] = lax.dot_general(
                q_ref[hh, half:tq], k_ref[hh, pl.ds(start, tq), :], nt_dims, preferred_element_type=f32)
        else:
            s_ref[slot, hh] = lax.dot_general(q_ref[hh], k_ref[hh, pl.ds(start, tq), :], nt_dims,
                                              preferred_element_type=f32)

    def softmax_pv(hh, j, slot, diagonal):
        for r in range(tq // SOFTMAX_ROWS):
            r0 = r * SOFTMAX_ROWS
            rows = slice(r0, r0 + SOFTMAX_ROWS)
            live = (r0 + SOFTMAX_ROWS - 1) // V7X_LANES + 1 if diagonal else n_tiles
            read = (half if r0 < half else tq) // V7X_LANES if diagonal else n_tiles

            def tile(c):
                s = s_ref[slot, hh, rows, c * V7X_LANES:(c + 1) * V7X_LANES]
                if diagonal and (c + 1) * V7X_LANES - 1 > r0:
                    row = r0 + lax.broadcasted_iota(jnp.int32, s.shape, 0)
                    col = c * V7X_LANES + lax.broadcasted_iota(jnp.int32, s.shape, 1)
                    s = jnp.where(col <= row, s, -jnp.inf)
                return s

            m_blk = tile(0)
            for c in range(1, live):
                m_blk = jnp.maximum(m_blk, tile(c))
            m_old = m_ref[hh, rows]
            m_new = jnp.maximum(m_old, jnp.max(m_blk, axis=-1, keepdims=True))
            m_ref[hh, rows] = m_new
            acc_ref[hh, rows] = jnp.exp2(m_old - m_new) * acc_ref[hh, rows]
            for c in range(live):
                p_ref[hh, rows, c * V7X_LANES:(c + 1) * V7X_LANES] = jnp.exp2(tile(c) - m_new).astype(bf16)
            if live < read:
                p_ref[hh, rows, live * V7X_LANES:read * V7X_LANES] = jnp.zeros(
                    (SOFTMAX_ROWS, (read - live) * V7X_LANES), bf16)
        start = pl.multiple_of(j * tq, tq)
        if diagonal:
            acc_ref[hh, 0:half] += _dot(p_ref[hh, 0:half, 0:half], v_ref[hh, pl.ds(start, half), :])
            acc_ref[hh, half:tq] += _dot(p_ref[hh, half:tq, :], v_ref[hh, pl.ds(start, tq), :])
        else:
            acc_ref[hh] += _dot(p_ref[hh], v_ref[hh, pl.ds(start, tq), :])

    def step(j, slot, next_diagonal=False):
        for hh in range(hp):
            softmax_pv(hh, j, slot, False)
            scores(hh, j + 1, 1 - slot, next_diagonal)

    def last_step(slot):
        for hh in range(hp):
            softmax_pv(hh, i, slot, True)

    @pl.when(i == 0)
    def _():
        for hh in range(hp):
            scores(hh, 0, 0, True)
        last_step(0)

    @pl.when(i > 0)
    def _():
        for hh in range(hp):
            scores(hh, 0, 0)

    def body(t, carry):
        step(2 * t, 0)
        step(2 * t + 1, 1)
        return carry

    lax.fori_loop(0, jnp.maximum(i - 1, 0) // 2, body, 0)

    @pl.when(i % 2 == 1)
    def _():
        step(i - 1, 0, True)
        last_step(1)

    @pl.when((i % 2 == 0) & (i > 0))
    def _():
        step(i - 2, 0)
        step(i - 1, 1, True)
        last_step(0)

    outs = [acc_ref[hh] / acc_ref[hh, :, ONES_LANE:ONES_LANE + 1] for hh in range(hp)]
    o_ref[...] = jnp.where(lane < V_HEAD_DIM, outs[0],
                           pltpu.roll(outs[1], V_HEAD_DIM, 1)).astype(o_ref.dtype)


def _attention(q, k, v):
    B, H, S, _ = q.shape
    tq = ATTN_BLOCK
    hp = HEADS_PER_STEP
    assert hp * V_HEAD_DIM == V7X_LANES
    return pl.pallas_call(
        _attn_kernel,
        grid=(B, H // hp, S // tq),
        in_specs=[pl.BlockSpec((None, hp, tq, HEAD_PAD), lambda b, g, i: (b, g, i, 0)),
                  pl.BlockSpec((None, hp, S, HEAD_PAD), lambda b, g, i: (b, g, 0, 0)),
                  pl.BlockSpec((None, hp, S, HEAD_PAD), lambda b, g, i: (b, g, 0, 0))],
        out_specs=pl.BlockSpec((None, tq, hp * V_HEAD_DIM), lambda b, g, i: (b, i, g)),
        out_shape=jax.ShapeDtypeStruct((B, S, H * V_HEAD_DIM), bf16),
        scratch_shapes=[pltpu.VMEM((2, hp, tq, tq), f32),
                        pltpu.VMEM((hp, tq, tq), bf16),
                        pltpu.VMEM((hp, tq, V7X_LANES), f32),
                        pltpu.VMEM((hp, tq, HEAD_PAD), f32)],
        compiler_params=pltpu.CompilerParams(
            dimension_semantics=("arbitrary", "arbitrary", "arbitrary"),
            vmem_limit_bytes=V7X_VMEM_BYTES * 7 // 8),
        name="mla_attention",
    )(q, k, v)


def _back_kernel(x_ref, o_ref, sb_ref, g1_ref, part_ref, w_ob_ref, w_out_ref, norm_post_ref,
                 out_ref):
    y_b = o_ref[...] * sb_ref[...]
    merged = part_ref[...].astype(f32) + g1_ref[...].astype(f32) * _dot(y_b, w_ob_ref[...])
    z = _dot(merged.astype(bf16), w_out_ref[...])
    out_ref[...] = x_ref[...] + _rms(z, norm_post_ref[...])


def _back(x, o, sb, g1, part, p, l):
    B, S, _ = x.shape
    tb = BACK_BLOCK
    tok = lambda width: pl.BlockSpec((None, tb, width), lambda b, j: (b, j, 0))
    consts = [p["w_ob"], p["w_out"], p["norm_post"]]
    return pl.pallas_call(
        _back_kernel,
        grid=(B, S // tb),
        in_specs=[tok(D_MODEL), tok(MLA_WIDTH), tok(MLA_WIDTH), tok(D_MODEL), tok(D_MODEL)]
        + [_const_spec(c, l) for c in consts],
        out_specs=tok(D_MODEL),
        out_shape=jax.ShapeDtypeStruct((B, S, D_MODEL), f32),
        compiler_params=pltpu.CompilerParams(
            dimension_semantics=("arbitrary", "arbitrary"),
            vmem_limit_bytes=V7X_VMEM_BYTES // 2),
        name="layer_back",
    )(x, o, sb, g1, part, *consts)


def _stacked_params(norm_pre, norm_post, w_in, b_gate, pool_w, pool_scale, q_norm, w_uq,
                    kv_norm, w_ukv, v_norm_g, v_norm_b, w_s, b_s, w_oa, w_ob, w_oc, w_out):
    L = w_in.shape[0]
    offs = np.cumsum((0,) + IN_SPLITS)
    seg = [w_in[..., offs[i]:offs[i + 1]].astype(bf16) for i in range(len(IN_SPLITS))]
    zeros = lambda *shape: jnp.zeros(shape, bf16)
    kr_cols = jnp.concatenate([zeros(L, D_MODEL, ROPE_LANE0), seg[4],
                               zeros(L, D_MODEL, HEAD_PAD - ROPE_LANE0 - QK_ROPE_DIM)], axis=-1)
    w_in_p = jnp.concatenate(seg[:4] + [kr_cols] + seg[5:], axis=-1)
    assert w_in_p.shape == (L, D_MODEL, IN_WIDTH_PAD)

    uq = w_uq.astype(bf16).reshape(L, Q_LORA_RANK, MLA_HEADS, QK_NOPE_DIM + QK_ROPE_DIM)
    nope, x1, x2 = uq[..., :QK_NOPE_DIM], uq[..., QK_NOPE_DIM:QK_NOPE_DIM + HALF_ROPE], uq[..., QK_NOPE_DIM + HALF_ROPE:]
    pad = zeros(L, Q_LORA_RANK, MLA_HEADS, HEAD_PAD - QK_NOPE_DIM - QK_ROPE_DIM)
    plain = jnp.concatenate([nope, x1, x2, pad], axis=-1).reshape(L, Q_LORA_RANK, MLA_HEADS * HEAD_PAD)
    swapped = jnp.concatenate([jnp.zeros_like(nope), x2, x1, pad], axis=-1).reshape(L, Q_LORA_RANK, MLA_HEADS * HEAD_PAD)
    w_uq_p = jnp.concatenate([plain, swapped], axis=-1)

    ukv = w_ukv.astype(bf16).reshape(L, KV_LORA_RANK, MLA_HEADS, QK_NOPE_DIM + V_HEAD_DIM)
    kpad = zeros(L, KV_LORA_RANK, MLA_HEADS, HEAD_PAD - QK_NOPE_DIM)
    vpad = zeros(L, KV_LORA_RANK, MLA_HEADS, HEAD_PAD - V_HEAD_DIM)
    k_cols = jnp.concatenate([ukv[..., :QK_NOPE_DIM], kpad], axis=-1).reshape(L, KV_LORA_RANK, MLA_HEADS * HEAD_PAD)
    v_cols = jnp.concatenate([ukv[..., QK_NOPE_DIM:], vpad], axis=-1).reshape(L, KV_LORA_RANK, MLA_HEADS * HEAD_PAD)
    w_ukv_p = jnp.concatenate([k_cols, v_cols], axis=-1)

    row = lambda a: a.reshape(L, 1, -1)
    return {
        "norm_pre": row(norm_pre), "norm_post": row(norm_post), "w_in": w_in_p,
        "b_gate": row(b_gate), "pool_w": pool_w.astype(bf16), "pool_scale": row(pool_scale),
        "q_norm": row(q_norm), "w_uq": w_uq_p, "kv_norm": row(kv_norm), "w_ukv": w_ukv_p,
        "v_norm_g": row(v_norm_g), "v_norm_b": row(v_norm_b), "w_s": w_s.astype(bf16),
        "bs_full": jnp.repeat(jnp.swapaxes(b_s, 1, 2), GMLP_GROUP_DIM, axis=2),
        "w_oa": w_oa.astype(bf16), "w_ob": w_ob.astype(bf16), "w_oc": w_oc.astype(bf16),
        "w_out": w_out.astype(bf16),
    }


def kernel(x, positions, norm_pre, norm_post, w_in, b_gate, pool_w, pool_scale, q_norm, w_uq, kv_norm, w_ukv, v_norm_g, v_norm_b, w_s, b_s, w_oa, w_ob, w_oc, w_out):
    B, S, _ = x.shape
    cos_t, sin_t = _rope_tables(positions)
    cos_t = cos_t.reshape(B, S, HEAD_PAD)
    sin_t = sin_t.reshape(B, S, HEAD_PAD)
    p = _stacked_params(norm_pre, norm_post, w_in, b_gate, pool_w, pool_scale, q_norm, w_uq,
                        kv_norm, w_ukv, v_norm_g, v_norm_b, w_s, b_s, w_oa, w_ob, w_oc, w_out)
    for l in range(norm_pre.shape[0]):
        q, k, v, part, g1, sb = _front(x, cos_t, sin_t, p, l)
        o = _attention(q, k, v)
        x = _back(x, o, sb, g1, part, p, l)
    return x
```

```python
import functools
import math

import jax
import jax.numpy as jnp
import numpy as np
from jax import lax
from jax.experimental import pallas as pl
from jax.experimental.pallas import tpu as pltpu

D_MODEL = 1024
POOL_WIDTH = 512
POOL_GROUPS = 4
POOL_GROUP_DIM = POOL_WIDTH // POOL_GROUPS
POOL_WINDOWS = (2, 4, 8, 16)
MLA_HEADS = 8
QK_NOPE_DIM = 64
QK_ROPE_DIM = 32
HALF_ROPE = QK_ROPE_DIM // 2
V_HEAD_DIM = 64
Q_LORA_RANK = 256
KV_LORA_RANK = 128
MLA_WIDTH = MLA_HEADS * V_HEAD_DIM
ROPE_THETA = 10000.0
GMLP_WIDTH = 512
GMLP_CHUNK = 128
GMLP_GROUPS = 8
GMLP_GROUP_DIM = GMLP_WIDTH // GMLP_GROUPS
N_BRANCHES = 3
EPS = 1e-6
IN_SPLITS = (POOL_WIDTH, POOL_WIDTH, Q_LORA_RANK, KV_LORA_RANK, QK_ROPE_DIM, MLA_WIDTH,
             GMLP_WIDTH, GMLP_WIDTH, GMLP_WIDTH, N_BRANCHES * D_MODEL)

V7X_LANES = 128
V7X_VMEM_BYTES = 64 * 1024 * 1024

HEAD_PAD = V7X_LANES
ROPE_LANE0 = QK_NOPE_DIM
ONES_LANE = V_HEAD_DIM
HALO = max(POOL_WINDOWS)

COL_A_IN = 0
COL_A_GATE = COL_A_IN + POOL_WIDTH
COL_CQ = COL_A_GATE + POOL_WIDTH
COL_CKV = COL_CQ + Q_LORA_RANK
COL_KR = COL_CKV + KV_LORA_RANK
COL_B_GATE = COL_KR + HEAD_PAD
COL_U = COL_B_GATE + MLA_WIDTH
COL_V = COL_U + GMLP_WIDTH
COL_C_GATE = COL_V + GMLP_WIDTH
COL_GATES = COL_C_GATE + GMLP_WIDTH
IN_WIDTH_PAD = COL_GATES + N_BRANCHES * D_MODEL

SOFTMAX_SCALE = (QK_NOPE_DIM + QK_ROPE_DIM) ** -0.5
LOG2E = math.log2(math.e)

FRONT_BLOCK = 512
ATTN_BLOCK = 1024
BACK_BLOCK = 512
HEADS_PER_STEP = 2
SOFTMAX_ROWS = 64

bf16 = jnp.bfloat16
f32 = jnp.float32


def _sigmoid(z):
    return 0.5 * jnp.tanh(0.5 * z) + 0.5


def _silu(z):
    return z * _sigmoid(z)


def _rms(x, g):
    return x * lax.rsqrt(jnp.mean(x * x, axis=-1, keepdims=True) + EPS) * g


def _dot(a, b):
    return jnp.dot(a, b, preferred_element_type=f32)


def _rope_table_kernel(pos_ref, invf_ref, cos_ref, sin_ref):
    ang = pos_ref[...].astype(f32) * invf_ref[...]
    cos_ref[...] = jnp.cos(ang)
    sin_ref[...] = jnp.sin(ang)


def _rope_tables(positions):
    n = positions.size
    per_row = V7X_LANES // HALF_ROPE
    inv_freq = ROPE_THETA ** (-jnp.arange(0, QK_ROPE_DIM, 2, dtype=f32) / QK_ROPE_DIM)
    invf = jnp.tile(inv_freq, per_row).reshape(1, V7X_LANES)
    pos = jnp.repeat(positions.reshape(n // per_row, per_row), HALF_ROPE, axis=1)
    dense = jax.ShapeDtypeStruct((n // per_row, V7X_LANES), f32)
    cos_d, sin_d = pl.pallas_call(
        _rope_table_kernel,
        out_shape=[dense, dense],
        name="rope_tables",
    )(pos, invf)
    c = cos_d.reshape(n, HALF_ROPE)
    s = sin_d.reshape(n, HALF_ROPE)
    pad = jnp.zeros((n, HEAD_PAD - ROPE_LANE0 - QK_ROPE_DIM), f32)
    cos_t = jnp.concatenate([jnp.ones((n, ROPE_LANE0), f32), c, c, pad], axis=1)
    sin_t = jnp.concatenate([jnp.zeros((n, ROPE_LANE0), f32), -s, s, pad], axis=1)
    return cos_t, sin_t


def _front_kernel(x_ref, x_next_ref, cos_ref, sin_ref, norm_pre_ref, w_in_ref, b_gate_ref, pool_w_ref,
                  pool_scale_ref, q_norm_ref, w_uq_ref, kv_norm_ref, w_ukv_ref, vn_g_ref,
                  vn_b_ref, w_s_ref, bs_ref, w_oa_ref, w_oc_ref,
                  q_ref, k_ref, v_ref, part_ref, g1_ref, sb_ref, abuf, hb_buf):
    tb = x_ref.shape[0]
    j = pl.program_id(1)

    @pl.when((pl.program_id(0) == 0) & (j == 0))
    def _():
        hb_buf[...] = _rms(x_ref[...], norm_pre_ref[...]).astype(bf16)

    def proj(c0, width):
        return _dot(hb_buf[...], w_in_ref[:, c0:c0 + width])

    @pl.when(j == 0)
    def _():
        abuf[0:HALO, :] = jnp.zeros((HALO, POOL_WIDTH), f32)

    a_in = proj(COL_A_IN, POOL_WIDTH)
    abuf[HALO:HALO + tb, :] = a_in
    cq = proj(COL_CQ, Q_LORA_RANK)
    ckv = proj(COL_CKV, KV_LORA_RANK)
    kr = proj(COL_KR, HEAD_PAD)
    vv = proj(COL_V, GMLP_WIDTH)
    a_gate = proj(COL_A_GATE, POOL_WIDTH)
    b_gate = proj(COL_B_GATE, MLA_WIDTH)
    u = proj(COL_U, GMLP_WIDTH)
    c_gate = proj(COL_C_GATE, GMLP_WIDTH)

    t = j * tb + lax.broadcasted_iota(jnp.int32, (tb, 1), 0)
    ys = []
    for g, w in enumerate(POOL_WINDOWS):
        c0 = g * POOL_GROUP_DIM
        win = abuf[HALO:HALO + tb, c0:c0 + POOL_GROUP_DIM]
        for i in range(1, w):
            win = win + abuf[HALO - i:HALO - i + tb, c0:c0 + POOL_GROUP_DIM]
        count = jnp.minimum(t + 1, w).astype(f32)
        mixed = win / count - a_in[:, c0:c0 + POOL_GROUP_DIM]
        ys.append(_dot(mixed.astype(bf16), pool_w_ref[g]))
    abuf[0:HALO, :] = abuf[tb:tb + HALO, :]

    cqn = _rms(cq, q_norm_ref[...]).astype(bf16)
    q2 = _dot(cqn, w_uq_ref[...])
    ckvn = _rms(ckv, kv_norm_ref[...]).astype(bf16)
    kv2 = _dot(ckvn, w_ukv_ref[...])

    vc = vv - jnp.mean(vv, axis=-1, keepdims=True)
    var = jnp.mean(vc * vc, axis=-1, keepdims=True)
    vb = (vc * lax.rsqrt(var + EPS) * vn_g_ref[...] + vn_b_ref[...]).astype(bf16)
    row = lax.broadcasted_iota(jnp.int32, (GMLP_CHUNK, GMLP_CHUNK), 0)
    col = lax.broadcasted_iota(jnp.int32, (GMLP_CHUNK, GMLP_CHUNK), 1)
    ws = [jnp.where(col <= row, w_s_ref[g], jnp.zeros((), bf16)) for g in range(GMLP_GROUPS)]
    groups_per_tile = V7X_LANES // GMLP_GROUP_DIM
    assert groups_per_tile == 2
    chunks = []
    for c in range(tb // GMLP_CHUNK):
        tiles = []
        for tile in range(GMLP_WIDTH // V7X_LANES):
            rhs = vb[c * GMLP_CHUNK:(c + 1) * GMLP_CHUNK, tile * V7X_LANES:(tile + 1) * V7X_LANES]
            lo = _dot(ws[2 * tile], rhs)
            hi = _dot(ws[2 * tile + 1], rhs)
            tiles.append(jnp.where(col < GMLP_GROUP_DIM, lo, hi))
        chunks.append(jnp.concatenate(tiles, axis=1) + bs_ref[...])
    sv = jnp.concatenate(chunks, axis=0)

    gate_pre = [proj(COL_GATES + i * D_MODEL, D_MODEL) for i in range(N_BRANCHES)]

    hb_buf[...] = _rms(x_next_ref[...], norm_pre_ref[...]).astype(bf16)

    cos_t = cos_ref[...]
    sin_t = sin_ref[...]
    lane = lax.broadcasted_iota(jnp.int32, (tb, HEAD_PAD), 1)
    swap0 = MLA_HEADS * HEAD_PAD
    for h in range(MLA_HEADS):
        plain = q2[:, h * HEAD_PAD:(h + 1) * HEAD_PAD]
        swapped = q2[:, swap0 + h * HEAD_PAD:swap0 + (h + 1) * HEAD_PAD]
        q_ref[h] = ((plain * cos_t + swapped * sin_t) * (SOFTMAX_SCALE * LOG2E)).astype(bf16)
    kr_swapped = jnp.where(lane < ROPE_LANE0 + HALF_ROPE,
                           pltpu.roll(kr, HEAD_PAD - HALF_ROPE, 1), pltpu.roll(kr, HALF_ROPE, 1))
    k_rope = kr * cos_t + kr_swapped * sin_t
    ones_lane = jnp.where(lane == ONES_LANE, 1.0, 0.0)
    for h in range(MLA_HEADS):
        k_ref[h] = (kv2[:, h * HEAD_PAD:(h + 1) * HEAD_PAD] + k_rope).astype(bf16)
        v_ref[h] = (kv2[:, swap0 + h * HEAD_PAD:swap0 + (h + 1) * HEAD_PAD] + ones_lane).astype(bf16)
    sb_ref[...] = _silu(b_gate).astype(sb_ref.dtype)

    y_a = jnp.concatenate(ys, axis=1) * pool_scale_ref[...] * _silu(a_gate)
    y_c = u * sv * _silu(c_gate)

    def gate(i):
        return _sigmoid(gate_pre[i] + b_gate_ref[:, i * D_MODEL:(i + 1) * D_MODEL])

    g1_ref[...] = gate(1).astype(g1_ref.dtype)
    part_ref[...] = (gate(0) * _dot(y_a.astype(bf16), w_oa_ref[...])
                     + gate(2) * _dot(y_c.astype(bf16), w_oc_ref[...])).astype(part_ref.dtype)


def _const_spec(stacked, l):
    index = (l,) + (0,) * (stacked.ndim - 1)
    return pl.BlockSpec((None,) + stacked.shape[1:], lambda b, j: index, pipeline_mode=pl.Buffered(1))


def _front(x, cos_t, sin_t, p, l):
    B, S, _ = x.shape
    tb = FRONT_BLOCK
    tok = lambda width: pl.BlockSpec((None, tb, width), lambda b, j: (b, j, 0))
    heads = pl.BlockSpec((None, MLA_HEADS, tb, HEAD_PAD), lambda b, j: (b, 0, j, 0))
    steps = S // tb

    def next_block(b, j):
        flat = jnp.minimum(b * steps + j + 1, B * steps - 1)
        return flat // steps, flat % steps, 0

    x_next = pl.BlockSpec((None, tb, D_MODEL), next_block)
    consts = [p["norm_pre"], p["w_in"], p["b_gate"], p["pool_w"], p["pool_scale"], p["q_norm"],
              p["w_uq"], p["kv_norm"], p["w_ukv"], p["v_norm_g"], p["v_norm_b"], p["w_s"],
              p["bs_full"], p["w_oa"], p["w_oc"]]
    head_shape = jax.ShapeDtypeStruct((B, MLA_HEADS, S, HEAD_PAD), bf16)
    return pl.pallas_call(
        _front_kernel,
        grid=(B, S // tb),
        in_specs=[tok(D_MODEL), x_next, tok(HEAD_PAD), tok(HEAD_PAD)] + [_const_spec(c, l) for c in consts],
        out_specs=[heads, heads, heads, tok(D_MODEL), tok(D_MODEL), tok(MLA_WIDTH)],
        out_shape=[head_shape, head_shape, head_shape,
                   jax.ShapeDtypeStruct((B, S, D_MODEL), bf16),
                   jax.ShapeDtypeStruct((B, S, D_MODEL), bf16),
                   jax.ShapeDtypeStruct((B, S, MLA_WIDTH), bf16)],
        scratch_shapes=[pltpu.VMEM((HALO + tb, POOL_WIDTH), f32),
                        pltpu.VMEM((tb, D_MODEL), bf16)],
        compiler_params=pltpu.CompilerParams(
            dimension_semantics=("arbitrary", "arbitrary"),
            vmem_limit_bytes=V7X_VMEM_BYTES * 7 // 8),
        name="layer_front",
    )(x, x, cos_t, sin_t, *consts)


def _attn_kernel(q_ref, k_ref, v_ref, o_ref, s_ref, p_ref, m_ref, acc_ref):
    hp, tq, _ = q_ref.shape
    i = pl.program_id(2)
    lane = lax.broadcasted_iota(jnp.int32, (tq, HEAD_PAD), 1)
    n_tiles = tq // V7X_LANES
    half = tq // 2

    m_ref[...] = jnp.full(m_ref.shape, -jnp.inf, f32)
    acc_ref[...] = jnp.zeros(acc_ref.shape, f32)

    nt_dims = (((1,), (1,)), ((), ()))

    def scores(hh, j, slot, diagonal=False):
        start = pl.multiple_of(j * tq, tq)
        if diagonal:
            s_ref[slot, hh, 0:half, 0:half] = lax.dot_general(
                q_ref[hh, 0:half], k_ref[hh, pl.ds(start, half), :], nt_dims, preferred_element_type=f32)
            s_ref[slot, hh, half:tq, :] = lax.dot_general(
                q_ref[hh, half:tq], k_ref[hh, pl.ds(start, tq), :], nt_dims, preferred_element_type=f32)
        else:
            s_ref[slot, hh] = lax.dot_general(q_ref[hh], k_ref[hh, pl.ds(start, tq), :], nt_dims,
                                              preferred_element_type=f32)

    def softmax_pv(hh, j, slot, diagonal):
        for r in range(tq // SOFTMAX_ROWS):
            r0 = r * SOFTMAX_ROWS
            rows = slice(r0, r0 + SOFTMAX_ROWS)
            live = (r0 + SOFTMAX_ROWS - 1) // V7X_LANES + 1 if diagonal else n_tiles
            read = (half if r0 < half else tq) // V7X_LANES if diagonal else n_tiles

            def tile(c):
                s = s_ref[slot, hh, rows, c * V7X_LANES:(c + 1) * V7X_LANES]
                if diagonal and (c + 1) * V7X_LANES - 1 > r0:
                    row = r0 + lax.broadcasted_iota(jnp.int32, s.shape, 0)
                    col = c * V7X_LANES + lax.broadcasted_iota(jnp.int32, s.shape, 1)
                    s = jnp.where(col <= row, s, -jnp.inf)
                return s

            m_blk = tile(0)
            for c in range(1, live):
                m_blk = jnp.maximum(m_blk, tile(c))
            m_old = m_ref[hh, rows]
            m_new = jnp.maximum(m_old, jnp.max(m_blk, axis=-1, keepdims=True))
            m_ref[hh, rows] = m_new
            acc_ref[hh, rows] = jnp.exp2(m_old - m_new) * acc_ref[hh, rows]
            for c in range(live):
                p_ref[hh, rows, c * V7X_LANES:(c + 1) * V7X_LANES] = jnp.exp2(tile(c) - m_new).astype(bf16)
            if live < read:
                p_ref[hh, rows, live * V7X_LANES:read * V7X_LANES] = jnp.zeros(
                    (SOFTMAX_ROWS, (read - live) * V7X_LANES), bf16)
        start = pl.multiple_of(j * tq, tq)
        if diagonal:
            acc_ref[hh, 0:half] += _dot(p_ref[hh, 0:half, 0:half], v_ref[hh, pl.ds(start, half), :])
            acc_ref[hh, half:tq] += _dot(p_ref[hh, half:tq, :], v_ref[hh, pl.ds(start, tq), :])
        else:
            acc_ref[hh] += _dot(p_ref[hh], v_ref[hh, pl.ds(start, tq), :])

    def step(j, slot, next_diagonal=False):
        for hh in range(hp):
            softmax_pv(hh, j, slot, False)
            scores(hh, j + 1, 1 - slot, next_diagonal)

    def last_step(slot):
        for hh in range(hp):
            softmax_pv(hh, i, slot, True)

    @pl.when(i == 0)
    def _():
        for hh in range(hp):
            scores(hh, 0, 0, True)
        last_step(0)

    @pl.when(i > 0)
    def _():
        for hh in range(hp):
            scores(hh, 0, 0)

    def body(t, carry):
        step(2 * t, 0)
        step(2 * t + 1, 1)
        return carry

    lax.fori_loop(0, jnp.maximum(i - 1, 0) // 2, body, 0)

    @pl.when(i % 2 == 1)
    def _():
        step(i - 1, 0, True)
        last_step(1)

    @pl.when((i % 2 == 0) & (i > 0))
    def _():
        step(i - 2, 0)
        step(i - 1, 1, True)
        last_step(0)

    outs = [acc_ref[hh] / acc_ref[hh, :, ONES_LANE:ONES_LANE + 1] for hh in range(hp)]
    o_ref[...] = jnp.where(lane < V_HEAD_DIM, outs[0],
                           pltpu.roll(outs[1], V_HEAD_DIM, 1)).astype(o_ref.dtype)


def _attention(q, k, v):
    B, H, S, _ = q.shape
    tq = ATTN_BLOCK
    hp = HEADS_PER_STEP
    assert hp * V_HEAD_DIM == V7X_LANES
    return pl.pallas_call(
        _attn_kernel,
        grid=(B, H // hp, S // tq),
        in_specs=[pl.BlockSpec((None, hp, tq, HEAD_PAD), lambda b, g, i: (b, g, i, 0)),
                  pl.BlockSpec((None, hp, S, HEAD_PAD), lambda b, g, i: (b, g, 0, 0)),
                  pl.BlockSpec((None, hp, S, HEAD_PAD), lambda b, g, i: (b, g, 0, 0))],
        out_specs=pl.BlockSpec((None, tq, hp * V_HEAD_DIM), lambda b, g, i: (b, i, g)),
        out_shape=jax.ShapeDtypeStruct((B, S, H * V_HEAD_DIM), bf16),
        scratch_shapes=[pltpu.VMEM((2, hp, tq, tq), f32),
                        pltpu.VMEM((hp, tq, tq), bf16),
                        pltpu.VMEM((hp, tq, V7X_LANES), f32),
                        pltpu.VMEM((hp, tq, HEAD_PAD), f32)],
        compiler_params=pltpu.CompilerParams(
            dimension_semantics=("arbitrary", "arbitrary", "arbitrary"),
            vmem_limit_bytes=V7X_VMEM_BYTES * 7 // 8),
        name="mla_attention",
    )(q, k, v)


def _back_kernel(x_ref, o_ref, sb_ref, g1_ref, part_ref, w_ob_ref, w_out_ref, norm_post_ref,
                 out_ref):
    y_b = o_ref[...] * sb_ref[...]
    merged = part_ref[...].astype(f32) + g1_ref[...].astype(f32) * _dot(y_b, w_ob_ref[...])
    z = _dot(merged.astype(bf16), w_out_ref[...])
    out_ref[...] = x_ref[...] + _rms(z, norm_post_ref[...])


def _back(x, o, sb, g1, part, p, l):
    B, S, _ = x.shape
    tb = BACK_BLOCK
    tok = lambda width: pl.BlockSpec((None, tb, width), lambda b, j: (b, j, 0))
    consts = [p["w_ob"], p["w_out"], p["norm_post"]]
    return pl.pallas_call(
        _back_kernel,
        grid=(B, S // tb),
        in_specs=[tok(D_MODEL), tok(MLA_WIDTH), tok(MLA_WIDTH), tok(D_MODEL), tok(D_MODEL)]
        + [_const_spec(c, l) for c in consts],
        out_specs=tok(D_MODEL),
        out_shape=jax.ShapeDtypeStruct((B, S, D_MODEL), f32),
        compiler_params=pltpu.CompilerParams(
            dimension_semantics=("arbitrary", "arbitrary"),
            vmem_limit_bytes=V7X_VMEM_BYTES // 2),
        name="layer_back",
    )(x, o, sb, g1, part, *consts)


def _stacked_params(norm_pre, norm_post, w_in, b_gate, pool_w, pool_scale, q_norm, w_uq,
                    kv_norm, w_ukv, v_norm_g, v_norm_b, w_s, b_s, w_oa, w_ob, w_oc, w_out):
    L = w_in.shape[0]
    offs = np.cumsum((0,) + IN_SPLITS)
    seg = [w_in[..., offs[i]:offs[i + 1]].astype(bf16) for i in range(len(IN_SPLITS))]
    zeros = lambda *shape: jnp.zeros(shape, bf16)
    kr_cols = jnp.concatenate([zeros(L, D_MODEL, ROPE_LANE0), seg[4],
                               zeros(L, D_MODEL, HEAD_PAD - ROPE_LANE0 - QK_ROPE_DIM)], axis=-1)
    w_in_p = jnp.concatenate(seg[:4] + [kr_cols] + seg[5:], axis=-1)
    assert w_in_p.shape == (L, D_MODEL, IN_WIDTH_PAD)

    uq = w_uq.astype(bf16).reshape(L, Q_LORA_RANK, MLA_HEADS, QK_NOPE_DIM + QK_ROPE_DIM)
    nope, x1, x2 = uq[..., :QK_NOPE_DIM], uq[..., QK_NOPE_DIM:QK_NOPE_DIM + HALF_ROPE], uq[..., QK_NOPE_DIM + HALF_ROPE:]
    pad = zeros(L, Q_LORA_RANK, MLA_HEADS, HEAD_PAD - QK_NOPE_DIM - QK_ROPE_DIM)
    plain = jnp.concatenate([nope, x1, x2, pad], axis=-1).reshape(L, Q_LORA_RANK, MLA_HEADS * HEAD_PAD)
    swapped = jnp.concatenate([jnp.zeros_like(nope), x2, x1, pad], axis=-1).reshape(L, Q_LORA_RANK, MLA_HEADS * HEAD_PAD)
    w_uq_p = jnp.concatenate([plain, swapped], axis=-1)

    ukv = w_ukv.astype(bf16).reshape(L, KV_LORA_RANK, MLA_HEADS, QK_NOPE_DIM + V_HEAD_DIM)
    kpad = zeros(L, KV_LORA_RANK, MLA_HEADS, HEAD_PAD - QK_NOPE_DIM)
    vpad = zeros(L, KV_LORA_RANK, MLA_HEADS, HEAD_PAD - V_HEAD_DIM)
    k_cols = jnp.concatenate([ukv[..., :QK_NOPE_DIM], kpad], axis=-1).reshape(L, KV_LORA_RANK, MLA_HEADS * HEAD_PAD)
    v_cols = jnp.concatenate([ukv[..., QK_NOPE_DIM:], vpad], axis=-1).reshape(L, KV_LORA_RANK, MLA_HEADS * HEAD_PAD)
    w_ukv_p = jnp.concatenate([k_cols, v_cols], axis=-1)

    row = lambda a: a.reshape(L, 1, -1)
    return {
        "norm_pre": row(norm_pre), "norm_post": row(norm_post), "w_in": w_in_p,
        "b_gate": row(b_gate), "pool_w": pool_w.astype(bf16), "pool_scale": row(pool_scale),
        "q_norm": row(q_norm), "w_uq": w_uq_p, "kv_norm": row(kv_norm), "w_ukv": w_ukv_p,
        "v_norm_g": row(v_norm_g), "v_norm_b": row(v_norm_b), "w_s": w_s.astype(bf16),
        "bs_full": jnp.repeat(jnp.swapaxes(b_s, 1, 2), GMLP_GROUP_DIM, axis=2),
        "w_oa": w_oa.astype(bf16), "w_ob": w_ob.astype(bf16), "w_oc": w_oc.astype(bf16),
        "w_out": w_out.astype(bf16),
    }


def kernel(x, positions, norm_pre, norm_post, w_in, b_gate, pool_w, pool_scale, q_norm, w_uq, kv_norm, w_ukv, v_norm_g, v_norm_b, w_s, b_s, w_oa, w_ob, w_oc, w_out):
    B, S, _ = x.shape
    cos_t, sin_t = _rope_tables(positions)
    cos_t = cos_t.reshape(B, S, HEAD_PAD)
    sin_t = sin_t.reshape(B, S, HEAD_PAD)
    p = _stacked_params(norm_pre, norm_post, w_in, b_gate, pool_w, pool_scale, q_norm, w_uq,
                        kv_norm, w_ukv, v_norm_g, v_norm_b, w_s, b_s, w_oa, w_ob, w_oc, w_out)
    for l in range(norm_pre.shape[0]):
        q, k, v, part, g1, sb = _front(x, cos_t, sin_t, p, l)
        o = _attention(q, k, v)
        x = _back(x, o, sb, g1, part, p, l)
    return x
```

```python
import functools
import math

import jax
import jax.numpy as jnp
import numpy as np
from jax import lax
from jax.experimental import pallas as pl
from jax.experimental.pallas import tpu as pltpu

D_MODEL = 1024
POOL_WIDTH = 512
POOL_GROUPS = 4
POOL_GROUP_DIM = POOL_WIDTH // POOL_GROUPS
POOL_WINDOWS = (2, 4, 8, 16)
MLA_HEADS = 8
QK_NOPE_DIM = 64
QK_ROPE_DIM = 32
HALF_ROPE = QK_ROPE_DIM // 2
V_HEAD_DIM = 64
Q_LORA_RANK = 256
KV_LORA_RANK = 128
MLA_WIDTH = MLA_HEADS * V_HEAD_DIM
ROPE_THETA = 10000.0
GMLP_WIDTH = 512
GMLP_CHUNK = 128
GMLP_GROUPS = 8
GMLP_GROUP_DIM = GMLP_WIDTH // GMLP_GROUPS
N_BRANCHES = 3
EPS = 1e-6
IN_SPLITS = (POOL_WIDTH, POOL_WIDTH, Q_LORA_RANK, KV_LORA_RANK, QK_ROPE_DIM, MLA_WIDTH,
             GMLP_WIDTH, GMLP_WIDTH, GMLP_WIDTH, N_BRANCHES * D_MODEL)

V7X_LANES = 128
V7X_VMEM_BYTES = 64 * 1024 * 1024

HEAD_PAD = V7X_LANES
ROPE_LANE0 = QK_NOPE_DIM
ONES_LANE = V_HEAD_DIM
HALO = max(POOL_WINDOWS)

COL_A_IN = 0
COL_A_GATE = COL_A_IN + POOL_WIDTH
COL_CQ = COL_A_GATE + POOL_WIDTH
COL_CKV = COL_CQ + Q_LORA_RANK
COL_KR = COL_CKV + KV_LORA_RANK
COL_B_GATE = COL_KR + HEAD_PAD
COL_U = COL_B_GATE + MLA_WIDTH
COL_V = COL_U + GMLP_WIDTH
COL_C_GATE = COL_V + GMLP_WIDTH
COL_GATES = COL_C_GATE + GMLP_WIDTH
IN_WIDTH_PAD = COL_GATES + N_BRANCHES * D_MODEL

SOFTMAX_SCALE = (QK_NOPE_DIM + QK_ROPE_DIM) ** -0.5
LOG2E = math.log2(math.e)

FRONT_BLOCK = 512
ATTN_BLOCK = 1024
BACK_BLOCK = 512
HEADS_PER_STEP = 2
SOFTMAX_ROWS = 64

bf16 = jnp.bfloat16
f32 = jnp.float32


def _sigmoid(z):
    return 0.5 * jnp.tanh(0.5 * z) + 0.5


def _silu(z):
    return z * _sigmoid(z)


def _rms(x, g):
    return x * lax.rsqrt(jnp.mean(x * x, axis=-1, keepdims=True) + EPS) * g


def _dot(a, b):
    return jnp.dot(a, b, preferred_element_type=f32)


def _rope_table_kernel(pos_ref, invf_ref, cos_ref, sin_ref):
    ang = pos_ref[...].astype(f32) * invf_ref[...]
    cos_ref[...] = jnp.cos(ang)
    sin_ref[...] = jnp.sin(ang)


def _rope_tables(positions):
    n = positions.size
    per_row = V7X_LANES // HALF_ROPE
    inv_freq = ROPE_THETA ** (-jnp.arange(0, QK_ROPE_DIM, 2, dtype=f32) / QK_ROPE_DIM)
    invf = jnp.tile(inv_freq, per_row).reshape(1, V7X_LANES)
    pos = jnp.repeat(positions.reshape(n // per_row, per_row), HALF_ROPE, axis=1)
    dense = jax.ShapeDtypeStruct((n // per_row, V7X_LANES), f32)
    cos_d, sin_d = pl.pallas_call(
        _rope_table_kernel,
        out_shape=[dense, dense],
        name="rope_tables",
    )(pos, invf)
    c = cos_d.reshape(n, HALF_ROPE)
    s = sin_d.reshape(n, HALF_ROPE)
    pad = jnp.zeros((n, HEAD_PAD - ROPE_LANE0 - QK_ROPE_DIM), f32)
    cos_t = jnp.concatenate([jnp.ones((n, ROPE_LANE0), f32), c, c, pad], axis=1)
    sin_t = jnp.concatenate([jnp.zeros((n, ROPE_LANE0), f32), -s, s, pad], axis=1)
    return cos_t, sin_t


def _front_kernel(x_ref, cos_ref, sin_ref, norm_pre_ref, w_in_ref, b_gate_ref, pool_w_ref,
                  pool_scale_ref, q_norm_ref, w_uq_ref, kv_norm_ref, w_ukv_ref, vn_g_ref,
                  vn_b_ref, w_s_ref, bs_ref, w_oa_ref, w_oc_ref,
                  q_ref, k_ref, v_ref, part_ref, g1_ref, sb_ref, abuf):
    tb = x_ref.shape[0]
    j = pl.program_id(1)

    hb = _rms(x_ref[...], norm_pre_ref[...]).astype(bf16)

    def proj(c0, width):
        return _dot(hb, w_in_ref[:, c0:c0 + width])

    @pl.when(j == 0)
    def _():
        abuf[0:HALO, :] = jnp.zeros((HALO, POOL_WIDTH), f32)

    a_in = proj(COL_A_IN, POOL_WIDTH)
    abuf[HALO:HALO + tb, :] = a_in
    cq = proj(COL_CQ, Q_LORA_RANK)
    ckv = proj(COL_CKV, KV_LORA_RANK)
    kr = proj(COL_KR, HEAD_PAD)
    vv = proj(COL_V, GMLP_WIDTH)
    a_gate = proj(COL_A_GATE, POOL_WIDTH)
    b_gate = proj(COL_B_GATE, MLA_WIDTH)
    u = proj(COL_U, GMLP_WIDTH)
    c_gate = proj(COL_C_GATE, GMLP_WIDTH)

    t = j * tb + lax.broadcasted_iota(jnp.int32, (tb, 1), 0)
    ys = []
    for g, w in enumerate(POOL_WINDOWS):
        c0 = g * POOL_GROUP_DIM
        win = abuf[HALO:HALO + tb, c0:c0 + POOL_GROUP_DIM]
        for i in range(1, w):
            win = win + abuf[HALO - i:HALO - i + tb, c0:c0 + POOL_GROUP_DIM]
        count = jnp.minimum(t + 1, w).astype(f32)
        mixed = win / count - a_in[:, c0:c0 + POOL_GROUP_DIM]
        ys.append(_dot(mixed.astype(bf16), pool_w_ref[g]))
    abuf[0:HALO, :] = abuf[tb:tb + HALO, :]

    cqn = _rms(cq, q_norm_ref[...]).astype(bf16)
    q2 = _dot(cqn, w_uq_ref[...])
    ckvn = _rms(ckv, kv_norm_ref[...]).astype(bf16)
    kv2 = _dot(ckvn, w_ukv_ref[...])

    vc = vv - jnp.mean(vv, axis=-1, keepdims=True)
    var = jnp.mean(vc * vc, axis=-1, keepdims=True)
    vb = (vc * lax.rsqrt(var + EPS) * vn_g_ref[...] + vn_b_ref[...]).astype(bf16)
    row = lax.broadcasted_iota(jnp.int32, (GMLP_CHUNK, GMLP_CHUNK), 0)
    col = lax.broadcasted_iota(jnp.int32, (GMLP_CHUNK, GMLP_CHUNK), 1)
    ws = [jnp.where(col <= row, w_s_ref[g], jnp.zeros((), bf16)) for g in range(GMLP_GROUPS)]
    groups_per_tile = V7X_LANES // GMLP_GROUP_DIM
    assert groups_per_tile == 2
    chunks = []
    for c in range(tb // GMLP_CHUNK):
        tiles = []
        for tile in range(GMLP_WIDTH // V7X_LANES):
            rhs = vb[c * GMLP_CHUNK:(c + 1) * GMLP_CHUNK, tile * V7X_LANES:(tile + 1) * V7X_LANES]
            lo = _dot(ws[2 * tile], rhs)
            hi = _dot(ws[2 * tile + 1], rhs)
            tiles.append(jnp.where(col < GMLP_GROUP_DIM, lo, hi))
        chunks.append(jnp.concatenate(tiles, axis=1) + bs_ref[...])
    sv = jnp.concatenate(chunks, axis=0)

    gate_pre = [proj(COL_GATES + i * D_MODEL, D_MODEL) for i in range(N_BRANCHES)]

    cos_t = cos_ref[...]
    sin_t = sin_ref[...]
    lane = lax.broadcasted_iota(jnp.int32, (tb, HEAD_PAD), 1)
    swap0 = MLA_HEADS * HEAD_PAD
    for h in range(MLA_HEADS):
        plain = q2[:, h * HEAD_PAD:(h + 1) * HEAD_PAD]
        swapped = q2[:, swap0 + h * HEAD_PAD:swap0 + (h + 1) * HEAD_PAD]
        q_ref[h] = ((plain * cos_t + swapped * sin_t) * (SOFTMAX_SCALE * LOG2E)).astype(bf16)
    kr_swapped = jnp.where(lane < ROPE_LANE0 + HALF_ROPE,
                           pltpu.roll(kr, HEAD_PAD - HALF_ROPE, 1), pltpu.roll(kr, HALF_ROPE, 1))
    k_rope = kr * cos_t + kr_swapped * sin_t
    ones_lane = jnp.where(lane == ONES_LANE, 1.0, 0.0)
    for h in range(MLA_HEADS):
        k_ref[h] = (kv2[:, h * HEAD_PAD:(h + 1) * HEAD_PAD] + k_rope).astype(bf16)
        v_ref[h] = (kv2[:, swap0 + h * HEAD_PAD:swap0 + (h + 1) * HEAD_PAD] + ones_lane).astype(bf16)
    sb_ref[...] = _silu(b_gate).astype(sb_ref.dtype)

    y_a = jnp.concatenate(ys, axis=1) * pool_scale_ref[...] * _silu(a_gate)
    y_c = u * sv * _silu(c_gate)

    def gate(i):
        return _sigmoid(gate_pre[i] + b_gate_ref[:, i * D_MODEL:(i + 1) * D_MODEL])

    g1_ref[...] = gate(1).astype(g1_ref.dtype)
    part_ref[...] = (gate(0) * _dot(y_a.astype(bf16), w_oa_ref[...])
                     + gate(2) * _dot(y_c.astype(bf16), w_oc_ref[...])).astype(part_ref.dtype)


def _const_spec(stacked, l):
    index = (l,) + (0,) * (stacked.ndim - 1)
    return pl.BlockSpec((None,) + stacked.shape[1:], lambda b, j: index, pipeline_mode=pl.Buffered(1))


def _front(x, cos_t, sin_t, p, l):
    B, S, _ = x.shape
    tb = FRONT_BLOCK
    tok = lambda width: pl.BlockSpec((None, tb, width), lambda b, j: (b, j, 0))
    heads = pl.BlockSpec((None, MLA_HEADS, tb, HEAD_PAD), lambda b, j: (b, 0, j, 0))
    consts = [p["norm_pre"], p["w_in"], p["b_gate"], p["pool_w"], p["pool_scale"], p["q_norm"],
              p["w_uq"], p["kv_norm"], p["w_ukv"], p["v_norm_g"], p["v_norm_b"], p["w_s"],
              p["bs_full"], p["w_oa"], p["w_oc"]]
    head_shape = jax.ShapeDtypeStruct((B, MLA_HEADS, S, HEAD_PAD), bf16)
    return pl.pallas_call(
        _front_kernel,
        grid=(B, S // tb),
        in_specs=[tok(D_MODEL), tok(HEAD_PAD), tok(HEAD_PAD)] + [_const_spec(c, l) for c in consts],
        out_specs=[heads, heads, heads, tok(D_MODEL), tok(D_MODEL), tok(MLA_WIDTH)],
        out_shape=[head_shape, head_shape, head_shape,
                   jax.ShapeDtypeStruct((B, S, D_MODEL), bf16),
                   jax.ShapeDtypeStruct((B, S, D_MODEL), bf16),
                   jax.ShapeDtypeStruct((B, S, MLA_WIDTH), bf16)],
        scratch_shapes=[pltpu.VMEM((HALO + tb, POOL_WIDTH), f32)],
        compiler_params=pltpu.CompilerParams(
            dimension_semantics=("arbitrary", "arbitrary"),
            vmem_limit_bytes=V7X_VMEM_BYTES * 7 // 8),
        name="layer_front",
    )(x, cos_t, sin_t, *consts)


def _attn_kernel(q_ref, k_ref, v_ref, o_ref, s_ref, p_ref, m_ref, acc_ref):
    hp, tq, _ = q_ref.shape
    i = pl.program_id(2)
    lane = lax.broadcasted_iota(jnp.int32, (tq, HEAD_PAD), 1)
    n_tiles = tq // V7X_LANES
    half = tq // 2

    m_ref[...] = jnp.full(m_ref.shape, -jnp.inf, f32)
    acc_ref[...] = jnp.zeros(acc_ref.shape, f32)

    nt_dims = (((1,), (1,)), ((), ()))

    def scores(hh, j, slot, diagonal=False):
        start = pl.multiple_of(j * tq, tq)
        if diagonal:
            s_ref[slot, hh, 0:half, 0:half] = lax.dot_general(
                q_ref[hh, 0:half], k_ref[hh, pl.ds(start, half), :], nt_dims, preferred_element_type=f32)
            s_ref[slot, hh, half:tq, :] = lax.dot_general(
                q_ref[hh, half:tq], k_ref[hh, pl.ds(start, tq), :], nt_dims, preferred_element_type=f32)
        else:
            s_ref[slot, hh] = lax.dot_general(q_ref[hh], k_ref[hh, pl.ds(start, tq), :], nt_dims,
                                              preferred_element_type=f32)

    def softmax_pv(hh, j, slot, diagonal):
        for r in range(tq // SOFTMAX_ROWS):
            r0 = r * SOFTMAX_ROWS
            rows = slice(r0, r0 + SOFTMAX_ROWS)
            live = (r0 + SOFTMAX_ROWS - 1) // V7X_LANES + 1 if diagonal else n_tiles
            read = (half if r0 < half else tq) // V7X_LANES if diagonal else n_tiles

            def tile(c):
                s = s_ref[slot, hh, rows, c * V7X_LANES:(c + 1) * V7X_LANES]
                if diagonal and (c + 1) * V7X_LANES - 1 > r0:
                    row = r0 + lax.broadcasted_iota(jnp.int32, s.shape, 0)
                    col = c * V7X_LANES + lax.broadcasted_iota(jnp.int32, s.shape, 1)
                    s = jnp.where(col <= row, s, -jnp.inf)
                return s

            m_blk = tile(0)
            for c in range(1, live):
                m_blk = jnp.maximum(m_blk, tile(c))
            m_old = m_ref[hh, rows]
            m_new = jnp.maximum(m_old, jnp.max(m_blk, axis=-1, keepdims=True))
            m_ref[hh, rows] = m_new
            acc_ref[hh, rows] = jnp.exp2(m_old - m_new) * acc_ref[hh, rows]
            for c in range(live):
                p_ref[hh, rows, c * V7X_LANES:(c + 1) * V7X_LANES] = jnp.exp2(tile(c) - m_new).astype(bf16)
            if live < read:
                p_ref[hh, rows, live * V7X_LANES:read * V7X_LANES] = jnp.zeros(
                    (SOFTMAX_ROWS, (read - live) * V7X_LANES), bf16)
        start = pl.multiple_of(j * tq, tq)
        if diagonal:
            acc_ref[hh, 0:half] += _dot(p_ref[hh, 0:half, 0:half], v_ref[hh, pl.ds(start, half), :])
            acc_ref[hh, half:tq] += _dot(p_ref[hh, half:tq, :], v_ref[hh, pl.ds(start, tq), :])
        else:
            acc_ref[hh] += _dot(p_ref[hh], v_ref[hh, pl.ds(start, tq), :])

    def step(j, slot, next_diagonal=False):
        for hh in range(hp):
            softmax_pv(hh, j, slot, False)
            scores(hh, j + 1, 1 - slot, next_diagonal)

    def last_step(slot):
        for hh in range(hp):
            softmax_pv(hh, i, slot, True)

    @pl.when(i == 0)
    def _():
        for hh in range(hp):
            scores(hh, 0, 0, True)
        last_step(0)

    @pl.when(i > 0)
    def _():
        for hh in range(hp):
            scores(hh, 0, 0)

    def body(t, carry):
        step(2 * t, 0)
        step(2 * t + 1, 1)
        return carry

    lax.fori_loop(0, jnp.maximum(i - 1, 0) // 2, body, 0)

    @pl.when(i % 2 == 1)
    def _():
        step(i - 1, 0, True)
        last_step(1)

    @pl.when((i % 2 == 0) & (i > 0))
    def _():
        step(i - 2, 0)
        step(i - 1, 1, True)
        last_step(0)

    outs = [acc_ref[hh] / acc_ref[hh, :, ONES_LANE:ONES_LANE + 1] for hh in range(hp)]
    o_ref[...] = jnp.where(lane < V_HEAD_DIM, outs[0],
                           pltpu.roll(outs[1], V_HEAD_DIM, 1)).astype(o_ref.dtype)


def _attention(q, k, v):
    B, H, S, _ = q.shape
    tq = ATTN_BLOCK
    hp = HEADS_PER_STEP
    assert hp * V_HEAD_DIM == V7X_LANES
    return pl.pallas_call(
        _attn_kernel,
        grid=(B, H // hp, S // tq),
        in_specs=[pl.BlockSpec((None, hp, tq, HEAD_PAD), lambda b, g, i: (b, g, i, 0)),
                  pl.BlockSpec((None, hp, S, HEAD_PAD), lambda b, g, i: (b, g, 0, 0)),
                  pl.BlockSpec((None, hp, S, HEAD_PAD), lambda b, g, i: (b, g, 0, 0))],
        out_specs=pl.BlockSpec((None, tq, hp * V_HEAD_DIM), lambda b, g, i: (b, i, g)),
        out_shape=jax.ShapeDtypeStruct((B, S, H * V_HEAD_DIM), bf16),
        scratch_shapes=[pltpu.VMEM((2, hp, tq, tq), f32),
                        pltpu.VMEM((hp, tq, tq), bf16),
                        pltpu.VMEM((hp, tq, V7X_LANES), f32),
                        pltpu.VMEM((hp, tq, HEAD_PAD), f32)],
        compiler_params=pltpu.CompilerParams(
            dimension_semantics=("arbitrary", "arbitrary", "arbitrary"),
            vmem_limit_bytes=V7X_VMEM_BYTES * 7 // 8),
        name="mla_attention",
    )(q, k, v)


def _back_kernel(x_ref, o_ref, sb_ref, g1_ref, part_ref, w_ob_ref, w_out_ref, norm_post_ref,
                 out_ref):
    y_b = o_ref[...] * sb_ref[...]
    merged = part_ref[...].astype(f32) + g1_ref[...].astype(f32) * _dot(y_b, w_ob_ref[...])
    z = _dot(merged.astype(bf16), w_out_ref[...])
    out_ref[...] = x_ref[...] + _rms(z, norm_post_ref[...])


def _back(x, o, sb, g1, part, p, l):
    B, S, _ = x.shape
    tb = BACK_BLOCK
    tok = lambda width: pl.BlockSpec((None, tb, width), lambda b, j: (b, j, 0))
    consts = [p["w_ob"], p["w_out"], p["norm_post"]]
    return pl.pallas_call(
        _back_kernel,
        grid=(B, S // tb),
        in_specs=[tok(D_MODEL), tok(MLA_WIDTH), tok(MLA_WIDTH), tok(D_MODEL), tok(D_MODEL)]
        + [_const_spec(c, l) for c in consts],
        out_specs=tok(D_MODEL),
        out_shape=jax.ShapeDtypeStruct((B, S, D_MODEL), f32),
        compiler_params=pltpu.CompilerParams(
            dimension_semantics=("arbitrary", "arbitrary"),
            vmem_limit_bytes=V7X_VMEM_BYTES // 2),
        name="layer_back",
    )(x, o, sb, g1, part, *consts)


W_IN_ROWS = 256


def _w_in_layout_kernel(src_ref, out_ref):
    x = src_ref[...]
    kr0 = COL_KR
    kr1 = kr0 + QK_ROPE_DIM
    out_ref[:, 0:kr0] = x[:, 0:kr0].astype(bf16)
    out_ref[:, kr0:COL_B_GATE] = jnp.zeros((x.shape[0], HEAD_PAD), bf16)
    out_ref[:, kr0 + ROPE_LANE0:kr0 + ROPE_LANE0 + QK_ROPE_DIM] = x[:, kr0:kr1].astype(bf16)
    out_ref[:, COL_B_GATE:IN_WIDTH_PAD] = x[:, kr1:].astype(bf16)


def _w_in_layout(w_in):
    L, rows, width = w_in.shape
    assert width == sum(IN_SPLITS) and COL_KR == sum(IN_SPLITS[:4])
    return pl.pallas_call(
        _w_in_layout_kernel,
        grid=(L, rows // W_IN_ROWS),
        in_specs=[pl.BlockSpec((None, W_IN_ROWS, width), lambda l, i: (l, i, 0))],
        out_specs=pl.BlockSpec((None, W_IN_ROWS, IN_WIDTH_PAD), lambda l, i: (l, i, 0)),
        out_shape=jax.ShapeDtypeStruct((L, rows, IN_WIDTH_PAD), bf16),
        name="w_in_layout",
    )(w_in)


def _stacked_params(norm_pre, norm_post, w_in, b_gate, pool_w, pool_scale, q_norm, w_uq,
                    kv_norm, w_ukv, v_norm_g, v_norm_b, w_s, b_s, w_oa, w_ob, w_oc, w_out):
    L = w_in.shape[0]
    zeros = lambda *shape: jnp.zeros(shape, bf16)
    w_in_p = _w_in_layout(w_in)

    uq = w_uq.astype(bf16).reshape(L, Q_LORA_RANK, MLA_HEADS, QK_NOPE_DIM + QK_ROPE_DIM)
    nope, x1, x2 = uq[..., :QK_NOPE_DIM], uq[..., QK_NOPE_DIM:QK_NOPE_DIM + HALF_ROPE], uq[..., QK_NOPE_DIM + HALF_ROPE:]
    pad = zeros(L, Q_LORA_RANK, MLA_HEADS, HEAD_PAD - QK_NOPE_DIM - QK_ROPE_DIM)
    plain = jnp.concatenate([nope, x1, x2, pad], axis=-1).reshape(L, Q_LORA_RANK, MLA_HEADS * HEAD_PAD)
    swapped = jnp.concatenate([jnp.zeros_like(nope), x2, x1, pad], axis=-1).reshape(L, Q_LORA_RANK, MLA_HEADS * HEAD_PAD)
    w_uq_p = jnp.concatenate([plain, swapped], axis=-1)

    ukv = w_ukv.astype(bf16).reshape(L, KV_LORA_RANK, MLA_HEADS, QK_NOPE_DIM + V_HEAD_DIM)
    kpad = zeros(L, KV_LORA_RANK, MLA_HEADS, HEAD_PAD - QK_NOPE_DIM)
    vpad = zeros(L, KV_LORA_RANK, MLA_HEADS, HEAD_PAD - V_HEAD_DIM)
    k_cols = jnp.concatenate([ukv[..., :QK_NOPE_DIM], kpad], axis=-1).reshape(L, KV_LORA_RANK, MLA_HEADS * HEAD_PAD)
    v_cols = jnp.concatenate([ukv[..., QK_NOPE_DIM:], vpad], axis=-1).reshape(L, KV_LORA_RANK, MLA_HEADS * HEAD_PAD)
    w_ukv_p = jnp.concatenate([k_cols, v_cols], axis=-1)

    row = lambda a: a.reshape(L, 1, -1)
    return {
        "norm_pre": row(norm_pre), "norm_post": row(norm_post), "w_in": w_in_p,
        "b_gate": row(b_gate), "pool_w": pool_w.astype(bf16), "pool_scale": row(pool_scale),
        "q_norm": row(q_norm), "w_uq": w_uq_p, "kv_norm": row(kv_norm), "w_ukv": w_ukv_p,
        "v_norm_g": row(v_norm_g), "v_norm_b": row(v_norm_b), "w_s": w_s.astype(bf16),
        "bs_full": jnp.repeat(jnp.swapaxes(b_s, 1, 2), GMLP_GROUP_DIM, axis=2),
        "w_oa": w_oa.astype(bf16), "w_ob": w_ob.astype(bf16), "w_oc": w_oc.astype(bf16),
        "w_out": w_out.astype(bf16),
    }


def kernel(x, positions, norm_pre, norm_post, w_in, b_gate, pool_w, pool_scale, q_norm, w_uq, kv_norm, w_ukv, v_norm_g, v_norm_b, w_s, b_s, w_oa, w_ob, w_oc, w_out):
    B, S, _ = x.shape
    cos_t, sin_t = _rope_tables(positions)
    cos_t = cos_t.reshape(B, S, HEAD_PAD)
    sin_t = sin_t.reshape(B, S, HEAD_PAD)
    p = _stacked_params(norm_pre, norm_post, w_in, b_gate, pool_w, pool_scale, q_norm, w_uq,
                        kv_norm, w_ukv, v_norm_g, v_norm_b, w_s, b_s, w_oa, w_ob, w_oc, w_out)
    for l in range(norm_pre.shape[0]):
        q, k, v, part, g1, sb = _front(x, cos_t, sin_t, p, l)
        o = _attention(q, k, v)
        x = _back(x, o, sb, g1, part, p, l)
    return x
```

```python
import functools
import math

import jax
import jax.numpy as jnp
import numpy as np
from jax import lax
from jax.experimental import pallas as pl
from jax.experimental.pallas import tpu as pltpu

D_MODEL = 1024
POOL_WIDTH = 512
POOL_GROUPS = 4
POOL_GROUP_DIM = POOL_WIDTH // POOL_GROUPS
POOL_WINDOWS = (2, 4, 8, 16)
MLA_HEADS = 8
QK_NOPE_DIM = 64
QK_ROPE_DIM = 32
HALF_ROPE = QK_ROPE_DIM // 2
V_HEAD_DIM = 64
Q_LORA_RANK = 256
KV_LORA_RANK = 128
MLA_WIDTH = MLA_HEADS * V_HEAD_DIM
ROPE_THETA = 10000.0
GMLP_WIDTH = 512
GMLP_CHUNK = 128
GMLP_GROUPS = 8
GMLP_GROUP_DIM = GMLP_WIDTH // GMLP_GROUPS
N_BRANCHES = 3
EPS = 1e-6
IN_SPLITS = (POOL_WIDTH, POOL_WIDTH, Q_LORA_RANK, KV_LORA_RANK, QK_ROPE_DIM, MLA_WIDTH,
             GMLP_WIDTH, GMLP_WIDTH, GMLP_WIDTH, N_BRANCHES * D_MODEL)

V7X_LANES = 128
V7X_VMEM_BYTES = 64 * 1024 * 1024

HEAD_PAD = V7X_LANES
ROPE_LANE0 = QK_NOPE_DIM
ONES_LANE = V_HEAD_DIM
HALO = max(POOL_WINDOWS)

COL_A_IN = 0
COL_A_GATE = COL_A_IN + POOL_WIDTH
COL_CQ = COL_A_GATE + POOL_WIDTH
COL_CKV = COL_CQ + Q_LORA_RANK
COL_KR = COL_CKV + KV_LORA_RANK
COL_B_GATE = COL_KR + HEAD_PAD
COL_U = COL_B_GATE + MLA_WIDTH
COL_V = COL_U + GMLP_WIDTH
COL_C_GATE = COL_V + GMLP_WIDTH
COL_GATES = COL_C_GATE + GMLP_WIDTH
IN_WIDTH_PAD = COL_GATES + N_BRANCHES * D_MODEL

SOFTMAX_SCALE = (QK_NOPE_DIM + QK_ROPE_DIM) ** -0.5
LOG2E = math.log2(math.e)

FRONT_BLOCK = 512
ATTN_BLOCK = 1024
BACK_BLOCK = 512
HEADS_PER_STEP = 2
SOFTMAX_ROWS = 64

bf16 = jnp.bfloat16
f32 = jnp.float32


def _sigmoid(z):
    return 0.5 * jnp.tanh(0.5 * z) + 0.5


def _silu(z):
    return z * _sigmoid(z)


def _rms(x, g):
    return x * lax.rsqrt(jnp.mean(x * x, axis=-1, keepdims=True) + EPS) * g


def _dot(a, b):
    return jnp.dot(a, b, preferred_element_type=f32)


def _rope_table_kernel(pos_ref, invf_ref, cos_ref, sin_ref):
    ang = pos_ref[...].astype(f32) * invf_ref[...]
    cos_ref[...] = jnp.cos(ang)
    sin_ref[...] = jnp.sin(ang)


def _rope_tables(positions):
    n = positions.size
    per_row = V7X_LANES // HALF_ROPE
    inv_freq = ROPE_THETA ** (-jnp.arange(0, QK_ROPE_DIM, 2, dtype=f32) / QK_ROPE_DIM)
    invf = jnp.tile(inv_freq, per_row).reshape(1, V7X_LANES)
    pos = jnp.repeat(positions.reshape(n // per_row, per_row), HALF_ROPE, axis=1)
    dense = jax.ShapeDtypeStruct((n // per_row, V7X_LANES), f32)
    cos_d, sin_d = pl.pallas_call(
        _rope_table_kernel,
        out_shape=[dense, dense],
        name="rope_tables",
    )(pos, invf)
    c = cos_d.reshape(n, HALF_ROPE)
    s = sin_d.reshape(n, HALF_ROPE)
    pad = jnp.zeros((n, HEAD_PAD - ROPE_LANE0 - QK_ROPE_DIM), f32)
    cos_t = jnp.concatenate([jnp.ones((n, ROPE_LANE0), f32), c, c, pad], axis=1)
    sin_t = jnp.concatenate([jnp.zeros((n, ROPE_LANE0), f32), -s, s, pad], axis=1)
    return cos_t, sin_t


def _front_kernel(x_ref, cos_ref, sin_ref, norm_pre_ref, w_in_ref, b_gate_ref, pool_w_ref,
                  pool_scale_ref, q_norm_ref, w_uq_ref, kv_norm_ref, w_ukv_ref, vn_g_ref,
                  vn_b_ref, w_s_ref, bs_ref, w_oa_ref, w_oc_ref,
                  q_ref, k_ref, v_ref, part_ref, g1_ref, sb_ref, abuf):
    tb = x_ref.shape[0]
    j = pl.program_id(1)

    hb = _rms(x_ref[...], norm_pre_ref[...]).astype(bf16)

    def proj(c0, width):
        return _dot(hb, w_in_ref[:, c0:c0 + width])

    @pl.when(j == 0)
    def _():
        abuf[0:HALO, :] = jnp.zeros((HALO, POOL_WIDTH), f32)

    a_in = proj(COL_A_IN, POOL_WIDTH)
    abuf[HALO:HALO + tb, :] = a_in
    cq = proj(COL_CQ, Q_LORA_RANK)
    ckv = proj(COL_CKV, KV_LORA_RANK)
    kr = proj(COL_KR, HEAD_PAD)
    vv = proj(COL_V, GMLP_WIDTH)
    a_gate = proj(COL_A_GATE, POOL_WIDTH)
    b_gate = proj(COL_B_GATE, MLA_WIDTH)
    u = proj(COL_U, GMLP_WIDTH)
    c_gate = proj(COL_C_GATE, GMLP_WIDTH)

    t = j * tb + lax.broadcasted_iota(jnp.int32, (tb, 1), 0)
    ys = []
    for g, w in enumerate(POOL_WINDOWS):
        c0 = g * POOL_GROUP_DIM
        win = abuf[HALO:HALO + tb, c0:c0 + POOL_GROUP_DIM]
        for i in range(1, w):
            win = win + abuf[HALO - i:HALO - i + tb, c0:c0 + POOL_GROUP_DIM]
        count = jnp.minimum(t + 1, w).astype(f32)
        mixed = win / count - a_in[:, c0:c0 + POOL_GROUP_DIM]
        ys.append(_dot(mixed.astype(bf16), pool_w_ref[g]))
    abuf[0:HALO, :] = abuf[tb:tb + HALO, :]

    cqn = _rms(cq, q_norm_ref[...]).astype(bf16)
    q2 = _dot(cqn, w_uq_ref[...])
    ckvn = _rms(ckv, kv_norm_ref[...]).astype(bf16)
    kv2 = _dot(ckvn, w_ukv_ref[...])

    vc = vv - jnp.mean(vv, axis=-1, keepdims=True)
    var = jnp.mean(vc * vc, axis=-1, keepdims=True)
    vb = (vc * lax.rsqrt(var + EPS) * vn_g_ref[...] + vn_b_ref[...]).astype(bf16)
    row = lax.broadcasted_iota(jnp.int32, (GMLP_CHUNK, GMLP_CHUNK), 0)
    col = lax.broadcasted_iota(jnp.int32, (GMLP_CHUNK, GMLP_CHUNK), 1)
    ws = [jnp.where(col <= row, w_s_ref[g], jnp.zeros((), bf16)) for g in range(GMLP_GROUPS)]
    groups_per_tile = V7X_LANES // GMLP_GROUP_DIM
    assert groups_per_tile == 2
    chunks = []
    for c in range(tb // GMLP_CHUNK):
        tiles = []
        for tile in range(GMLP_WIDTH // V7X_LANES):
            rhs = vb[c * GMLP_CHUNK:(c + 1) * GMLP_CHUNK, tile * V7X_LANES:(tile + 1) * V7X_LANES]
            lo = _dot(ws[2 * tile], rhs)
            hi = _dot(ws[2 * tile + 1], rhs)
            tiles.append(jnp.where(col < GMLP_GROUP_DIM, lo, hi))
        chunks.append(jnp.concatenate(tiles, axis=1) + bs_ref[...])
    sv = jnp.concatenate(chunks, axis=0)

    gate_pre = [proj(COL_GATES + i * D_MODEL, D_MODEL) for i in range(N_BRANCHES)]

    cos_t = cos_ref[...]
    sin_t = sin_ref[...]
    lane = lax.broadcasted_iota(jnp.int32, (tb, HEAD_PAD), 1)
    swap0 = MLA_HEADS * HEAD_PAD
    for h in range(MLA_HEADS):
        plain = q2[:, h * HEAD_PAD:(h + 1) * HEAD_PAD]
        swapped = q2[:, swap0 + h * HEAD_PAD:swap0 + (h + 1) * HEAD_PAD]
        q_ref[h] = ((plain * cos_t + swapped * sin_t) * (SOFTMAX_SCALE * LOG2E)).astype(bf16)
    kr_swapped = jnp.where(lane < ROPE_LANE0 + HALF_ROPE,
                           pltpu.roll(kr, HEAD_PAD - HALF_ROPE, 1), pltpu.roll(kr, HALF_ROPE, 1))
    k_rope = kr * cos_t + kr_swapped * sin_t
    ones_lane = jnp.where(lane == ONES_LANE, 1.0, 0.0)
    for h in range(MLA_HEADS):
        k_ref[h] = (kv2[:, h * HEAD_PAD:(h + 1) * HEAD_PAD] + k_rope).astype(bf16)
        v_ref[h] = (kv2[:, swap0 + h * HEAD_PAD:swap0 + (h + 1) * HEAD_PAD] + ones_lane).astype(bf16)
    sb_ref[...] = _silu(b_gate).astype(sb_ref.dtype)

    y_a = jnp.concatenate(ys, axis=1) * pool_scale_ref[...] * _silu(a_gate)
    y_c = u * sv * _silu(c_gate)

    def gate(i):
        return _sigmoid(gate_pre[i] + b_gate_ref[:, i * D_MODEL:(i + 1) * D_MODEL])

    g1_ref[...] = gate(1).astype(g1_ref.dtype)
    part_ref[...] = (gate(0) * _dot(y_a.astype(bf16), w_oa_ref[...])
                     + gate(2) * _dot(y_c.astype(bf16), w_oc_ref[...])).astype(part_ref.dtype)


def _const_spec(stacked, l):
    index = (l,) + (0,) * (stacked.ndim - 1)
    return pl.BlockSpec((None,) + stacked.shape[1:], lambda b, j: index, pipeline_mode=pl.Buffered(1))


def _front(x, cos_t, sin_t, p, l):
    B, S, _ = x.shape
    tb = FRONT_BLOCK
    tok = lambda width: pl.BlockSpec((None, tb, width), lambda b, j: (b, j, 0))
    heads = pl.BlockSpec((None, MLA_HEADS, tb, HEAD_PAD), lambda b, j: (b, 0, j, 0))
    consts = [p["norm_pre"], p["w_in"], p["b_gate"], p["pool_w"], p["pool_scale"], p["q_norm"],
              p["w_uq"], p["kv_norm"], p["w_ukv"], p["v_norm_g"], p["v_norm_b"], p["w_s"],
              p["bs_full"], p["w_oa"], p["w_oc"]]
    head_shape = jax.ShapeDtypeStruct((B, MLA_HEADS, S, HEAD_PAD), bf16)
    return pl.pallas_call(
        _front_kernel,
        grid=(B, S // tb),
        in_specs=[tok(D_MODEL), tok(HEAD_PAD), tok(HEAD_PAD)] + [_const_spec(c, l) for c in consts],
        out_specs=[heads, heads, heads, tok(D_MODEL), tok(D_MODEL), tok(MLA_WIDTH)],
        out_shape=[head_shape, head_shape, head_shape,
                   jax.ShapeDtypeStruct((B, S, D_MODEL), bf16),
                   jax.ShapeDtypeStruct((B, S, D_MODEL), bf16),
                   jax.ShapeDtypeStruct((B, S, MLA_WIDTH), bf16)],
        scratch_shapes=[pltpu.VMEM((HALO + tb, POOL_WIDTH), f32)],
        compiler_params=pltpu.CompilerParams(
            dimension_semantics=("arbitrary", "arbitrary"),
            vmem_limit_bytes=V7X_VMEM_BYTES * 7 // 8),
        name="layer_front",
    )(x, cos_t, sin_t, *consts)


def _attn_kernel(q_ref, k_ref, v_ref, o_ref, s_ref, p_ref, m_ref, acc_ref):
    hp, tq, _ = q_ref.shape
    i = pl.program_id(2)
    lane = lax.broadcasted_iota(jnp.int32, (tq, HEAD_PAD), 1)
    n_tiles = tq // V7X_LANES
    half = tq // 2

    m_ref[...] = jnp.full(m_ref.shape, -jnp.inf, f32)
    acc_ref[...] = jnp.zeros(acc_ref.shape, f32)

    nt_dims = (((1,), (1,)), ((), ()))

    def scores(hh, j, slot, diagonal=False):
        start = pl.multiple_of(j * tq, tq)
        if diagonal:
            s_ref[slot, hh, 0:half, 0:half] = lax.dot_general(
                q_ref[hh, 0:half], k_ref[hh, pl.ds(start, half), :], nt_dims, preferred_element_type=f32)
            s_ref[slot, hh, half:tq, :] = lax.dot_general(
                q_ref[hh, half:tq], k_ref[hh, pl.ds(start, tq), :], nt_dims, preferred_element_type=f32)
        else:
            s_ref[slot, hh] = lax.dot_general(q_ref[hh], k_ref[hh, pl.ds(start, tq), :], nt_dims,
                                              preferred_element_type=f32)

    def softmax_pv(hh, j, slot, diagonal):
        for r in range(tq // SOFTMAX_ROWS):
            r0 = r * SOFTMAX_ROWS
            rows = slice(r0, r0 + SOFTMAX_ROWS)
            live = (r0 + SOFTMAX_ROWS - 1) // V7X_LANES + 1 if diagonal else n_tiles
            read = (half if r0 < half else tq) // V7X_LANES if diagonal else n_tiles

            def tile(c):
                s = s_ref[slot, hh, rows, c * V7X_LANES:(c + 1) * V7X_LANES]
                if diagonal and (c + 1) * V7X_LANES - 1 > r0:
                    row = r0 + lax.broadcasted_iota(jnp.int32, s.shape, 0)
                    col = c * V7X_LANES + lax.broadcasted_iota(jnp.int32, s.shape, 1)
                    s = jnp.where(col <= row, s, -jnp.inf)
                return s

            m_blk = tile(0)
            for c in range(1, live):
                m_blk = jnp.maximum(m_blk, tile(c))
            m_old = m_ref[hh, rows]
            m_new = jnp.maximum(m_old, jnp.max(m_blk, axis=-1, keepdims=True))
            m_ref[hh, rows] = m_new
            acc_ref[hh, rows] = jnp.exp2(m_old - m_new) * acc_ref[hh, rows]
            for c in range(live):
                p_ref[hh, rows, c * V7X_LANES:(c + 1) * V7X_LANES] = jnp.exp2(tile(c) - m_new).astype(bf16)
            if live < read:
                p_ref[hh, rows, live * V7X_LANES:read * V7X_LANES] = jnp.zeros(
                    (SOFTMAX_ROWS, (read - live) * V7X_LANES), bf16)
        start = pl.multiple_of(j * tq, tq)
        if diagonal:
            acc_ref[hh, 0:half] += _dot(p_ref[hh, 0:half, 0:half], v_ref[hh, pl.ds(start, half), :])
            acc_ref[hh, half:tq] += _dot(p_ref[hh, half:tq, :], v_ref[hh, pl.ds(start, tq), :])
        else:
            acc_ref[hh] += _dot(p_ref[hh], v_ref[hh, pl.ds(start, tq), :])

    def step(j, slot, next_diagonal=False):
        for hh in range(hp):
            softmax_pv(hh, j, slot, False)
            scores(hh, j + 1, 1 - slot, next_diagonal)

    def last_step(slot):
        for hh in range(hp):
            softmax_pv(hh, i, slot, True)

    @pl.when(i == 0)
    def _():
        for hh in range(hp):
            scores(hh, 0, 0, True)
        last_step(0)

    @pl.when(i > 0)
    def _():
        for hh in range(hp):
            scores(hh, 0, 0)

    def body(t, carry):
        step(2 * t, 0)
        step(2 * t + 1, 1)
        return carry

    lax.fori_loop(0, jnp.maximum(i - 1, 0) // 2, body, 0)

    @pl.when(i % 2 == 1)
    def _():
        step(i - 1, 0, True)
        last_step(1)

    @pl.when((i % 2 == 0) & (i > 0))
    def _():
        step(i - 2, 0)
        step(i - 1, 1, True)
        last_step(0)

    outs = [acc_ref[hh] / acc_ref[hh, :, ONES_LANE:ONES_LANE + 1] for hh in range(hp)]
    o_ref[...] = jnp.where(lane < V_HEAD_DIM, outs[0],
                           pltpu.roll(outs[1], V_HEAD_DIM, 1)).astype(o_ref.dtype)


def _attention(q, k, v):
    B, H, S, _ = q.shape
    tq = ATTN_BLOCK
    hp = HEADS_PER_STEP
    assert hp * V_HEAD_DIM == V7X_LANES
    return pl.pallas_call(
        _attn_kernel,
        grid=(B, H // hp, S // tq),
        in_specs=[pl.BlockSpec((None, hp, tq, HEAD_PAD), lambda b, g, i: (b, g, i, 0)),
                  pl.BlockSpec((None, hp, S, HEAD_PAD), lambda b, g, i: (b, g, 0, 0)),
                  pl.BlockSpec((None, hp, S, HEAD_PAD), lambda b, g, i: (b, g, 0, 0))],
        out_specs=pl.BlockSpec((None, tq, hp * V_HEAD_DIM), lambda b, g, i: (b, i, g)),
        out_shape=jax.ShapeDtypeStruct((B, S, H * V_HEAD_DIM), bf16),
        scratch_shapes=[pltpu.VMEM((2, hp, tq, tq), f32),
                        pltpu.VMEM((hp, tq, tq), bf16),
                        pltpu.VMEM((hp, tq, V7X_LANES), f32),
                        pltpu.VMEM((hp, tq, HEAD_PAD), f32)],
        compiler_params=pltpu.CompilerParams(
            dimension_semantics=("arbitrary", "arbitrary", "arbitrary"),
            vmem_limit_bytes=V7X_VMEM_BYTES * 7 // 8),
        name="mla_attention",
    )(q, k, v)


def _back_kernel(x_ref, o_ref, sb_ref, g1_ref, part_ref, w_ob_ref, w_out_ref, norm_post_ref,
                 out_ref):
    y_b = o_ref[...] * sb_ref[...]
    merged = part_ref[...].astype(f32) + g1_ref[...].astype(f32) * _dot(y_b, w_ob_ref[...])
    z = _dot(merged.astype(bf16), w_out_ref[...])
    out_ref[...] = x_ref[...] + _rms(z, norm_post_ref[...])


def _back(x, o, sb, g1, part, p, l):
    B, S, _ = x.shape
    tb = BACK_BLOCK
    tok = lambda width: pl.BlockSpec((None, tb, width), lambda b, j: (b, j, 0))
    consts = [p["w_ob"], p["w_out"], p["norm_post"]]
    return pl.pallas_call(
        _back_kernel,
        grid=(B, S // tb),
        in_specs=[tok(D_MODEL), tok(MLA_WIDTH), tok(MLA_WIDTH), tok(D_MODEL), tok(D_MODEL)]
        + [_const_spec(c, l) for c in consts],
        out_specs=tok(D_MODEL),
        out_shape=jax.ShapeDtypeStruct((B, S, D_MODEL), f32),
        compiler_params=pltpu.CompilerParams(
            dimension_semantics=("arbitrary", "arbitrary"),
            vmem_limit_bytes=V7X_VMEM_BYTES // 2),
        name="layer_back",
    )(x, o, sb, g1, part, *consts)


W_IN_ROWS = 256


def _w_in_layout_kernel(src_ref, out_ref):
    x = src_ref[...]
    kr0 = COL_KR
    kr1 = kr0 + QK_ROPE_DIM
    out_ref[:, 0:kr0] = x[:, 0:kr0].astype(bf16)
    out_ref[:, kr0:COL_B_GATE] = jnp.zeros((x.shape[0], HEAD_PAD), bf16)
    out_ref[:, kr0 + ROPE_LANE0:kr0 + ROPE_LANE0 + QK_ROPE_DIM] = x[:, kr0:kr1].astype(bf16)
    out_ref[:, COL_B_GATE:IN_WIDTH_PAD] = x[:, kr1:].astype(bf16)


def _w_in_layout(w_in):
    L, rows, width = w_in.shape
    assert width == sum(IN_SPLITS) and COL_KR == sum(IN_SPLITS[:4])
    return pl.pallas_call(
        _w_in_layout_kernel,
        grid=(L, rows // W_IN_ROWS),
        in_specs=[pl.BlockSpec((None, W_IN_ROWS, width), lambda l, i: (l, i, 0))],
        out_specs=pl.BlockSpec((None, W_IN_ROWS, IN_WIDTH_PAD), lambda l, i: (l, i, 0)),
        out_shape=jax.ShapeDtypeStruct((L, rows, IN_WIDTH_PAD), bf16),
        name="w_in_layout",
    )(w_in)


def _stacked_params(norm_pre, norm_post, w_in, b_gate, pool_w, pool_scale, q_norm, w_uq,
                    kv_norm, w_ukv, v_norm_g, v_norm_b, w_s, b_s, w_oa, w_ob, w_oc, w_out):
    L = w_in.shape[0]
    zeros = lambda *shape: jnp.zeros(shape, bf16)
    w_in_p = _w_in_layout(w_in.astype(bf16))

    uq = w_uq.astype(bf16).reshape(L, Q_LORA_RANK, MLA_HEADS, QK_NOPE_DIM + QK_ROPE_DIM)
    nope, x1, x2 = uq[..., :QK_NOPE_DIM], uq[..., QK_NOPE_DIM:QK_NOPE_DIM + HALF_ROPE], uq[..., QK_NOPE_DIM + HALF_ROPE:]
    pad = zeros(L, Q_LORA_RANK, MLA_HEADS, HEAD_PAD - QK_NOPE_DIM - QK_ROPE_DIM)
    plain = jnp.concatenate([nope, x1, x2, pad], axis=-1).reshape(L, Q_LORA_RANK, MLA_HEADS * HEAD_PAD)
    swapped = jnp.concatenate([jnp.zeros_like(nope), x2, x1, pad], axis=-1).reshape(L, Q_LORA_RANK, MLA_HEADS * HEAD_PAD)
    w_uq_p = jnp.concatenate([plain, swapped], axis=-1)

    ukv = w_ukv.astype(bf16).reshape(L, KV_LORA_RANK, MLA_HEADS, QK_NOPE_DIM + V_HEAD_DIM)
    kpad = zeros(L, KV_LORA_RANK, MLA_HEADS, HEAD_PAD - QK_NOPE_DIM)
    vpad = zeros(L, KV_LORA_RANK, MLA_HEADS, HEAD_PAD - V_HEAD_DIM)
    k_cols = jnp.concatenate([ukv[..., :QK_NOPE_DIM], kpad], axis=-1).reshape(L, KV_LORA_RANK, MLA_HEADS * HEAD_PAD)
    v_cols = jnp.concatenate([ukv[..., QK_NOPE_DIM:], vpad], axis=-1).reshape(L, KV_LORA_RANK, MLA_HEADS * HEAD_PAD)
    w_ukv_p = jnp.concatenate([k_cols, v_cols], axis=-1)

    row = lambda a: a.reshape(L, 1, -1)
    return {
        "norm_pre": row(norm_pre), "norm_post": row(norm_post), "w_in": w_in_p,
        "b_gate": row(b_gate), "pool_w": pool_w.astype(bf16), "pool_scale": row(pool_scale),
        "q_norm": row(q_norm), "w_uq": w_uq_p, "kv_norm": row(kv_norm), "w_ukv": w_ukv_p,
        "v_norm_g": row(v_norm_g), "v_norm_b": row(v_norm_b), "w_s": w_s.astype(bf16),
        "bs_full": jnp.repeat(jnp.swapaxes(b_s, 1, 2), GMLP_GROUP_DIM, axis=2),
        "w_oa": w_oa.astype(bf16), "w_ob": w_ob.astype(bf16), "w_oc": w_oc.astype(bf16),
        "w_out": w_out.astype(bf16),
    }


def kernel(x, positions, norm_pre, norm_post, w_in, b_gate, pool_w, pool_scale, q_norm, w_uq, kv_norm, w_ukv, v_norm_g, v_norm_b, w_s, b_s, w_oa, w_ob, w_oc, w_out):
    B, S, _ = x.shape
    cos_t, sin_t = _rope_tables(positions)
    cos_t = cos_t.reshape(B, S, HEAD_PAD)
    sin_t = sin_t.reshape(B, S, HEAD_PAD)
    p = _stacked_params(norm_pre, norm_post, w_in, b_gate, pool_w, pool_scale, q_norm, w_uq,
                        kv_norm, w_ukv, v_norm_g, v_norm_b, w_s, b_s, w_oa, w_ob, w_oc, w_out)
    for l in range(norm_pre.shape[0]):
        q, k, v, part, g1, sb = _front(x, cos_t, sin_t, p, l)
        o = _attention(q, k, v)
        x = _back(x, o, sb, g1, part, p, l)
    return x
```

```python
import functools
import math

import jax
import jax.numpy as jnp
import numpy as np
from jax import lax
from jax.experimental import pallas as pl
from jax.experimental.pallas import tpu as pltpu

D_MODEL = 1024
POOL_WIDTH = 512
POOL_GROUPS = 4
POOL_GROUP_DIM = POOL_WIDTH // POOL_GROUPS
POOL_WINDOWS = (2, 4, 8, 16)
MLA_HEADS = 8
QK_NOPE_DIM = 64
QK_ROPE_DIM = 32
HALF_ROPE = QK_ROPE_DIM // 2
V_HEAD_DIM = 64
Q_LORA_RANK = 256
KV_LORA_RANK = 128
MLA_WIDTH = MLA_HEADS * V_HEAD_DIM
ROPE_THETA = 10000.0
GMLP_WIDTH = 512
GMLP_CHUNK = 128
GMLP_GROUPS = 8
GMLP_GROUP_DIM = GMLP_WIDTH // GMLP_GROUPS
N_BRANCHES = 3
EPS = 1e-6
IN_SPLITS = (POOL_WIDTH, POOL_WIDTH, Q_LORA_RANK, KV_LORA_RANK, QK_ROPE_DIM, MLA_WIDTH,
             GMLP_WIDTH, GMLP_WIDTH, GMLP_WIDTH, N_BRANCHES * D_MODEL)

V7X_LANES = 128
V7X_VMEM_BYTES = 64 * 1024 * 1024

HEAD_PAD = V7X_LANES
ROPE_LANE0 = QK_NOPE_DIM
ONES_LANE = V_HEAD_DIM
HALO = max(POOL_WINDOWS)

COL_A_IN = 0
COL_A_GATE = COL_A_IN + POOL_WIDTH
COL_CQ = COL_A_GATE + POOL_WIDTH
COL_CKV = COL_CQ + Q_LORA_RANK
COL_KR = COL_CKV + KV_LORA_RANK
COL_B_GATE = COL_KR + HEAD_PAD
COL_U = COL_B_GATE + MLA_WIDTH
COL_V = COL_U + GMLP_WIDTH
COL_C_GATE = COL_V + GMLP_WIDTH
COL_GATES = COL_C_GATE + GMLP_WIDTH
IN_WIDTH_PAD = COL_GATES + N_BRANCHES * D_MODEL

SOFTMAX_SCALE = (QK_NOPE_DIM + QK_ROPE_DIM) ** -0.5
LOG2E = math.log2(math.e)

FRONT_BLOCK = 512
ATTN_BLOCK = 1024
BACK_BLOCK = 512
HEADS_PER_STEP = 2
SOFTMAX_ROWS = 64

bf16 = jnp.bfloat16
f32 = jnp.float32


def _sigmoid(z):
    return 0.5 * jnp.tanh(0.5 * z) + 0.5


def _silu(z):
    return z * _sigmoid(z)


def _rms(x, g):
    return x * lax.rsqrt(jnp.mean(x * x, axis=-1, keepdims=True) + EPS) * g


def _dot(a, b):
    return jnp.dot(a, b, preferred_element_type=f32)


def _rope_table_kernel(pos_ref, invf_ref, cos_ref, sin_ref):
    ang = pos_ref[...].astype(f32) * invf_ref[...]
    cos_ref[...] = jnp.cos(ang)
    sin_ref[...] = jnp.sin(ang)


def _rope_tables(positions):
    n = positions.size
    per_row = V7X_LANES // HALF_ROPE
    inv_freq = ROPE_THETA ** (-jnp.arange(0, QK_ROPE_DIM, 2, dtype=f32) / QK_ROPE_DIM)
    invf = jnp.tile(inv_freq, per_row).reshape(1, V7X_LANES)
    pos = jnp.repeat(positions.reshape(n // per_row, per_row), HALF_ROPE, axis=1)
    dense = jax.ShapeDtypeStruct((n // per_row, V7X_LANES), f32)
    cos_d, sin_d = pl.pallas_call(
        _rope_table_kernel,
        out_shape=[dense, dense],
        name="rope_tables",
    )(pos, invf)
    c = cos_d.reshape(n, HALF_ROPE)
    s = sin_d.reshape(n, HALF_ROPE)
    pad = jnp.zeros((n, HEAD_PAD - ROPE_LANE0 - QK_ROPE_DIM), f32)
    cos_t = jnp.concatenate([jnp.ones((n, ROPE_LANE0), f32), c, c, pad], axis=1)
    sin_t = jnp.concatenate([jnp.zeros((n, ROPE_LANE0), f32), -s, s, pad], axis=1)
    return cos_t, sin_t


def _front_kernel(x_ref, cos_ref, sin_ref, norm_pre_ref, w_head_ref, w_kr_ref, w_tail_ref, b_gate_ref, pool_w_ref,
                  pool_scale_ref, q_norm_ref, w_uq_ref, kv_norm_ref, w_ukv_ref, vn_g_ref,
                  vn_b_ref, w_s_ref, bs_ref, w_oa_ref, w_oc_ref,
                  q_ref, k_ref, v_ref, part_ref, g1_ref, sb_ref, abuf):
    tb = x_ref.shape[0]
    j = pl.program_id(1)

    hb = _rms(x_ref[...], norm_pre_ref[...]).astype(bf16)

    def proj(c0, width):
        if c0 + width <= COL_KR:
            w = w_head_ref[:, c0:c0 + width]
        elif c0 == COL_KR and width == HEAD_PAD:
            w = w_kr_ref[...]
        else:
            assert c0 >= COL_B_GATE
            w = w_tail_ref[:, c0 - COL_B_GATE:c0 - COL_B_GATE + width]
        return _dot(hb, w)

    @pl.when(j == 0)
    def _():
        abuf[0:HALO, :] = jnp.zeros((HALO, POOL_WIDTH), f32)

    a_in = proj(COL_A_IN, POOL_WIDTH)
    abuf[HALO:HALO + tb, :] = a_in
    cq = proj(COL_CQ, Q_LORA_RANK)
    ckv = proj(COL_CKV, KV_LORA_RANK)
    kr = proj(COL_KR, HEAD_PAD)
    vv = proj(COL_V, GMLP_WIDTH)
    a_gate = proj(COL_A_GATE, POOL_WIDTH)
    b_gate = proj(COL_B_GATE, MLA_WIDTH)
    u = proj(COL_U, GMLP_WIDTH)
    c_gate = proj(COL_C_GATE, GMLP_WIDTH)

    t = j * tb + lax.broadcasted_iota(jnp.int32, (tb, 1), 0)
    ys = []
    for g, w in enumerate(POOL_WINDOWS):
        c0 = g * POOL_GROUP_DIM
        win = abuf[HALO:HALO + tb, c0:c0 + POOL_GROUP_DIM]
        for i in range(1, w):
            win = win + abuf[HALO - i:HALO - i + tb, c0:c0 + POOL_GROUP_DIM]
        count = jnp.minimum(t + 1, w).astype(f32)
        mixed = win / count - a_in[:, c0:c0 + POOL_GROUP_DIM]
        ys.append(_dot(mixed.astype(bf16), pool_w_ref[g]))
    abuf[0:HALO, :] = abuf[tb:tb + HALO, :]

    cqn = _rms(cq, q_norm_ref[...]).astype(bf16)
    q2 = _dot(cqn, w_uq_ref[...])
    ckvn = _rms(ckv, kv_norm_ref[...]).astype(bf16)
    kv2 = _dot(ckvn, w_ukv_ref[...])

    vc = vv - jnp.mean(vv, axis=-1, keepdims=True)
    var = jnp.mean(vc * vc, axis=-1, keepdims=True)
    vb = (vc * lax.rsqrt(var + EPS) * vn_g_ref[...] + vn_b_ref[...]).astype(bf16)
    row = lax.broadcasted_iota(jnp.int32, (GMLP_CHUNK, GMLP_CHUNK), 0)
    col = lax.broadcasted_iota(jnp.int32, (GMLP_CHUNK, GMLP_CHUNK), 1)
    ws = [jnp.where(col <= row, w_s_ref[g], jnp.zeros((), bf16)) for g in range(GMLP_GROUPS)]
    groups_per_tile = V7X_LANES // GMLP_GROUP_DIM
    assert groups_per_tile == 2
    chunks = []
    for c in range(tb // GMLP_CHUNK):
        tiles = []
        for tile in range(GMLP_WIDTH // V7X_LANES):
            rhs = vb[c * GMLP_CHUNK:(c + 1) * GMLP_CHUNK, tile * V7X_LANES:(tile + 1) * V7X_LANES]
            lo = _dot(ws[2 * tile], rhs)
            hi = _dot(ws[2 * tile + 1], rhs)
            tiles.append(jnp.where(col < GMLP_GROUP_DIM, lo, hi))
        chunks.append(jnp.concatenate(tiles, axis=1) + bs_ref[...])
    sv = jnp.concatenate(chunks, axis=0)

    gate_pre = [proj(COL_GATES + i * D_MODEL, D_MODEL) for i in range(N_BRANCHES)]

    cos_t = cos_ref[...]
    sin_t = sin_ref[...]
    lane = lax.broadcasted_iota(jnp.int32, (tb, HEAD_PAD), 1)
    swap0 = MLA_HEADS * HEAD_PAD
    for h in range(MLA_HEADS):
        plain = q2[:, h * HEAD_PAD:(h + 1) * HEAD_PAD]
        swapped = q2[:, swap0 + h * HEAD_PAD:swap0 + (h + 1) * HEAD_PAD]
        q_ref[h] = ((plain * cos_t + swapped * sin_t) * (SOFTMAX_SCALE * LOG2E)).astype(bf16)
    kr_swapped = jnp.where(lane < ROPE_LANE0 + HALF_ROPE,
                           pltpu.roll(kr, HEAD_PAD - HALF_ROPE, 1), pltpu.roll(kr, HALF_ROPE, 1))
    k_rope = kr * cos_t + kr_swapped * sin_t
    ones_lane = jnp.where(lane == ONES_LANE, 1.0, 0.0)
    for h in range(MLA_HEADS):
        k_ref[h] = (kv2[:, h * HEAD_PAD:(h + 1) * HEAD_PAD] + k_rope).astype(bf16)
        v_ref[h] = (kv2[:, swap0 + h * HEAD_PAD:swap0 + (h + 1) * HEAD_PAD] + ones_lane).astype(bf16)
    sb_ref[...] = _silu(b_gate).astype(sb_ref.dtype)

    y_a = jnp.concatenate(ys, axis=1) * pool_scale_ref[...] * _silu(a_gate)
    y_c = u * sv * _silu(c_gate)

    def gate(i):
        return _sigmoid(gate_pre[i] + b_gate_ref[:, i * D_MODEL:(i + 1) * D_MODEL])

    g1_ref[...] = gate(1).astype(g1_ref.dtype)
    part_ref[...] = (gate(0) * _dot(y_a.astype(bf16), w_oa_ref[...])
                     + gate(2) * _dot(y_c.astype(bf16), w_oc_ref[...])).astype(part_ref.dtype)


def _const_spec(stacked, l):
    index = (l,) + (0,) * (stacked.ndim - 1)
    return pl.BlockSpec((None,) + stacked.shape[1:], lambda b, j: index, pipeline_mode=pl.Buffered(1))


def _front(x, cos_t, sin_t, p, l):
    B, S, _ = x.shape
    tb = FRONT_BLOCK
    tok = lambda width: pl.BlockSpec((None, tb, width), lambda b, j: (b, j, 0))
    heads = pl.BlockSpec((None, MLA_HEADS, tb, HEAD_PAD), lambda b, j: (b, 0, j, 0))
    consts = [p["norm_pre"], p["w_head"], p["w_kr"], p["w_tail"], p["b_gate"], p["pool_w"],
              p["pool_scale"], p["q_norm"],
              p["w_uq"], p["kv_norm"], p["w_ukv"], p["v_norm_g"], p["v_norm_b"], p["w_s"],
              p["bs_full"], p["w_oa"], p["w_oc"]]
    head_shape = jax.ShapeDtypeStruct((B, MLA_HEADS, S, HEAD_PAD), bf16)
    return pl.pallas_call(
        _front_kernel,
        grid=(B, S // tb),
        in_specs=[tok(D_MODEL), tok(HEAD_PAD), tok(HEAD_PAD)] + [_const_spec(c, l) for c in consts],
        out_specs=[heads, heads, heads, tok(D_MODEL), tok(D_MODEL), tok(MLA_WIDTH)],
        out_shape=[head_shape, head_shape, head_shape,
                   jax.ShapeDtypeStruct((B, S, D_MODEL), bf16),
                   jax.ShapeDtypeStruct((B, S, D_MODEL), bf16),
                   jax.ShapeDtypeStruct((B, S, MLA_WIDTH), bf16)],
        scratch_shapes=[pltpu.VMEM((HALO + tb, POOL_WIDTH), f32)],
        compiler_params=pltpu.CompilerParams(
            dimension_semantics=("arbitrary", "arbitrary"),
            vmem_limit_bytes=V7X_VMEM_BYTES * 7 // 8),
        name="layer_front",
    )(x, cos_t, sin_t, *consts)


def _attn_kernel(q_ref, k_ref, v_ref, o_ref, s_ref, p_ref, m_ref, acc_ref):
    hp, tq, _ = q_ref.shape
    i = pl.program_id(2)
    lane = lax.broadcasted_iota(jnp.int32, (tq, HEAD_PAD), 1)
    n_tiles = tq // V7X_LANES
    half = tq // 2

    m_ref[...] = jnp.full(m_ref.shape, -jnp.inf, f32)
    acc_ref[...] = jnp.zeros(acc_ref.shape, f32)

    nt_dims = (((1,), (1,)), ((), ()))

    def scores(hh, j, slot, diagonal=False):
        start = pl.multiple_of(j * tq, tq)
        if diagonal:
            s_ref[slot, hh, 0:half, 0:half] = lax.dot_general(
                q_ref[hh, 0:half], k_ref[hh, pl.ds(start, half), :], nt_dims, preferred_element_type=f32)
            s_ref[slot, hh, half:tq, :] = lax.dot_general(
                q_ref[hh, half:tq], k_ref[hh, pl.ds(start, tq), :], nt_dims, preferred_element_type=f32)
        else:
            s_ref[slot, hh] = lax.dot_general(q_ref[hh], k_ref[hh, pl.ds(start, tq), :], nt_dims,
                                              preferred_element_type=f32)

    def softmax_pv(hh, j, slot, diagonal):
        for r in range(tq // SOFTMAX_ROWS):
            r0 = r * SOFTMAX_ROWS
            rows = slice(r0, r0 + SOFTMAX_ROWS)
            live = (r0 + SOFTMAX_ROWS - 1) // V7X_LANES + 1 if diagonal else n_tiles
            read = (half if r0 < half else tq) // V7X_LANES if diagonal else n_tiles

            def tile(c):
                s = s_ref[slot, hh, rows, c * V7X_LANES:(c + 1) * V7X_LANES]
                if diagonal and (c + 1) * V7X_LANES - 1 > r0:
                    row = r0 + lax.broadcasted_iota(jnp.int32, s.shape, 0)
                    col = c * V7X_LANES + lax.broadcasted_iota(jnp.int32, s.shape, 1)
                    s = jnp.where(col <= row, s, -jnp.inf)
                return s

            m_blk = tile(0)
            for c in range(1, live):
                m_blk = jnp.maximum(m_blk, tile(c))
            m_old = m_ref[hh, rows]
            m_new = jnp.maximum(m_old, jnp.max(m_blk, axis=-1, keepdims=True))
            m_ref[hh, rows] = m_new
            acc_ref[hh, rows] = jnp.exp2(m_old - m_new) * acc_ref[hh, rows]
            for c in range(live):
                p_ref[hh, rows, c * V7X_LANES:(c + 1) * V7X_LANES] = jnp.exp2(tile(c) - m_new).astype(bf16)
            if live < read:
                p_ref[hh, rows, live * V7X_LANES:read * V7X_LANES] = jnp.zeros(
                    (SOFTMAX_ROWS, (read - live) * V7X_LANES), bf16)
        start = pl.multiple_of(j * tq, tq)
        if diagonal:
            acc_ref[hh, 0:half] += _dot(p_ref[hh, 0:half, 0:half], v_ref[hh, pl.ds(start, half), :])
            acc_ref[hh, half:tq] += _dot(p_ref[hh, half:tq, :], v_ref[hh, pl.ds(start, tq), :])
        else:
            acc_ref[hh] += _dot(p_ref[hh], v_ref[hh, pl.ds(start, tq), :])

    def step(j, slot, next_diagonal=False):
        for hh in range(hp):
            softmax_pv(hh, j, slot, False)
            scores(hh, j + 1, 1 - slot, next_diagonal)

    def last_step(slot):
        for hh in range(hp):
            softmax_pv(hh, i, slot, True)

    @pl.when(i == 0)
    def _():
        for hh in range(hp):
            scores(hh, 0, 0, True)
        last_step(0)

    @pl.when(i > 0)
    def _():
        for hh in range(hp):
            scores(hh, 0, 0)

    def body(t, carry):
        step(2 * t, 0)
        step(2 * t + 1, 1)
        return carry

    lax.fori_loop(0, jnp.maximum(i - 1, 0) // 2, body, 0)

    @pl.when(i % 2 == 1)
    def _():
        step(i - 1, 0, True)
        last_step(1)

    @pl.when((i % 2 == 0) & (i > 0))
    def _():
        step(i - 2, 0)
        step(i - 1, 1, True)
        last_step(0)

    outs = [acc_ref[hh] / acc_ref[hh, :, ONES_LANE:ONES_LANE + 1] for hh in range(hp)]
    o_ref[...] = jnp.where(lane < V_HEAD_DIM, outs[0],
                           pltpu.roll(outs[1], V_HEAD_DIM, 1)).astype(o_ref.dtype)


def _attention(q, k, v):
    B, H, S, _ = q.shape
    tq = ATTN_BLOCK
    hp = HEADS_PER_STEP
    assert hp * V_HEAD_DIM == V7X_LANES
    return pl.pallas_call(
        _attn_kernel,
        grid=(B, H // hp, S // tq),
        in_specs=[pl.BlockSpec((None, hp, tq, HEAD_PAD), lambda b, g, i: (b, g, i, 0)),
                  pl.BlockSpec((None, hp, S, HEAD_PAD), lambda b, g, i: (b, g, 0, 0)),
                  pl.BlockSpec((None, hp, S, HEAD_PAD), lambda b, g, i: (b, g, 0, 0))],
        out_specs=pl.BlockSpec((None, tq, hp * V_HEAD_DIM), lambda b, g, i: (b, i, g)),
        out_shape=jax.ShapeDtypeStruct((B, S, H * V_HEAD_DIM), bf16),
        scratch_shapes=[pltpu.VMEM((2, hp, tq, tq), f32),
                        pltpu.VMEM((hp, tq, tq), bf16),
                        pltpu.VMEM((hp, tq, V7X_LANES), f32),
                        pltpu.VMEM((hp, tq, HEAD_PAD), f32)],
        compiler_params=pltpu.CompilerParams(
            dimension_semantics=("arbitrary", "arbitrary", "arbitrary"),
            vmem_limit_bytes=V7X_VMEM_BYTES * 7 // 8),
        name="mla_attention",
    )(q, k, v)


def _back_kernel(x_ref, o_ref, sb_ref, g1_ref, part_ref, w_ob_ref, w_out_ref, norm_post_ref,
                 out_ref):
    y_b = o_ref[...] * sb_ref[...]
    merged = part_ref[...].astype(f32) + g1_ref[...].astype(f32) * _dot(y_b, w_ob_ref[...])
    z = _dot(merged.astype(bf16), w_out_ref[...])
    out_ref[...] = x_ref[...] + _rms(z, norm_post_ref[...])


def _back(x, o, sb, g1, part, p, l):
    B, S, _ = x.shape
    tb = BACK_BLOCK
    tok = lambda width: pl.BlockSpec((None, tb, width), lambda b, j: (b, j, 0))
    consts = [p["w_ob"], p["w_out"], p["norm_post"]]
    return pl.pallas_call(
        _back_kernel,
        grid=(B, S // tb),
        in_specs=[tok(D_MODEL), tok(MLA_WIDTH), tok(MLA_WIDTH), tok(D_MODEL), tok(D_MODEL)]
        + [_const_spec(c, l) for c in consts],
        out_specs=tok(D_MODEL),
        out_shape=jax.ShapeDtypeStruct((B, S, D_MODEL), f32),
        compiler_params=pltpu.CompilerParams(
            dimension_semantics=("arbitrary", "arbitrary"),
            vmem_limit_bytes=V7X_VMEM_BYTES // 2),
        name="layer_back",
    )(x, o, sb, g1, part, *consts)


def _stacked_params(norm_pre, norm_post, w_in, b_gate, pool_w, pool_scale, q_norm, w_uq,
                    kv_norm, w_ukv, v_norm_g, v_norm_b, w_s, b_s, w_oa, w_ob, w_oc, w_out):
    L = w_in.shape[0]
    zeros = lambda *shape: jnp.zeros(shape, bf16)
    kr_src = sum(IN_SPLITS[:4])
    assert kr_src == COL_KR
    w_head = w_in[..., :kr_src].astype(bf16)
    w_kr = jnp.concatenate([zeros(L, D_MODEL, ROPE_LANE0),
                            w_in[..., kr_src:kr_src + QK_ROPE_DIM].astype(bf16),
                            zeros(L, D_MODEL, HEAD_PAD - ROPE_LANE0 - QK_ROPE_DIM)], axis=-1)
    w_tail = w_in[..., kr_src + QK_ROPE_DIM:].astype(bf16)
    assert w_tail.shape[-1] == IN_WIDTH_PAD - COL_B_GATE

    uq = w_uq.astype(bf16).reshape(L, Q_LORA_RANK, MLA_HEADS, QK_NOPE_DIM + QK_ROPE_DIM)
    nope, x1, x2 = uq[..., :QK_NOPE_DIM], uq[..., QK_NOPE_DIM:QK_NOPE_DIM + HALF_ROPE], uq[..., QK_NOPE_DIM + HALF_ROPE:]
    pad = zeros(L, Q_LORA_RANK, MLA_HEADS, HEAD_PAD - QK_NOPE_DIM - QK_ROPE_DIM)
    plain = jnp.concatenate([nope, x1, x2, pad], axis=-1).reshape(L, Q_LORA_RANK, MLA_HEADS * HEAD_PAD)
    swapped = jnp.concatenate([jnp.zeros_like(nope), x2, x1, pad], axis=-1).reshape(L, Q_LORA_RANK, MLA_HEADS * HEAD_PAD)
    w_uq_p = jnp.concatenate([plain, swapped], axis=-1)

    ukv = w_ukv.astype(bf16).reshape(L, KV_LORA_RANK, MLA_HEADS, QK_NOPE_DIM + V_HEAD_DIM)
    kpad = zeros(L, KV_LORA_RANK, MLA_HEADS, HEAD_PAD - QK_NOPE_DIM)
    vpad = zeros(L, KV_LORA_RANK, MLA_HEADS, HEAD_PAD - V_HEAD_DIM)
    k_cols = jnp.concatenate([ukv[..., :QK_NOPE_DIM], kpad], axis=-1).reshape(L, KV_LORA_RANK, MLA_HEADS * HEAD_PAD)
    v_cols = jnp.concatenate([ukv[..., QK_NOPE_DIM:], vpad], axis=-1).reshape(L, KV_LORA_RANK, MLA_HEADS * HEAD_PAD)
    w_ukv_p = jnp.concatenate([k_cols, v_cols], axis=-1)

    row = lambda a: a.reshape(L, 1, -1)
    return {
        "norm_pre": row(norm_pre), "norm_post": row(norm_post),
        "w_head": w_head, "w_kr": w_kr, "w_tail": w_tail,
        "b_gate": row(b_gate), "pool_w": pool_w.astype(bf16), "pool_scale": row(pool_scale),
        "q_norm": row(q_norm), "w_uq": w_uq_p, "kv_norm": row(kv_norm), "w_ukv": w_ukv_p,
        "v_norm_g": row(v_norm_g), "v_norm_b": row(v_norm_b), "w_s": w_s.astype(bf16),
        "bs_full": jnp.repeat(jnp.swapaxes(b_s, 1, 2), GMLP_GROUP_DIM, axis=2),
        "w_oa": w_oa.astype(bf16), "w_ob": w_ob.astype(bf16), "w_oc": w_oc.astype(bf16),
        "w_out": w_out.astype(bf16),
    }


def kernel(x, positions, norm_pre, norm_post, w_in, b_gate, pool_w, pool_scale, q_norm, w_uq, kv_norm, w_ukv, v_norm_g, v_norm_b, w_s, b_s, w_oa, w_ob, w_oc, w_out):
    B, S, _ = x.shape
    cos_t, sin_t = _rope_tables(positions)
    cos_t = cos_t.reshape(B, S, HEAD_PAD)
    sin_t = sin_t.reshape(B, S, HEAD_PAD)
    p = _stacked_params(norm_pre, norm_post, w_in, b_gate, pool_w, pool_scale, q_norm, w_uq,
                        kv_norm, w_ukv, v_norm_g, v_norm_b, w_s, b_s, w_oa, w_ob, w_oc, w_out)
    for l in range(norm_pre.shape[0]):
        q, k, v, part, g1, sb = _front(x, cos_t, sin_t, p, l)
        o = _attention(q, k, v)
        x = _back(x, o, sb, g1, part, p, l)
    return x
```

```python
import math

import jax
import jax.numpy as jnp
from jax import lax
from jax.experimental import pallas as pl
from jax.experimental.pallas import tpu as pltpu

D_MODEL = 1024
POOL_WIDTH = 512
POOL_GROUPS = 4
POOL_GROUP_DIM = POOL_WIDTH // POOL_GROUPS
POOL_WINDOWS = (2, 4, 8, 16)
MLA_HEADS = 8
QK_NOPE_DIM = 64
QK_ROPE_DIM = 32
HALF_ROPE = QK_ROPE_DIM // 2
V_HEAD_DIM = 64
Q_LORA_RANK = 256
KV_LORA_RANK = 128
MLA_WIDTH = MLA_HEADS * V_HEAD_DIM
ROPE_THETA = 10000.0
GMLP_WIDTH = 512
GMLP_CHUNK = 128
GMLP_GROUPS = 8
GMLP_GROUP_DIM = GMLP_WIDTH // GMLP_GROUPS
N_BRANCHES = 3
EPS = 1e-6
IN_SPLITS = (POOL_WIDTH, POOL_WIDTH, Q_LORA_RANK, KV_LORA_RANK, QK_ROPE_DIM, MLA_WIDTH,
             GMLP_WIDTH, GMLP_WIDTH, GMLP_WIDTH, N_BRANCHES * D_MODEL)

V7X_LANES = 128
V7X_VMEM_BYTES = 64 * 1024 * 1024

HEAD_PAD = V7X_LANES
ROPE_LANE0 = QK_NOPE_DIM
ONES_LANE = V_HEAD_DIM
HALO = max(POOL_WINDOWS)

COL_A_IN = 0
COL_A_GATE = COL_A_IN + POOL_WIDTH
COL_CQ = COL_A_GATE + POOL_WIDTH
COL_CKV = COL_CQ + Q_LORA_RANK
COL_KR = COL_CKV + KV_LORA_RANK
COL_B_GATE = COL_KR + HEAD_PAD
COL_U = COL_B_GATE + MLA_WIDTH
COL_V = COL_U + GMLP_WIDTH
COL_C_GATE = COL_V + GMLP_WIDTH
COL_GATES = COL_C_GATE + GMLP_WIDTH
IN_WIDTH_PAD = COL_GATES + N_BRANCHES * D_MODEL

SOFTMAX_SCALE = (QK_NOPE_DIM + QK_ROPE_DIM) ** -0.5
LOG2E = math.log2(math.e)

FRONT_BLOCK = 512
ATTN_BLOCK = 1024
BACK_BLOCK = 1024
ROPE_BLOCK = 1024
HEADS_PER_STEP = 2
SOFTMAX_ROWS = 64

bf16 = jnp.bfloat16
f32 = jnp.float32


def _sigmoid(z):
    return 0.5 * jnp.tanh(0.5 * z) + 0.5


def _silu(z):
    return z * _sigmoid(z)


def _rms(x, g):
    return x * lax.rsqrt(jnp.mean(x * x, axis=-1, keepdims=True) + EPS) * g


def _dot(a, b):
    return jnp.dot(a, b, preferred_element_type=f32)


def _rope_table_kernel(pos_ref, invf_ref, cos_ref, sin_ref):
    lane = lax.broadcasted_iota(jnp.int32, cos_ref.shape, 1)
    ang = pos_ref[...].astype(f32) * invf_ref[...]
    c = jnp.cos(ang)
    s = jnp.sin(ang)
    rope = (lane >= ROPE_LANE0) & (lane < ROPE_LANE0 + QK_ROPE_DIM)
    first_half = lane < ROPE_LANE0 + HALF_ROPE
    cos_ref[...] = jnp.where(lane < ROPE_LANE0, 1.0, jnp.where(rope, c, 0.0))
    sin_ref[...] = jnp.where(rope, jnp.where(first_half, -s, s), 0.0)


def _rope_tables(positions):
    n = positions.size
    inv_freq = ROPE_THETA ** (-jnp.arange(0, QK_ROPE_DIM, 2, dtype=f32) / QK_ROPE_DIM)
    pad = jnp.zeros((HEAD_PAD - ROPE_LANE0 - QK_ROPE_DIM,), f32)
    invf = jnp.concatenate([jnp.zeros((ROPE_LANE0,), f32), inv_freq, inv_freq, pad]).reshape(1, HEAD_PAD)
    out = jax.ShapeDtypeStruct((n, HEAD_PAD), f32)
    return pl.pallas_call(
        _rope_table_kernel,
        grid=(n // ROPE_BLOCK,),
        in_specs=[pl.BlockSpec((ROPE_BLOCK, 1), lambda i: (i, 0)),
                  pl.BlockSpec((1, HEAD_PAD), lambda i: (0, 0))],
        out_specs=[pl.BlockSpec((ROPE_BLOCK, HEAD_PAD), lambda i: (i, 0))] * 2,
        out_shape=[out, out],
        name="rope_tables",
    )(positions.reshape(n, 1), invf)


def _front_kernel(x_ref, cos_ref, sin_ref, norm_pre_ref, w_head_ref, w_kr_ref, w_tail_ref, b_gate_ref, pool_w_ref,
                  pool_scale_ref, q_norm_ref, w_uq_ref, kv_norm_ref, w_ukv_ref, vn_g_ref,
                  vn_b_ref, w_s_ref, bs_ref, w_oa_ref, w_oc_ref,
                  q_ref, k_ref, v_ref, part_ref, g1_ref, sb_ref, abuf):
    tb = x_ref.shape[0]
    j = pl.program_id(1)

    hb = _rms(x_ref[...], norm_pre_ref[...]).astype(bf16)

    def proj(c0, width):
        if c0 + width <= COL_KR:
            w = w_head_ref[:, c0:c0 + width]
        elif c0 == COL_KR and width == HEAD_PAD:
            w = w_kr_ref[...]
        else:
            assert c0 >= COL_B_GATE
            w = w_tail_ref[:, c0 - COL_B_GATE:c0 - COL_B_GATE + width]
        return _dot(hb, w)

    @pl.when(j == 0)
    def _():
        abuf[0:HALO, :] = jnp.zeros((HALO, POOL_WIDTH), f32)

    a_in = proj(COL_A_IN, POOL_WIDTH)
    abuf[HALO:HALO + tb, :] = a_in
    cq = proj(COL_CQ, Q_LORA_RANK)
    ckv = proj(COL_CKV, KV_LORA_RANK)
    kr = proj(COL_KR, HEAD_PAD)
    vv = proj(COL_V, GMLP_WIDTH)
    a_gate = proj(COL_A_GATE, POOL_WIDTH)
    b_gate = proj(COL_B_GATE, MLA_WIDTH)
    u = proj(COL_U, GMLP_WIDTH)
    c_gate = proj(COL_C_GATE, GMLP_WIDTH)

    t = j * tb + lax.broadcasted_iota(jnp.int32, (tb, 1), 0)
    ys = []
    for g, w in enumerate(POOL_WINDOWS):
        c0 = g * POOL_GROUP_DIM
        win = abuf[HALO:HALO + tb, c0:c0 + POOL_GROUP_DIM]
        for i in range(1, w):
            win = win + abuf[HALO - i:HALO - i + tb, c0:c0 + POOL_GROUP_DIM]
        count = jnp.minimum(t + 1, w).astype(f32)
        mixed = win / count - a_in[:, c0:c0 + POOL_GROUP_DIM]
        ys.append(_dot(mixed.astype(bf16), pool_w_ref[g]))
    abuf[0:HALO, :] = abuf[tb:tb + HALO, :]

    cqn = _rms(cq, q_norm_ref[...]).astype(bf16)
    q2 = _dot(cqn, w_uq_ref[...])
    ckvn = _rms(ckv, kv_norm_ref[...]).astype(bf16)
    kv2 = _dot(ckvn, w_ukv_ref[...])

    vc = vv - jnp.mean(vv, axis=-1, keepdims=True)
    var = jnp.mean(vc * vc, axis=-1, keepdims=True)
    vb = (vc * lax.rsqrt(var + EPS) * vn_g_ref[...] + vn_b_ref[...]).astype(bf16)
    row = lax.broadcasted_iota(jnp.int32, (GMLP_CHUNK, GMLP_CHUNK), 0)
    col = lax.broadcasted_iota(jnp.int32, (GMLP_CHUNK, GMLP_CHUNK), 1)
    ws = [jnp.where(col <= row, w_s_ref[g], jnp.zeros((), bf16)) for g in range(GMLP_GROUPS)]
    groups_per_tile = V7X_LANES // GMLP_GROUP_DIM
    assert groups_per_tile == 2
    chunks = []
    for c in range(tb // GMLP_CHUNK):
        tiles = []
        for tile in range(GMLP_WIDTH // V7X_LANES):
            rhs = vb[c * GMLP_CHUNK:(c + 1) * GMLP_CHUNK, tile * V7X_LANES:(tile + 1) * V7X_LANES]
            lo = _dot(ws[2 * tile], rhs)
            hi = _dot(ws[2 * tile + 1], rhs)
            tiles.append(jnp.where(col < GMLP_GROUP_DIM, lo, hi))
        chunks.append(jnp.concatenate(tiles, axis=1) + bs_ref[...])
    sv = jnp.concatenate(chunks, axis=0)

    gate_pre = [proj(COL_GATES + i * D_MODEL, D_MODEL) for i in range(N_BRANCHES)]

    cos_t = cos_ref[...]
    sin_t = sin_ref[...]
    lane = lax.broadcasted_iota(jnp.int32, (tb, HEAD_PAD), 1)
    swap0 = MLA_HEADS * HEAD_PAD
    for h in range(MLA_HEADS):
        plain = q2[:, h * HEAD_PAD:(h + 1) * HEAD_PAD]
        swapped = q2[:, swap0 + h * HEAD_PAD:swap0 + (h + 1) * HEAD_PAD]
        q_ref[h] = ((plain * cos_t + swapped * sin_t) * (SOFTMAX_SCALE * LOG2E)).astype(bf16)
    kr_swapped = jnp.where(lane < ROPE_LANE0 + HALF_ROPE,
                           pltpu.roll(kr, HEAD_PAD - HALF_ROPE, 1), pltpu.roll(kr, HALF_ROPE, 1))
    k_rope = kr * cos_t + kr_swapped * sin_t
    ones_lane = jnp.where(lane == ONES_LANE, 1.0, 0.0)
    for h in range(MLA_HEADS):
        k_ref[h] = (kv2[:, h * HEAD_PAD:(h + 1) * HEAD_PAD] + k_rope).astype(bf16)
        v_ref[h] = (kv2[:, swap0 + h * HEAD_PAD:swap0 + (h + 1) * HEAD_PAD] + ones_lane).astype(bf16)
    sb_ref[...] = _silu(b_gate).astype(sb_ref.dtype)

    y_a = jnp.concatenate(ys, axis=1) * pool_scale_ref[...] * _silu(a_gate)
    y_c = u * sv * _silu(c_gate)

    def gate(i):
        return _sigmoid(gate_pre[i] + b_gate_ref[:, i * D_MODEL:(i + 1) * D_MODEL])

    g1_ref[...] = gate(1).astype(g1_ref.dtype)
    part_ref[...] = (gate(0) * _dot(y_a.astype(bf16), w_oa_ref[...])
                     + gate(2) * _dot(y_c.astype(bf16), w_oc_ref[...])).astype(part_ref.dtype)


def _const_spec(stacked, l):
    index = (l,) + (0,) * (stacked.ndim - 1)
    return pl.BlockSpec((None,) + stacked.shape[1:], lambda b, j: index, pipeline_mode=pl.Buffered(1))


def _front(x, cos_t, sin_t, p, l):
    B, S, _ = x.shape
    tb = FRONT_BLOCK
    tok = lambda width: pl.BlockSpec((None, tb, width), lambda b, j: (b, j, 0))
    heads = pl.BlockSpec((None, MLA_HEADS, tb, HEAD_PAD), lambda b, j: (b, 0, j, 0))
    consts = [p["norm_pre"], p["w_head"], p["w_kr"], p["w_tail"], p["b_gate"], p["pool_w"],
              p["pool_scale"], p["q_norm"],
              p["w_uq"], p["kv_norm"], p["w_ukv"], p["v_norm_g"], p["v_norm_b"], p["w_s"],
              p["bs_full"], p["w_oa"], p["w_oc"]]
    head_shape = jax.ShapeDtypeStruct((B, MLA_HEADS, S, HEAD_PAD), bf16)
    return pl.pallas_call(
        _front_kernel,
        grid=(B, S // tb),
        in_specs=[tok(D_MODEL), tok(HEAD_PAD), tok(HEAD_PAD)] + [_const_spec(c, l) for c in consts],
        out_specs=[heads, heads, heads, tok(D_MODEL), tok(D_MODEL), tok(MLA_WIDTH)],
        out_shape=[head_shape, head_shape, head_shape,
                   jax.ShapeDtypeStruct((B, S, D_MODEL), bf16),
                   jax.ShapeDtypeStruct((B, S, D_MODEL), bf16),
                   jax.ShapeDtypeStruct((B, S, MLA_WIDTH), bf16)],
        scratch_shapes=[pltpu.VMEM((HALO + tb, POOL_WIDTH), f32)],
        compiler_params=pltpu.CompilerParams(
            dimension_semantics=("arbitrary", "arbitrary"),
            vmem_limit_bytes=V7X_VMEM_BYTES * 7 // 8),
        name="layer_front",
    )(x, cos_t, sin_t, *consts)


def _attn_kernel(q_ref, k_ref, v_ref, o_ref, s_ref, p_ref, m_ref, acc_ref):
    hp, tq, _ = q_ref.shape
    i = pl.program_id(2)
    lane = lax.broadcasted_iota(jnp.int32, (tq, HEAD_PAD), 1)
    n_tiles = tq // V7X_LANES
    half = tq // 2

    m_ref[...] = jnp.full(m_ref.shape, -jnp.inf, f32)
    acc_ref[...] = jnp.zeros(acc_ref.shape, f32)

    nt_dims = (((1,), (1,)), ((), ()))

    def scores(hh, j, slot, diagonal=False):
        start = pl.multiple_of(j * tq, tq)
        if diagonal:
            s_ref[slot, hh, 0:half, 0:half] = lax.dot_general(
                q_ref[hh, 0:half], k_ref[hh, pl.ds(start, half), :], nt_dims, preferred_element_type=f32)
            s_ref[slot, hh, half:tq, :] = lax.dot_general(
                q_ref[hh, half:tq], k_ref[hh, pl.ds(start, tq), :], nt_dims, preferred_element_type=f32)
        else:
            s_ref[slot, hh] = lax.dot_general(q_ref[hh], k_ref[hh, pl.ds(start, tq), :], nt_dims,
                                              preferred_element_type=f32)

    def softmax_pv(hh, j, slot, diagonal):
        for r in range(tq // SOFTMAX_ROWS):
            r0 = r * SOFTMAX_ROWS
            rows = slice(r0, r0 + SOFTMAX_ROWS)
            live = (r0 + SOFTMAX_ROWS - 1) // V7X_LANES + 1 if diagonal else n_tiles
            read = (half if r0 < half else tq) // V7X_LANES if diagonal else n_tiles

            def tile(c):
                s = s_ref[slot, hh, rows, c * V7X_LANES:(c + 1) * V7X_LANES]
                if diagonal and (c + 1) * V7X_LANES - 1 > r0:
                    row = r0 + lax.broadcasted_iota(jnp.int32, s.shape, 0)
                    col = c * V7X_LANES + lax.broadcasted_iota(jnp.int32, s.shape, 1)
                    s = jnp.where(col <= row, s, -jnp.inf)
                return s

            m_blk = tile(0)
            for c in range(1, live):
                m_blk = jnp.maximum(m_blk, tile(c))
            m_old = m_ref[hh, rows]
            m_new = jnp.maximum(m_old, jnp.max(m_blk, axis=-1, keepdims=True))
            m_ref[hh, rows] = m_new
            acc_ref[hh, rows] = jnp.exp2(m_old - m_new) * acc_ref[hh, rows]
            for c in range(live):
                p_ref[hh, rows, c * V7X_LANES:(c + 1) * V7X_LANES] = jnp.exp2(tile(c) - m_new).astype(bf16)
            if live < read:
                p_ref[hh, rows, live * V7X_LANES:read * V7X_LANES] = jnp.zeros(
                    (SOFTMAX_ROWS, (read - live) * V7X_LANES), bf16)
        start = pl.multiple_of(j * tq, tq)
        if diagonal:
            acc_ref[hh, 0:half] += _dot(p_ref[hh, 0:half, 0:half], v_ref[hh, pl.ds(start, half), :])
            acc_ref[hh, half:tq] += _dot(p_ref[hh, half:tq, :], v_ref[hh, pl.ds(start, tq), :])
        else:
            acc_ref[hh] += _dot(p_ref[hh], v_ref[hh, pl.ds(start, tq), :])

    def step(j, slot, next_diagonal=False):
        for hh in range(hp):
            softmax_pv(hh, j, slot, False)
            scores(hh, j + 1, 1 - slot, next_diagonal)

    def last_step(slot):
        for hh in range(hp):
            softmax_pv(hh, i, slot, True)

    @pl.when(i == 0)
    def _():
        for hh in range(hp):
            scores(hh, 0, 0, True)
        last_step(0)

    @pl.when(i > 0)
    def _():
        for hh in range(hp):
            scores(hh, 0, 0)

    def body(t, carry):
        step(2 * t, 0)
        step(2 * t + 1, 1)
        return carry

    lax.fori_loop(0, jnp.maximum(i - 1, 0) // 2, body, 0)

    @pl.when(i % 2 == 1)
    def _():
        step(i - 1, 0, True)
        last_step(1)

    @pl.when((i % 2 == 0) & (i > 0))
    def _():
        step(i - 2, 0)
        step(i - 1, 1, True)
        last_step(0)

    outs = [acc_ref[hh] / acc_ref[hh, :, ONES_LANE:ONES_LANE + 1] for hh in range(hp)]
    o_ref[...] = jnp.where(lane < V_HEAD_DIM, outs[0],
                           pltpu.roll(outs[1], V_HEAD_DIM, 1)).astype(o_ref.dtype)


def _attention(q, k, v):
    B, H, S, _ = q.shape
    tq = ATTN_BLOCK
    hp = HEADS_PER_STEP
    assert hp * V_HEAD_DIM == V7X_LANES
    return pl.pallas_call(
        _attn_kernel,
        grid=(B, H // hp, S // tq),
        in_specs=[pl.BlockSpec((None, hp, tq, HEAD_PAD), lambda b, g, i: (b, g, i, 0)),
                  pl.BlockSpec((None, hp, S, HEAD_PAD), lambda b, g, i: (b, g, 0, 0)),
                  pl.BlockSpec((None, hp, S, HEAD_PAD), lambda b, g, i: (b, g, 0, 0))],
        out_specs=pl.BlockSpec((None, tq, hp * V_HEAD_DIM), lambda b, g, i: (b, i, g)),
        out_shape=jax.ShapeDtypeStruct((B, S, H * V_HEAD_DIM), bf16),
        scratch_shapes=[pltpu.VMEM((2, hp, tq, tq), f32),
                        pltpu.VMEM((hp, tq, tq), bf16),
                        pltpu.VMEM((hp, tq, V7X_LANES), f32),
                        pltpu.VMEM((hp, tq, HEAD_PAD), f32)],
        compiler_params=pltpu.CompilerParams(
            dimension_semantics=("arbitrary", "arbitrary", "arbitrary"),
            vmem_limit_bytes=V7X_VMEM_BYTES * 7 // 8),
        name="mla_attention",
    )(q, k, v)


def _back_kernel(x_ref, o_ref, sb_ref, g1_ref, part_ref, w_ob_ref, w_out_ref, norm_post_ref,
                 out_ref):
    y_b = o_ref[...] * sb_ref[...]
    merged = part_ref[...].astype(f32) + g1_ref[...].astype(f32) * _dot(y_b, w_ob_ref[...])
    z = _dot(merged.astype(bf16), w_out_ref[...])
    out_ref[...] = x_ref[...] + _rms(z, norm_post_ref[...])


def _back(x, o, sb, g1, part, p, l):
    B, S, _ = x.shape
    tb = BACK_BLOCK
    tok = lambda width: pl.BlockSpec((None, tb, width), lambda b, j: (b, j, 0))
    consts = [p["w_ob"], p["w_out"], p["norm_post"]]
    return pl.pallas_call(
        _back_kernel,
        grid=(B, S // tb),
        in_specs=[tok(D_MODEL), tok(MLA_WIDTH), tok(MLA_WIDTH), tok(D_MODEL), tok(D_MODEL)]
        + [_const_spec(c, l) for c in consts],
        out_specs=tok(D_MODEL),
        out_shape=jax.ShapeDtypeStruct((B, S, D_MODEL), f32),
        compiler_params=pltpu.CompilerParams(
            dimension_semantics=("arbitrary", "arbitrary"),
            vmem_limit_bytes=V7X_VMEM_BYTES * 3 // 4),
        name="layer_back",
    )(x, o, sb, g1, part, *consts)


def _stacked_params(norm_pre, norm_post, w_in, b_gate, pool_w, pool_scale, q_norm, w_uq,
                    kv_norm, w_ukv, v_norm_g, v_norm_b, w_s, b_s, w_oa, w_ob, w_oc, w_out):
    L = w_in.shape[0]
    zeros = lambda *shape: jnp.zeros(shape, bf16)
    kr_src = sum(IN_SPLITS[:4])
    assert kr_src == COL_KR
    w_head = w_in[..., :kr_src].astype(bf16)
    w_kr = jnp.concatenate([zeros(L, D_MODEL, ROPE_LANE0),
                            w_in[..., kr_src:kr_src + QK_ROPE_DIM].astype(bf16),
                            zeros(L, D_MODEL, HEAD_PAD - ROPE_LANE0 - QK_ROPE_DIM)], axis=-1)
    w_tail = w_in[..., kr_src + QK_ROPE_DIM:].astype(bf16)
    assert w_tail.shape[-1] == IN_WIDTH_PAD - COL_B_GATE

    uq = w_uq.astype(bf16).reshape(L, Q_LORA_RANK, MLA_HEADS, QK_NOPE_DIM + QK_ROPE_DIM)
    nope, x1, x2 = uq[..., :QK_NOPE_DIM], uq[..., QK_NOPE_DIM:QK_NOPE_DIM + HALF_ROPE], uq[..., QK_NOPE_DIM + HALF_ROPE:]
    pad = zeros(L, Q_LORA_RANK, MLA_HEADS, HEAD_PAD - QK_NOPE_DIM - QK_ROPE_DIM)
    plain = jnp.concatenate([nope, x1, x2, pad], axis=-1).reshape(L, Q_LORA_RANK, MLA_HEADS * HEAD_PAD)
    swapped = jnp.concatenate([jnp.zeros_like(nope), x2, x1, pad], axis=-1).reshape(L, Q_LORA_RANK, MLA_HEADS * HEAD_PAD)
    w_uq_p = jnp.concatenate([plain, swapped], axis=-1)

    ukv = w_ukv.astype(bf16).reshape(L, KV_LORA_RANK, MLA_HEADS, QK_NOPE_DIM + V_HEAD_DIM)
    kpad = zeros(L, KV_LORA_RANK, MLA_HEADS, HEAD_PAD - QK_NOPE_DIM)
    vpad = zeros(L, KV_LORA_RANK, MLA_HEADS, HEAD_PAD - V_HEAD_DIM)
    k_cols = jnp.concatenate([ukv[..., :QK_NOPE_DIM], kpad], axis=-1).reshape(L, KV_LORA_RANK, MLA_HEADS * HEAD_PAD)
    v_cols = jnp.concatenate([ukv[..., QK_NOPE_DIM:], vpad], axis=-1).reshape(L, KV_LORA_RANK, MLA_HEADS * HEAD_PAD)
    w_ukv_p = jnp.concatenate([k_cols, v_cols], axis=-1)

    row = lambda a: a.reshape(L, 1, -1)
    return {
        "norm_pre": row(norm_pre), "norm_post": row(norm_post),
        "w_head": w_head, "w_kr": w_kr, "w_tail": w_tail,
        "b_gate": row(b_gate), "pool_w": pool_w.astype(bf16), "pool_scale": row(pool_scale),
        "q_norm": row(q_norm), "w_uq": w_uq_p, "kv_norm": row(kv_norm), "w_ukv": w_ukv_p,
        "v_norm_g": row(v_norm_g), "v_norm_b": row(v_norm_b), "w_s": w_s.astype(bf16),
        "bs_full": jnp.repeat(jnp.swapaxes(b_s, 1, 2), GMLP_GROUP_DIM, axis=2),
        "w_oa": w_oa.astype(bf16), "w_ob": w_ob.astype(bf16), "w_oc": w_oc.astype(bf16),
        "w_out": w_out.astype(bf16),
    }


def kernel(x, positions, norm_pre, norm_post, w_in, b_gate, pool_w, pool_scale, q_norm, w_uq, kv_norm, w_ukv, v_norm_g, v_norm_b, w_s, b_s, w_oa, w_ob, w_oc, w_out):
    B, S, _ = x.shape
    cos_t, sin_t = _rope_tables(positions)
    cos_t = cos_t.reshape(B, S, HEAD_PAD)
    sin_t = sin_t.reshape(B, S, HEAD_PAD)
    p = _stacked_params(norm_pre, norm_post, w_in, b_gate, pool_w, pool_scale, q_norm, w_uq,
                        kv_norm, w_ukv, v_norm_g, v_norm_b, w_s, b_s, w_oa, w_ob, w_oc, w_out)
    for l in range(norm_pre.shape[0]):
        q, k, v, part, g1, sb = _front(x, cos_t, sin_t, p, l)
        o = _attention(q, k, v)
        x = _back(x, o, sb, g1, part, p, l)
    return x
```

```python
import math

import jax
import jax.numpy as jnp
from jax import lax
from jax.experimental import pallas as pl
from jax.experimental.pallas import tpu as pltpu

D_MODEL = 1024
POOL_WIDTH = 512
POOL_GROUPS = 4
POOL_GROUP_DIM = POOL_WIDTH // POOL_GROUPS
POOL_WINDOWS = (2, 4, 8, 16)
MLA_HEADS = 8
QK_NOPE_DIM = 64
QK_ROPE_DIM = 32
HALF_ROPE = QK_ROPE_DIM // 2
V_HEAD_DIM = 64
Q_LORA_RANK = 256
KV_LORA_RANK = 128
MLA_WIDTH = MLA_HEADS * V_HEAD_DIM
ROPE_THETA = 10000.0
GMLP_WIDTH = 512
GMLP_CHUNK = 128
GMLP_GROUPS = 8
GMLP_GROUP_DIM = GMLP_WIDTH // GMLP_GROUPS
N_BRANCHES = 3
EPS = 1e-6
IN_SPLITS = (POOL_WIDTH, POOL_WIDTH, Q_LORA_RANK, KV_LORA_RANK, QK_ROPE_DIM, MLA_WIDTH,
             GMLP_WIDTH, GMLP_WIDTH, GMLP_WIDTH, N_BRANCHES * D_MODEL)

V7X_LANES = 128
V7X_VMEM_BYTES = 64 * 1024 * 1024

HEAD_PAD = V7X_LANES
ROPE_LANE0 = QK_NOPE_DIM
ONES_LANE = V_HEAD_DIM
HALO = max(POOL_WINDOWS)

COL_A_IN = 0
COL_A_GATE = COL_A_IN + POOL_WIDTH
COL_CQ = COL_A_GATE + POOL_WIDTH
COL_CKV = COL_CQ + Q_LORA_RANK
COL_KR = COL_CKV + KV_LORA_RANK
COL_B_GATE = COL_KR + HEAD_PAD
COL_U = COL_B_GATE + MLA_WIDTH
COL_V = COL_U + GMLP_WIDTH
COL_C_GATE = COL_V + GMLP_WIDTH
COL_GATES = COL_C_GATE + GMLP_WIDTH
IN_WIDTH_PAD = COL_GATES + N_BRANCHES * D_MODEL

SOFTMAX_SCALE = (QK_NOPE_DIM + QK_ROPE_DIM) ** -0.5
LOG2E = math.log2(math.e)

FRONT_BLOCK = 512
ATTN_BLOCK = 1024
BACK_BLOCK = 1024
ROPE_BLOCK = 1024
HEADS_PER_STEP = 2
SOFTMAX_ROWS = 64

bf16 = jnp.bfloat16
f32 = jnp.float32


def _sigmoid(z):
    return 0.5 * jnp.tanh(0.5 * z) + 0.5


def _silu(z):
    return z * _sigmoid(z)


def _rms(x, g):
    return x * lax.rsqrt(jnp.mean(x * x, axis=-1, keepdims=True) + EPS) * g


def _dot(a, b):
    return jnp.dot(a, b, preferred_element_type=f32)


def _rope_table_kernel(pos_ref, invf_ref, cos_ref, sin_ref):
    lane = lax.broadcasted_iota(jnp.int32, cos_ref.shape, 1)
    ang = pos_ref[...].astype(f32) * invf_ref[...]
    c = jnp.cos(ang)
    s = jnp.sin(ang)
    rope = (lane >= ROPE_LANE0) & (lane < ROPE_LANE0 + QK_ROPE_DIM)
    first_half = lane < ROPE_LANE0 + HALF_ROPE
    cos_ref[...] = jnp.where(lane < ROPE_LANE0, 1.0, jnp.where(rope, c, 0.0))
    sin_ref[...] = jnp.where(rope, jnp.where(first_half, -s, s), 0.0)


def _rope_tables(positions):
    n = positions.size
    inv_freq = ROPE_THETA ** (-jnp.arange(0, QK_ROPE_DIM, 2, dtype=f32) / QK_ROPE_DIM)
    pad = jnp.zeros((HEAD_PAD - ROPE_LANE0 - QK_ROPE_DIM,), f32)
    invf = jnp.concatenate([jnp.zeros((ROPE_LANE0,), f32), inv_freq, inv_freq, pad]).reshape(1, HEAD_PAD)
    out = jax.ShapeDtypeStruct((n, HEAD_PAD), f32)
    return pl.pallas_call(
        _rope_table_kernel,
        grid=(n // ROPE_BLOCK,),
        in_specs=[pl.BlockSpec((ROPE_BLOCK, 1), lambda i: (i, 0)),
                  pl.BlockSpec((1, HEAD_PAD), lambda i: (0, 0))],
        out_specs=[pl.BlockSpec((ROPE_BLOCK, HEAD_PAD), lambda i: (i, 0))] * 2,
        out_shape=[out, out],
        name="rope_tables",
    )(positions.reshape(n, 1), invf)


def _front_kernel(x_ref, cos_ref, sin_ref, norm_pre_ref, w_head_ref, w_kr_ref, w_tail_ref, b_gate_ref, pool_w_ref,
                  pool_scale_ref, q_norm_ref, w_uq_ref, kv_norm_ref, w_ukv_ref, vn_g_ref,
                  vn_b_ref, w_s_ref, bs_ref, w_oa_ref, w_oc_ref,
                  q_ref, k_ref, v_ref, part_ref, g1_ref, sb_ref, abuf):
    tb = x_ref.shape[0]
    j = pl.program_id(1)

    hb = _rms(x_ref[...], norm_pre_ref[...]).astype(bf16)

    def proj(c0, width):
        if c0 + width <= COL_KR:
            w = w_head_ref[:, c0:c0 + width]
        elif c0 == COL_KR and width == HEAD_PAD:
            w = w_kr_ref[...]
        else:
            assert c0 >= COL_B_GATE
            w = w_tail_ref[:, c0 - COL_B_GATE:c0 - COL_B_GATE + width]
        return _dot(hb, w)

    @pl.when(j == 0)
    def _():
        abuf[0:HALO, :] = jnp.zeros((HALO, POOL_WIDTH), f32)

    a_in = proj(COL_A_IN, POOL_WIDTH)
    abuf[HALO:HALO + tb, :] = a_in
    cq = proj(COL_CQ, Q_LORA_RANK)
    ckv = proj(COL_CKV, KV_LORA_RANK)
    kr = proj(COL_KR, HEAD_PAD)
    vv = proj(COL_V, GMLP_WIDTH)
    a_gate = proj(COL_A_GATE, POOL_WIDTH)
    b_gate = proj(COL_B_GATE, MLA_WIDTH)
    u = proj(COL_U, GMLP_WIDTH)
    c_gate = proj(COL_C_GATE, GMLP_WIDTH)

    t = j * tb + lax.broadcasted_iota(jnp.int32, (tb, 1), 0)
    ys = []
    for g, w in enumerate(POOL_WINDOWS):
        c0 = g * POOL_GROUP_DIM
        win = abuf[HALO:HALO + tb, c0:c0 + POOL_GROUP_DIM]
        for i in range(1, w):
            win = win + abuf[HALO - i:HALO - i + tb, c0:c0 + POOL_GROUP_DIM]
        count = jnp.minimum(t + 1, w).astype(f32)
        mixed = win / count - a_in[:, c0:c0 + POOL_GROUP_DIM]
        ys.append(_dot(mixed.astype(bf16), pool_w_ref[g]))
    abuf[0:HALO, :] = abuf[tb:tb + HALO, :]

    cqn = _rms(cq, q_norm_ref[...]).astype(bf16)
    q2 = _dot(cqn, w_uq_ref[...])
    ckvn = _rms(ckv, kv_norm_ref[...]).astype(bf16)
    kv2 = _dot(ckvn, w_ukv_ref[...])

    vc = vv - jnp.mean(vv, axis=-1, keepdims=True)
    var = jnp.mean(vc * vc, axis=-1, keepdims=True)
    vb = (vc * lax.rsqrt(var + EPS) * vn_g_ref[...] + vn_b_ref[...]).astype(bf16)
    row = lax.broadcasted_iota(jnp.int32, (GMLP_CHUNK, GMLP_CHUNK), 0)
    col = lax.broadcasted_iota(jnp.int32, (GMLP_CHUNK, GMLP_CHUNK), 1)
    ws = [jnp.where(col <= row, w_s_ref[g], jnp.zeros((), bf16)) for g in range(GMLP_GROUPS)]
    groups_per_tile = V7X_LANES // GMLP_GROUP_DIM
    assert groups_per_tile == 2
    chunks = []
    for c in range(tb // GMLP_CHUNK):
        tiles = []
        for tile in range(GMLP_WIDTH // V7X_LANES):
            rhs = vb[c * GMLP_CHUNK:(c + 1) * GMLP_CHUNK, tile * V7X_LANES:(tile + 1) * V7X_LANES]
            lo = _dot(ws[2 * tile], rhs)
            hi = _dot(ws[2 * tile + 1], rhs)
            tiles.append(jnp.where(col < GMLP_GROUP_DIM, lo, hi))
        chunks.append(jnp.concatenate(tiles, axis=1) + bs_ref[...])
    sv = jnp.concatenate(chunks, axis=0)

    gate_pre = [proj(COL_GATES + i * D_MODEL, D_MODEL) for i in range(N_BRANCHES)]

    cos_t = cos_ref[...]
    sin_t = sin_ref[...]
    lane = lax.broadcasted_iota(jnp.int32, (tb, HEAD_PAD), 1)
    first_half = lane < ROPE_LANE0 + HALF_ROPE

    def rotate(t):
        swapped = jnp.where(first_half, pltpu.roll(t, HEAD_PAD - HALF_ROPE, 1), pltpu.roll(t, HALF_ROPE, 1))
        return t * cos_t + swapped * sin_t

    for h in range(MLA_HEADS):
        q_ref[h] = (rotate(q2[:, h * HEAD_PAD:(h + 1) * HEAD_PAD]) * (SOFTMAX_SCALE * LOG2E)).astype(bf16)
    k_rope = rotate(kr)
    low = lane < QK_NOPE_DIM
    ones_lane = jnp.where(lane == ONES_LANE, 1.0, 0.0)
    v0 = MLA_HEADS * QK_NOPE_DIM
    for pair in range(MLA_HEADS // 2):
        k_pair = kv2[:, pair * HEAD_PAD:(pair + 1) * HEAD_PAD]
        v_pair = kv2[:, v0 + pair * HEAD_PAD:v0 + (pair + 1) * HEAD_PAD]
        for odd in range(2):
            k_h = pltpu.roll(k_pair, QK_NOPE_DIM, 1) if odd else k_pair
            v_h = pltpu.roll(v_pair, V_HEAD_DIM, 1) if odd else v_pair
            k_ref[2 * pair + odd] = (jnp.where(low, k_h, 0.0) + k_rope).astype(bf16)
            v_ref[2 * pair + odd] = jnp.where(low, v_h, ones_lane).astype(bf16)
    sb_ref[...] = _silu(b_gate).astype(sb_ref.dtype)

    y_a = jnp.concatenate(ys, axis=1) * pool_scale_ref[...] * _silu(a_gate)
    y_c = u * sv * _silu(c_gate)

    def gate(i):
        return _sigmoid(gate_pre[i] + b_gate_ref[:, i * D_MODEL:(i + 1) * D_MODEL])

    g1_ref[...] = gate(1).astype(g1_ref.dtype)
    part_ref[...] = (gate(0) * _dot(y_a.astype(bf16), w_oa_ref[...])
                     + gate(2) * _dot(y_c.astype(bf16), w_oc_ref[...])).astype(part_ref.dtype)


def _const_spec(stacked, l):
    index = (l,) + (0,) * (stacked.ndim - 1)
    return pl.BlockSpec((None,) + stacked.shape[1:], lambda b, j: index, pipeline_mode=pl.Buffered(1))


def _front(x, cos_t, sin_t, p, l):
    B, S, _ = x.shape
    tb = FRONT_BLOCK
    tok = lambda width: pl.BlockSpec((None, tb, width), lambda b, j: (b, j, 0))
    heads = pl.BlockSpec((None, MLA_HEADS, tb, HEAD_PAD), lambda b, j: (b, 0, j, 0))
    consts = [p["norm_pre"], p["w_head"], p["w_kr"], p["w_tail"], p["b_gate"], p["pool_w"],
              p["pool_scale"], p["q_norm"],
              p["w_uq"], p["kv_norm"], p["w_ukv"], p["v_norm_g"], p["v_norm_b"], p["w_s"],
              p["bs_full"], p["w_oa"], p["w_oc"]]
    head_shape = jax.ShapeDtypeStruct((B, MLA_HEADS, S, HEAD_PAD), bf16)
    return pl.pallas_call(
        _front_kernel,
        grid=(B, S // tb),
        in_specs=[tok(D_MODEL), tok(HEAD_PAD), tok(HEAD_PAD)] + [_const_spec(c, l) for c in consts],
        out_specs=[heads, heads, heads, tok(D_MODEL), tok(D_MODEL), tok(MLA_WIDTH)],
        out_shape=[head_shape, head_shape, head_shape,
                   jax.ShapeDtypeStruct((B, S, D_MODEL), bf16),
                   jax.ShapeDtypeStruct((B, S, D_MODEL), bf16),
                   jax.ShapeDtypeStruct((B, S, MLA_WIDTH), bf16)],
        scratch_shapes=[pltpu.VMEM((HALO + tb, POOL_WIDTH), f32)],
        compiler_params=pltpu.CompilerParams(
            dimension_semantics=("arbitrary", "arbitrary"),
            vmem_limit_bytes=V7X_VMEM_BYTES * 7 // 8),
        name="layer_front",
    )(x, cos_t, sin_t, *consts)


def _attn_kernel(q_ref, k_ref, v_ref, o_ref, s_ref, p_ref, m_ref, acc_ref):
    hp, tq, _ = q_ref.shape
    i = pl.program_id(2)
    lane = lax.broadcasted_iota(jnp.int32, (tq, HEAD_PAD), 1)
    n_tiles = tq // V7X_LANES
    half = tq // 2

    m_ref[...] = jnp.full(m_ref.shape, -jnp.inf, f32)
    acc_ref[...] = jnp.zeros(acc_ref.shape, f32)

    nt_dims = (((1,), (1,)), ((), ()))

    def scores(hh, j, slot, diagonal=False):
        start = pl.multiple_of(j * tq, tq)
        if diagonal:
            s_ref[slot, hh, 0:half, 0:half] = lax.dot_general(
                q_ref[hh, 0:half], k_ref[hh, pl.ds(start, half), :], nt_dims, preferred_element_type=f32)
            s_ref[slot, hh, half:tq, :] = lax.dot_general(
                q_ref[hh, half:tq], k_ref[hh, pl.ds(start, tq), :], nt_dims, preferred_element_type=f32)
        else:
            s_ref[slot, hh] = lax.dot_general(q_ref[hh], k_ref[hh, pl.ds(start, tq), :], nt_dims,
                                              preferred_element_type=f32)

    def softmax_pv(hh, j, slot, diagonal):
        for r in range(tq // SOFTMAX_ROWS):
            r0 = r * SOFTMAX_ROWS
            rows = slice(r0, r0 + SOFTMAX_ROWS)
            live = (r0 + SOFTMAX_ROWS - 1) // V7X_LANES + 1 if diagonal else n_tiles
            read = (half if r0 < half else tq) // V7X_LANES if diagonal else n_tiles

            def tile(c):
                s = s_ref[slot, hh, rows, c * V7X_LANES:(c + 1) * V7X_LANES]
                if diagonal and (c + 1) * V7X_LANES - 1 > r0:
                    row = r0 + lax.broadcasted_iota(jnp.int32, s.shape, 0)
                    col = c * V7X_LANES + lax.broadcasted_iota(jnp.int32, s.shape, 1)
                    s = jnp.where(col <= row, s, -jnp.inf)
                return s

            m_blk = tile(0)
            for c in range(1, live):
                m_blk = jnp.maximum(m_blk, tile(c))
            m_old = m_ref[hh, rows]
            m_new = jnp.maximum(m_old, jnp.max(m_blk, axis=-1, keepdims=True))
            m_ref[hh, rows] = m_new
            acc_ref[hh, rows] = jnp.exp2(m_old - m_new) * acc_ref[hh, rows]
            for c in range(live):
                p_ref[hh, rows, c * V7X_LANES:(c + 1) * V7X_LANES] = jnp.exp2(tile(c) - m_new).astype(bf16)
            if live < read:
                p_ref[hh, rows, live * V7X_LANES:read * V7X_LANES] = jnp.zeros(
                    (SOFTMAX_ROWS, (read - live) * V7X_LANES), bf16)
        start = pl.multiple_of(j * tq, tq)
        if diagonal:
            acc_ref[hh, 0:half] += _dot(p_ref[hh, 0:half, 0:half], v_ref[hh, pl.ds(start, half), :])
            acc_ref[hh, half:tq] += _dot(p_ref[hh, half:tq, :], v_ref[hh, pl.ds(start, tq), :])
        else:
            acc_ref[hh] += _dot(p_ref[hh], v_ref[hh, pl.ds(start, tq), :])

    def step(j, slot, next_diagonal=False):
        for hh in range(hp):
            softmax_pv(hh, j, slot, False)
            scores(hh, j + 1, 1 - slot, next_diagonal)

    def last_step(slot):
        for hh in range(hp):
            softmax_pv(hh, i, slot, True)

    @pl.when(i == 0)
    def _():
        for hh in range(hp):
            scores(hh, 0, 0, True)
        last_step(0)

    @pl.when(i > 0)
    def _():
        for hh in range(hp):
            scores(hh, 0, 0)

    def body(t, carry):
        step(2 * t, 0)
        step(2 * t + 1, 1)
        return carry

    lax.fori_loop(0, jnp.maximum(i - 1, 0) // 2, body, 0)

    @pl.when(i % 2 == 1)
    def _():
        step(i - 1, 0, True)
        last_step(1)

    @pl.when((i % 2 == 0) & (i > 0))
    def _():
        step(i - 2, 0)
        step(i - 1, 1, True)
        last_step(0)

    outs = [acc_ref[hh] / acc_ref[hh, :, ONES_LANE:ONES_LANE + 1] for hh in range(hp)]
    o_ref[...] = jnp.where(lane < V_HEAD_DIM, outs[0],
                           pltpu.roll(outs[1], V_HEAD_DIM, 1)).astype(o_ref.dtype)


def _attention(q, k, v):
    B, H, S, _ = q.shape
    tq = ATTN_BLOCK
    hp = HEADS_PER_STEP
    assert hp * V_HEAD_DIM == V7X_LANES
    return pl.pallas_call(
        _attn_kernel,
        grid=(B, H // hp, S // tq),
        in_specs=[pl.BlockSpec((None, hp, tq, HEAD_PAD), lambda b, g, i: (b, g, i, 0)),
                  pl.BlockSpec((None, hp, S, HEAD_PAD), lambda b, g, i: (b, g, 0, 0)),
                  pl.BlockSpec((None, hp, S, HEAD_PAD), lambda b, g, i: (b, g, 0, 0))],
        out_specs=pl.BlockSpec((None, tq, hp * V_HEAD_DIM), lambda b, g, i: (b, i, g)),
        out_shape=jax.ShapeDtypeStruct((B, S, H * V_HEAD_DIM), bf16),
        scratch_shapes=[pltpu.VMEM((2, hp, tq, tq), f32),
                        pltpu.VMEM((hp, tq, tq), bf16),
                        pltpu.VMEM((hp, tq, V7X_LANES), f32),
                        pltpu.VMEM((hp, tq, HEAD_PAD), f32)],
        compiler_params=pltpu.CompilerParams(
            dimension_semantics=("arbitrary", "arbitrary", "arbitrary"),
            vmem_limit_bytes=V7X_VMEM_BYTES * 7 // 8),
        name="mla_attention",
    )(q, k, v)


def _back_kernel(x_ref, o_ref, sb_ref, g1_ref, part_ref, w_ob_ref, w_out_ref, norm_post_ref,
                 out_ref):
    y_b = o_ref[...] * sb_ref[...]
    merged = part_ref[...].astype(f32) + g1_ref[...].astype(f32) * _dot(y_b, w_ob_ref[...])
    z = _dot(merged.astype(bf16), w_out_ref[...])
    out_ref[...] = x_ref[...] + _rms(z, norm_post_ref[...])


def _back(x, o, sb, g1, part, p, l):
    B, S, _ = x.shape
    tb = BACK_BLOCK
    tok = lambda width: pl.BlockSpec((None, tb, width), lambda b, j: (b, j, 0))
    consts = [p["w_ob"], p["w_out"], p["norm_post"]]
    return pl.pallas_call(
        _back_kernel,
        grid=(B, S // tb),
        in_specs=[tok(D_MODEL), tok(MLA_WIDTH), tok(MLA_WIDTH), tok(D_MODEL), tok(D_MODEL)]
        + [_const_spec(c, l) for c in consts],
        out_specs=tok(D_MODEL),
        out_shape=jax.ShapeDtypeStruct((B, S, D_MODEL), f32),
        compiler_params=pltpu.CompilerParams(
            dimension_semantics=("arbitrary", "arbitrary"),
            vmem_limit_bytes=V7X_VMEM_BYTES * 3 // 4),
        name="layer_back",
    )(x, o, sb, g1, part, *consts)


def _stacked_params(norm_pre, norm_post, w_in, b_gate, pool_w, pool_scale, q_norm, w_uq,
                    kv_norm, w_ukv, v_norm_g, v_norm_b, w_s, b_s, w_oa, w_ob, w_oc, w_out):
    L = w_in.shape[0]
    zeros = lambda *shape: jnp.zeros(shape, bf16)
    kr_src = sum(IN_SPLITS[:4])
    assert kr_src == COL_KR
    w16 = lax.optimization_barrier(w_in.astype(bf16))
    w_head = w16[..., :kr_src]
    w_kr = jnp.concatenate([zeros(L, D_MODEL, ROPE_LANE0), w16[..., kr_src:kr_src + QK_ROPE_DIM],
                            zeros(L, D_MODEL, HEAD_PAD - ROPE_LANE0 - QK_ROPE_DIM)], axis=-1)
    w_tail = w16[..., kr_src + QK_ROPE_DIM:]
    assert w_tail.shape[-1] == IN_WIDTH_PAD - COL_B_GATE

    uq = w_uq.astype(bf16).reshape(L, Q_LORA_RANK, MLA_HEADS, QK_NOPE_DIM + QK_ROPE_DIM)
    nope, x1, x2 = uq[..., :QK_NOPE_DIM], uq[..., QK_NOPE_DIM:QK_NOPE_DIM + HALF_ROPE], uq[..., QK_NOPE_DIM + HALF_ROPE:]
    pad = zeros(L, Q_LORA_RANK, MLA_HEADS, HEAD_PAD - QK_NOPE_DIM - QK_ROPE_DIM)
    w_uq_p = jnp.concatenate([nope, x1, x2, pad], axis=-1).reshape(L, Q_LORA_RANK, MLA_HEADS * HEAD_PAD)

    assert 2 * QK_NOPE_DIM == HEAD_PAD and V_HEAD_DIM == QK_NOPE_DIM
    ukv = w_ukv.astype(bf16).reshape(L, KV_LORA_RANK, MLA_HEADS, QK_NOPE_DIM + V_HEAD_DIM)
    k_cols = ukv[..., :QK_NOPE_DIM].reshape(L, KV_LORA_RANK, MLA_HEADS * QK_NOPE_DIM)
    v_cols = ukv[..., QK_NOPE_DIM:].reshape(L, KV_LORA_RANK, MLA_HEADS * V_HEAD_DIM)
    w_ukv_p = jnp.concatenate([k_cols, v_cols], axis=-1)

    row = lambda a: a.reshape(L, 1, -1)
    return {
        "norm_pre": row(norm_pre), "norm_post": row(norm_post),
        "w_head": w_head, "w_kr": w_kr, "w_tail": w_tail,
        "b_gate": row(b_gate), "pool_w": pool_w.astype(bf16), "pool_scale": row(pool_scale),
        "q_norm": row(q_norm), "w_uq": w_uq_p, "kv_norm": row(kv_norm), "w_ukv": w_ukv_p,
        "v_norm_g": row(v_norm_g), "v_norm_b": row(v_norm_b), "w_s": w_s.astype(bf16),
        "bs_full": jnp.repeat(jnp.swapaxes(b_s, 1, 2), GMLP_GROUP_DIM, axis=2),
        "w_oa": w_oa.astype(bf16), "w_ob": w_ob.astype(bf16), "w_oc": w_oc.astype(bf16),
        "w_out": w_out.astype(bf16),
    }


def kernel(x, positions, norm_pre, norm_post, w_in, b_gate, pool_w, pool_scale, q_norm, w_uq, kv_norm, w_ukv, v_norm_g, v_norm_b, w_s, b_s, w_oa, w_ob, w_oc, w_out):
    B, S, _ = x.shape
    cos_t, sin_t = _rope_tables(positions)
    cos_t = cos_t.reshape(B, S, HEAD_PAD)
    sin_t = sin_t.reshape(B, S, HEAD_PAD)
    p = _stacked_params(norm_pre, norm_post, w_in, b_gate, pool_w, pool_scale, q_norm, w_uq,
                        kv_norm, w_ukv, v_norm_g, v_norm_b, w_s, b_s, w_oa, w_ob, w_oc, w_out)
    for l in range(norm_pre.shape[0]):
        q, k, v, part, g1, sb = _front(x, cos_t, sin_t, p, l)
        o = _attention(q, k, v)
        x = _back(x, o, sb, g1, part, p, l)
    return x
```

```python
import math

import jax
import jax.numpy as jnp
from jax import lax
from jax.experimental import pallas as pl
from jax.experimental.pallas import tpu as pltpu

D_MODEL = 1024
POOL_WIDTH = 512
POOL_GROUPS = 4
POOL_GROUP_DIM = POOL_WIDTH // POOL_GROUPS
POOL_WINDOWS = (2, 4, 8, 16)
MLA_HEADS = 8
QK_NOPE_DIM = 64
QK_ROPE_DIM = 32
HALF_ROPE = QK_ROPE_DIM // 2
V_HEAD_DIM = 64
Q_LORA_RANK = 256
KV_LORA_RANK = 128
MLA_WIDTH = MLA_HEADS * V_HEAD_DIM
ROPE_THETA = 10000.0
GMLP_WIDTH = 512
GMLP_CHUNK = 128
GMLP_GROUPS = 8
GMLP_GROUP_DIM = GMLP_WIDTH // GMLP_GROUPS
N_BRANCHES = 3
EPS = 1e-6
IN_SPLITS = (POOL_WIDTH, POOL_WIDTH, Q_LORA_RANK, KV_LORA_RANK, QK_ROPE_DIM, MLA_WIDTH,
             GMLP_WIDTH, GMLP_WIDTH, GMLP_WIDTH, N_BRANCHES * D_MODEL)

V7X_LANES = 128
V7X_VMEM_BYTES = 64 * 1024 * 1024

HEAD_PAD = V7X_LANES
ROPE_LANE0 = QK_NOPE_DIM
ONES_LANE = V_HEAD_DIM
HALO = max(POOL_WINDOWS)
V7X_SUBLANES = 8
POOL_PAD = V7X_SUBLANES
POOL_TOP = POOL_PAD + HALO
assert all(w & (w - 1) == 0 and w <= HALO for w in POOL_WINDOWS)
POOL_LEVELS = sum(w.bit_length() - 2 for w in POOL_WINDOWS)

COL_A_IN = 0
COL_A_GATE = COL_A_IN + POOL_WIDTH
COL_CQ = COL_A_GATE + POOL_WIDTH
COL_CKV = COL_CQ + Q_LORA_RANK
COL_KR = COL_CKV + KV_LORA_RANK
COL_B_GATE = COL_KR + HEAD_PAD
COL_U = COL_B_GATE + MLA_WIDTH
COL_V = COL_U + GMLP_WIDTH
COL_C_GATE = COL_V + GMLP_WIDTH
COL_GATES = COL_C_GATE + GMLP_WIDTH
IN_WIDTH_PAD = COL_GATES + N_BRANCHES * D_MODEL

SOFTMAX_SCALE = (QK_NOPE_DIM + QK_ROPE_DIM) ** -0.5
LOG2E = math.log2(math.e)

FRONT_BLOCK = 512
ATTN_BLOCK = 1024
BACK_BLOCK = 1024
ROPE_BLOCK = 1024
HEADS_PER_STEP = 2
SOFTMAX_ROWS = 64

bf16 = jnp.bfloat16
f32 = jnp.float32


def _sigmoid(z):
    return 0.5 * jnp.tanh(0.5 * z) + 0.5


def _silu(z):
    return z * _sigmoid(z)


def _rms(x, g):
    return x * lax.rsqrt(jnp.mean(x * x, axis=-1, keepdims=True) + EPS) * g


def _dot(a, b):
    return jnp.dot(a, b, preferred_element_type=f32)


def _rope_table_kernel(pos_ref, invf_ref, cos_ref, sin_ref):
    lane = lax.broadcasted_iota(jnp.int32, cos_ref.shape, 1)
    ang = pos_ref[...].astype(f32) * invf_ref[...]
    c = jnp.cos(ang)
    s = jnp.sin(ang)
    rope = (lane >= ROPE_LANE0) & (lane < ROPE_LANE0 + QK_ROPE_DIM)
    first_half = lane < ROPE_LANE0 + HALF_ROPE
    cos_ref[...] = jnp.where(lane < ROPE_LANE0, 1.0, jnp.where(rope, c, 0.0))
    sin_ref[...] = jnp.where(rope, jnp.where(first_half, -s, s), 0.0)


def _rope_tables(positions):
    n = positions.size
    inv_freq = ROPE_THETA ** (-jnp.arange(0, QK_ROPE_DIM, 2, dtype=f32) / QK_ROPE_DIM)
    pad = jnp.zeros((HEAD_PAD - ROPE_LANE0 - QK_ROPE_DIM,), f32)
    invf = jnp.concatenate([jnp.zeros((ROPE_LANE0,), f32), inv_freq, inv_freq, pad]).reshape(1, HEAD_PAD)
    out = jax.ShapeDtypeStruct((n, HEAD_PAD), f32)
    return pl.pallas_call(
        _rope_table_kernel,
        grid=(n // ROPE_BLOCK,),
        in_specs=[pl.BlockSpec((ROPE_BLOCK, 1), lambda i: (i, 0)),
                  pl.BlockSpec((1, HEAD_PAD), lambda i: (0, 0))],
        out_specs=[pl.BlockSpec((ROPE_BLOCK, HEAD_PAD), lambda i: (i, 0))] * 2,
        out_shape=[out, out],
        name="rope_tables",
    )(positions.reshape(n, 1), invf)


def _front_kernel(x_ref, cos_ref, sin_ref, norm_pre_ref, w_head_ref, w_kr_ref, w_tail_ref, b_gate_ref, pool_w_ref,
                  pool_scale_ref, q_norm_ref, w_uq_ref, kv_norm_ref, w_ukv_ref, vn_g_ref,
                  vn_b_ref, w_s_ref, bs_ref, w_oa_ref, w_oc_ref,
                  q_ref, k_ref, v_ref, part_ref, g1_ref, sb_ref, abuf, lv_ref):
    tb = x_ref.shape[0]
    j = pl.program_id(1)

    hb = _rms(x_ref[...], norm_pre_ref[...]).astype(bf16)

    def proj(c0, width):
        if c0 + width <= COL_KR:
            w = w_head_ref[:, c0:c0 + width]
        elif c0 == COL_KR and width == HEAD_PAD:
            w = w_kr_ref[...]
        else:
            assert c0 >= COL_B_GATE
            w = w_tail_ref[:, c0 - COL_B_GATE:c0 - COL_B_GATE + width]
        return _dot(hb, w)

    @pl.when(j == 0)
    def _():
        abuf[0:POOL_TOP, :] = jnp.zeros((POOL_TOP, POOL_WIDTH), f32)
        lv_ref[:, 0:POOL_PAD, :] = jnp.zeros((lv_ref.shape[0], POOL_PAD, POOL_GROUP_DIM), f32)

    a_in = proj(COL_A_IN, POOL_WIDTH)
    abuf[POOL_TOP:POOL_TOP + tb, :] = a_in
    cq = proj(COL_CQ, Q_LORA_RANK)
    ckv = proj(COL_CKV, KV_LORA_RANK)
    kr = proj(COL_KR, HEAD_PAD)
    vv = proj(COL_V, GMLP_WIDTH)
    a_gate = proj(COL_A_GATE, POOL_WIDTH)
    b_gate = proj(COL_B_GATE, MLA_WIDTH)
    u = proj(COL_U, GMLP_WIDTH)
    c_gate = proj(COL_C_GATE, GMLP_WIDTH)

    t = j * tb + lax.broadcasted_iota(jnp.int32, (tb, 1), 0)
    ys = []
    level = 0
    for g, w in enumerate(POOL_WINDOWS):
        cols = slice(g * POOL_GROUP_DIM, (g + 1) * POOL_GROUP_DIM)
        src = lambda lo, n: abuf[lo:lo + n, cols]
        shift = 1
        while 2 * shift < w:
            lo, n = POOL_PAD, HALO + tb
            dst = level
            lv_ref[dst, lo:lo + n, :] = src(lo, n) + src(lo - shift, n)
            src = lambda lo, n, dst=dst: lv_ref[dst, lo:lo + n, :]
            level += 1
            shift *= 2
        win = src(POOL_TOP, tb) + src(POOL_TOP - shift, tb)
        count = jnp.minimum(t + 1, w).astype(f32)
        mixed = win / count - a_in[:, cols]
        ys.append(_dot(mixed.astype(bf16), pool_w_ref[g]))
    abuf[POOL_PAD:POOL_TOP, :] = abuf[tb + POOL_PAD:tb + POOL_TOP, :]

    cqn = _rms(cq, q_norm_ref[...]).astype(bf16)
    q2 = _dot(cqn, w_uq_ref[...])
    ckvn = _rms(ckv, kv_norm_ref[...]).astype(bf16)
    kv2 = _dot(ckvn, w_ukv_ref[...])

    vc = vv - jnp.mean(vv, axis=-1, keepdims=True)
    var = jnp.mean(vc * vc, axis=-1, keepdims=True)
    vb = (vc * lax.rsqrt(var + EPS) * vn_g_ref[...] + vn_b_ref[...]).astype(bf16)
    row = lax.broadcasted_iota(jnp.int32, (GMLP_CHUNK, GMLP_CHUNK), 0)
    col = lax.broadcasted_iota(jnp.int32, (GMLP_CHUNK, GMLP_CHUNK), 1)
    ws = [jnp.where(col <= row, w_s_ref[g], jnp.zeros((), bf16)) for g in range(GMLP_GROUPS)]
    groups_per_tile = V7X_LANES // GMLP_GROUP_DIM
    assert groups_per_tile == 2
    chunks = []
    for c in range(tb // GMLP_CHUNK):
        tiles = []
        for tile in range(GMLP_WIDTH // V7X_LANES):
            rhs = vb[c * GMLP_CHUNK:(c + 1) * GMLP_CHUNK, tile * V7X_LANES:(tile + 1) * V7X_LANES]
            lo = _dot(ws[2 * tile], rhs)
            hi = _dot(ws[2 * tile + 1], rhs)
            tiles.append(jnp.where(col < GMLP_GROUP_DIM, lo, hi))
        chunks.append(jnp.concatenate(tiles, axis=1) + bs_ref[...])
    sv = jnp.concatenate(chunks, axis=0)

    gate_pre = [proj(COL_GATES + i * D_MODEL, D_MODEL) for i in range(N_BRANCHES)]

    cos_t = cos_ref[...]
    sin_t = sin_ref[...]
    lane = lax.broadcasted_iota(jnp.int32, (tb, HEAD_PAD), 1)
    first_half = lane < ROPE_LANE0 + HALF_ROPE

    def rotate(t):
        swapped = jnp.where(first_half, pltpu.roll(t, HEAD_PAD - HALF_ROPE, 1), pltpu.roll(t, HALF_ROPE, 1))
        return t * cos_t + swapped * sin_t

    for h in range(MLA_HEADS):
        q_ref[h] = (rotate(q2[:, h * HEAD_PAD:(h + 1) * HEAD_PAD]) * (SOFTMAX_SCALE * LOG2E)).astype(bf16)
    k_rope = rotate(kr)
    low = lane < QK_NOPE_DIM
    ones_lane = jnp.where(lane == ONES_LANE, 1.0, 0.0)
    v0 = MLA_HEADS * QK_NOPE_DIM
    for pair in range(MLA_HEADS // 2):
        k_pair = kv2[:, pair * HEAD_PAD:(pair + 1) * HEAD_PAD]
        v_pair = kv2[:, v0 + pair * HEAD_PAD:v0 + (pair + 1) * HEAD_PAD]
        for odd in range(2):
            k_h = pltpu.roll(k_pair, QK_NOPE_DIM, 1) if odd else k_pair
            v_h = pltpu.roll(v_pair, V_HEAD_DIM, 1) if odd else v_pair
            k_ref[2 * pair + odd] = (jnp.where(low, k_h, 0.0) + k_rope).astype(bf16)
            v_ref[2 * pair + odd] = jnp.where(low, v_h, ones_lane).astype(bf16)
    sb_ref[...] = _silu(b_gate).astype(sb_ref.dtype)

    y_a = jnp.concatenate(ys, axis=1) * pool_scale_ref[...] * _silu(a_gate)
    y_c = u * sv * _silu(c_gate)

    def gate(i):
        return _sigmoid(gate_pre[i] + b_gate_ref[:, i * D_MODEL:(i + 1) * D_MODEL])

    g1_ref[...] = gate(1).astype(g1_ref.dtype)
    part_ref[...] = (gate(0) * _dot(y_a.astype(bf16), w_oa_ref[...])
                     + gate(2) * _dot(y_c.astype(bf16), w_oc_ref[...])).astype(part_ref.dtype)


def _const_spec(stacked, l):
    index = (l,) + (0,) * (stacked.ndim - 1)
    return pl.BlockSpec((None,) + stacked.shape[1:], lambda b, j: index, pipeline_mode=pl.Buffered(1))


def _front(x, cos_t, sin_t, p, l):
    B, S, _ = x.shape
    tb = FRONT_BLOCK
    tok = lambda width: pl.BlockSpec((None, tb, width), lambda b, j: (b, j, 0))
    heads = pl.BlockSpec((None, MLA_HEADS, tb, HEAD_PAD), lambda b, j: (b, 0, j, 0))
    consts = [p["norm_pre"], p["w_head"], p["w_kr"], p["w_tail"], p["b_gate"], p["pool_w"],
              p["pool_scale"], p["q_norm"],
              p["w_uq"], p["kv_norm"], p["w_ukv"], p["v_norm_g"], p["v_norm_b"], p["w_s"],
              p["bs_full"], p["w_oa"], p["w_oc"]]
    head_shape = jax.ShapeDtypeStruct((B, MLA_HEADS, S, HEAD_PAD), bf16)
    return pl.pallas_call(
        _front_kernel,
        grid=(B, S // tb),
        in_specs=[tok(D_MODEL), tok(HEAD_PAD), tok(HEAD_PAD)] + [_const_spec(c, l) for c in consts],
        out_specs=[heads, heads, heads, tok(D_MODEL), tok(D_MODEL), tok(MLA_WIDTH)],
        out_shape=[head_shape, head_shape, head_shape,
                   jax.ShapeDtypeStruct((B, S, D_MODEL), bf16),
                   jax.ShapeDtypeStruct((B, S, D_MODEL), bf16),
                   jax.ShapeDtypeStruct((B, S, MLA_WIDTH), bf16)],
        scratch_shapes=[pltpu.VMEM((POOL_TOP + tb, POOL_WIDTH), f32),
                        pltpu.VMEM((POOL_LEVELS, POOL_TOP + tb, POOL_GROUP_DIM), f32)],
        compiler_params=pltpu.CompilerParams(
            dimension_semantics=("arbitrary", "arbitrary"),
            vmem_limit_bytes=V7X_VMEM_BYTES * 7 // 8),
        name="layer_front",
    )(x, cos_t, sin_t, *consts)


def _attn_kernel(q_ref, k_ref, v_ref, o_ref, s_ref, p_ref, m_ref, acc_ref):
    hp, tq, _ = q_ref.shape
    i = pl.program_id(2)
    lane = lax.broadcasted_iota(jnp.int32, (tq, HEAD_PAD), 1)
    n_tiles = tq // V7X_LANES
    half = tq // 2

    m_ref[...] = jnp.full(m_ref.shape, -jnp.inf, f32)
    acc_ref[...] = jnp.zeros(acc_ref.shape, f32)

    nt_dims = (((1,), (1,)), ((), ()))

    def scores(hh, j, slot, diagonal=False):
        start = pl.multiple_of(j * tq, tq)
        if diagonal:
            s_ref[slot, hh, 0:half, 0:half] = lax.dot_general(
                q_ref[hh, 0:half], k_ref[hh, pl.ds(start, half), :], nt_dims, preferred_element_type=f32)
            s_ref[slot, hh, half:tq, :] = lax.dot_general(
                q_ref[hh, half:tq], k_ref[hh, pl.ds(start, tq), :], nt_dims, preferred_element_type=f32)
        else:
            s_ref[slot, hh] = lax.dot_general(q_ref[hh], k_ref[hh, pl.ds(start, tq), :], nt_dims,
                                              preferred_element_type=f32)

    def softmax_pv(hh, j, slot, diagonal):
        for r in range(tq // SOFTMAX_ROWS):
            r0 = r * SOFTMAX_ROWS
            rows = slice(r0, r0 + SOFTMAX_ROWS)
            live = (r0 + SOFTMAX_ROWS - 1) // V7X_LANES + 1 if diagonal else n_tiles
            read = (half if r0 < half else tq) // V7X_LANES if diagonal else n_tiles

            def tile(c):
                s = s_ref[slot, hh, rows, c * V7X_LANES:(c + 1) * V7X_LANES]
                if diagonal and (c + 1) * V7X_LANES - 1 > r0:
                    row = r0 + lax.broadcasted_iota(jnp.int32, s.shape, 0)
                    col = c * V7X_LANES + lax.broadcasted_iota(jnp.int32, s.shape, 1)
                    s = jnp.where(col <= row, s, -jnp.inf)
                return s

            m_blk = tile(0)
            for c in range(1, live):
                m_blk = jnp.maximum(m_blk, tile(c))
            m_old = m_ref[hh, rows]
            m_new = jnp.maximum(m_old, jnp.max(m_blk, axis=-1, keepdims=True))
            m_ref[hh, rows] = m_new
            acc_ref[hh, rows] = jnp.exp2(m_old - m_new) * acc_ref[hh, rows]
            for c in range(live):
                p_ref[hh, rows, c * V7X_LANES:(c + 1) * V7X_LANES] = jnp.exp2(tile(c) - m_new).astype(bf16)
            if live < read:
                p_ref[hh, rows, live * V7X_LANES:read * V7X_LANES] = jnp.zeros(
                    (SOFTMAX_ROWS, (read - live) * V7X_LANES), bf16)
        start = pl.multiple_of(j * tq, tq)
        if diagonal:
            acc_ref[hh, 0:half] += _dot(p_ref[hh, 0:half, 0:half], v_ref[hh, pl.ds(start, half), :])
            acc_ref[hh, half:tq] += _dot(p_ref[hh, half:tq, :], v_ref[hh, pl.ds(start, tq), :])
        else:
            acc_ref[hh] += _dot(p_ref[hh], v_ref[hh, pl.ds(start, tq), :])

    def step(j, slot, next_diagonal=False):
        for hh in range(hp):
            softmax_pv(hh, j, slot, False)
            scores(hh, j + 1, 1 - slot, next_diagonal)

    def last_step(slot):
        for hh in range(hp):
            softmax_pv(hh, i, slot, True)

    @pl.when(i == 0)
    def _():
        for hh in range(hp):
            scores(hh, 0, 0, True)
        last_step(0)

    @pl.when(i > 0)
    def _():
        for hh in range(hp):
            scores(hh, 0, 0)

    def body(t, carry):
        step(2 * t, 0)
        step(2 * t + 1, 1)
        return carry

    lax.fori_loop(0, jnp.maximum(i - 1, 0) // 2, body, 0)

    @pl.when(i % 2 == 1)
    def _():
        step(i - 1, 0, True)
        last_step(1)

    @pl.when((i % 2 == 0) & (i > 0))
    def _():
        step(i - 2, 0)
        step(i - 1, 1, True)
        last_step(0)

    outs = [acc_ref[hh] / acc_ref[hh, :, ONES_LANE:ONES_LANE + 1] for hh in range(hp)]
    o_ref[...] = jnp.where(lane < V_HEAD_DIM, outs[0],
                           pltpu.roll(outs[1], V_HEAD_DIM, 1)).astype(o_ref.dtype)


def _attention(q, k, v):
    B, H, S, _ = q.shape
    tq = ATTN_BLOCK
    hp = HEADS_PER_STEP
    assert hp * V_HEAD_DIM == V7X_LANES
    return pl.pallas_call(
        _attn_kernel,
        grid=(B, H // hp, S // tq),
        in_specs=[pl.BlockSpec((None, hp, tq, HEAD_PAD), lambda b, g, i: (b, g, i, 0)),
                  pl.BlockSpec((None, hp, S, HEAD_PAD), lambda b, g, i: (b, g, 0, 0)),
                  pl.BlockSpec((None, hp, S, HEAD_PAD), lambda b, g, i: (b, g, 0, 0))],
        out_specs=pl.BlockSpec((None, tq, hp * V_HEAD_DIM), lambda b, g, i: (b, i, g)),
        out_shape=jax.ShapeDtypeStruct((B, S, H * V_HEAD_DIM), bf16),
        scratch_shapes=[pltpu.VMEM((2, hp, tq, tq), f32),
                        pltpu.VMEM((hp, tq, tq), bf16),
                        pltpu.VMEM((hp, tq, V7X_LANES), f32),
                        pltpu.VMEM((hp, tq, HEAD_PAD), f32)],
        compiler_params=pltpu.CompilerParams(
            dimension_semantics=("arbitrary", "arbitrary", "arbitrary"),
            vmem_limit_bytes=V7X_VMEM_BYTES * 7 // 8),
        name="mla_attention",
    )(q, k, v)


def _back_kernel(x_ref, o_ref, sb_ref, g1_ref, part_ref, w_ob_ref, w_out_ref, norm_post_ref,
                 out_ref):
    y_b = o_ref[...] * sb_ref[...]
    merged = part_ref[...].astype(f32) + g1_ref[...].astype(f32) * _dot(y_b, w_ob_ref[...])
    z = _dot(merged.astype(bf16), w_out_ref[...])
    out_ref[...] = x_ref[...] + _rms(z, norm_post_ref[...])


def _back(x, o, sb, g1, part, p, l):
    B, S, _ = x.shape
    tb = BACK_BLOCK
    tok = lambda width: pl.BlockSpec((None, tb, width), lambda b, j: (b, j, 0))
    consts = [p["w_ob"], p["w_out"], p["norm_post"]]
    return pl.pallas_call(
        _back_kernel,
        grid=(B, S // tb),
        in_specs=[tok(D_MODEL), tok(MLA_WIDTH), tok(MLA_WIDTH), tok(D_MODEL), tok(D_MODEL)]
        + [_const_spec(c, l) for c in consts],
        out_specs=tok(D_MODEL),
        out_shape=jax.ShapeDtypeStruct((B, S, D_MODEL), f32),
        compiler_params=pltpu.CompilerParams(
            dimension_semantics=("arbitrary", "arbitrary"),
            vmem_limit_bytes=V7X_VMEM_BYTES * 3 // 4),
        name="layer_back",
    )(x, o, sb, g1, part, *consts)


def _stacked_params(norm_pre, norm_post, w_in, b_gate, pool_w, pool_scale, q_norm, w_uq,
                    kv_norm, w_ukv, v_norm_g, v_norm_b, w_s, b_s, w_oa, w_ob, w_oc, w_out):
    L = w_in.shape[0]
    zeros = lambda *shape: jnp.zeros(shape, bf16)
    kr_src = sum(IN_SPLITS[:4])
    assert kr_src == COL_KR
    w_head = w_in[..., :kr_src].astype(bf16)
    w_kr = jnp.concatenate([zeros(L, D_MODEL, ROPE_LANE0),
                            w_in[..., kr_src:kr_src + QK_ROPE_DIM].astype(bf16),
                            zeros(L, D_MODEL, HEAD_PAD - ROPE_LANE0 - QK_ROPE_DIM)], axis=-1)
    w_tail = w_in[..., kr_src + QK_ROPE_DIM:].astype(bf16)
    assert w_tail.shape[-1] == IN_WIDTH_PAD - COL_B_GATE

    uq = w_uq.astype(bf16).reshape(L, Q_LORA_RANK, MLA_HEADS, QK_NOPE_DIM + QK_ROPE_DIM)
    nope, x1, x2 = uq[..., :QK_NOPE_DIM], uq[..., QK_NOPE_DIM:QK_NOPE_DIM + HALF_ROPE], uq[..., QK_NOPE_DIM + HALF_ROPE:]
    pad = zeros(L, Q_LORA_RANK, MLA_HEADS, HEAD_PAD - QK_NOPE_DIM - QK_ROPE_DIM)
    w_uq_p = jnp.concatenate([nope, x1, x2, pad], axis=-1).reshape(L, Q_LORA_RANK, MLA_HEADS * HEAD_PAD)

    assert 2 * QK_NOPE_DIM == HEAD_PAD and V_HEAD_DIM == QK_NOPE_DIM
    ukv = w_ukv.astype(bf16).reshape(L, KV_LORA_RANK, MLA_HEADS, QK_NOPE_DIM + V_HEAD_DIM)
    k_cols = ukv[..., :QK_NOPE_DIM].reshape(L, KV_LORA_RANK, MLA_HEADS * QK_NOPE_DIM)
    v_cols = ukv[..., QK_NOPE_DIM:].reshape(L, KV_LORA_RANK, MLA_HEADS * V_HEAD_DIM)
    w_ukv_p = jnp.concatenate([k_cols, v_cols], axis=-1)

    row = lambda a: a.reshape(L, 1, -1)
    return {
        "norm_pre": row(norm_pre), "norm_post": row(norm_post),
        "w_head": w_head, "w_kr": w_kr, "w_tail": w_tail,
        "b_gate": row(b_gate), "pool_w": pool_w.astype(bf16), "pool_scale": row(pool_scale),
        "q_norm": row(q_norm), "w_uq": w_uq_p, "kv_norm": row(kv_norm), "w_ukv": w_ukv_p,
        "v_norm_g": row(v_norm_g), "v_norm_b": row(v_norm_b), "w_s": w_s.astype(bf16),
        "bs_full": jnp.repeat(jnp.swapaxes(b_s, 1, 2), GMLP_GROUP_DIM, axis=2),
        "w_oa": w_oa.astype(bf16), "w_ob": w_ob.astype(bf16), "w_oc": w_oc.astype(bf16),
        "w_out": w_out.astype(bf16),
    }


def kernel(x, positions, norm_pre, norm_post, w_in, b_gate, pool_w, pool_scale, q_norm, w_uq, kv_norm, w_ukv, v_norm_g, v_norm_b, w_s, b_s, w_oa, w_ob, w_oc, w_out):
    B, S, _ = x.shape
    cos_t, sin_t = _rope_tables(positions)
    cos_t = cos_t.reshape(B, S, HEAD_PAD)
    sin_t = sin_t.reshape(B, S, HEAD_PAD)
    p = _stacked_params(norm_pre, norm_post, w_in, b_gate, pool_w, pool_scale, q_norm, w_uq,
                        kv_norm, w_ukv, v_norm_g, v_norm_b, w_s, b_s, w_oa, w_ob, w_oc, w_out)
    for l in range(norm_pre.shape[0]):
        q, k, v, part, g1, sb = _front(x, cos_t, sin_t, p, l)
        o = _attention(q, k, v)
        x = _back(x, o, sb, g1, part, p, l)
    return x
```

```python
import math

import jax
import jax.numpy as jnp
from jax import lax
from jax.experimental import pallas as pl
from jax.experimental.pallas import tpu as pltpu

D_MODEL = 1024
POOL_WIDTH = 512
POOL_GROUPS = 4
POOL_GROUP_DIM = POOL_WIDTH // POOL_GROUPS
POOL_WINDOWS = (2, 4, 8, 16)
MLA_HEADS = 8
QK_NOPE_DIM = 64
QK_ROPE_DIM = 32
HALF_ROPE = QK_ROPE_DIM // 2
V_HEAD_DIM = 64
Q_LORA_RANK = 256
KV_LORA_RANK = 128
MLA_WIDTH = MLA_HEADS * V_HEAD_DIM
ROPE_THETA = 10000.0
GMLP_WIDTH = 512
GMLP_CHUNK = 128
GMLP_GROUPS = 8
GMLP_GROUP_DIM = GMLP_WIDTH // GMLP_GROUPS
N_BRANCHES = 3
EPS = 1e-6
IN_SPLITS = (POOL_WIDTH, POOL_WIDTH, Q_LORA_RANK, KV_LORA_RANK, QK_ROPE_DIM, MLA_WIDTH,
             GMLP_WIDTH, GMLP_WIDTH, GMLP_WIDTH, N_BRANCHES * D_MODEL)

V7X_LANES = 128
V7X_VMEM_BYTES = 64 * 1024 * 1024

HEAD_PAD = V7X_LANES
ROPE_LANE0 = QK_NOPE_DIM
ONES_LANE = V_HEAD_DIM
HALO = max(POOL_WINDOWS)
V7X_SUBLANES = 8
POOL_PAD = V7X_SUBLANES
POOL_TOP = POOL_PAD + HALO
assert all(w & (w - 1) == 0 and w <= HALO for w in POOL_WINDOWS)
POOL_LEVELS = sum(w.bit_length() - 2 for w in POOL_WINDOWS)

COL_A_IN = 0
COL_A_GATE = COL_A_IN + POOL_WIDTH
COL_CQ = COL_A_GATE + POOL_WIDTH
COL_CKV = COL_CQ + Q_LORA_RANK
COL_KR = COL_CKV + KV_LORA_RANK
COL_B_GATE = COL_KR + HEAD_PAD
COL_U = COL_B_GATE + MLA_WIDTH
COL_V = COL_U + GMLP_WIDTH
COL_C_GATE = COL_V + GMLP_WIDTH
COL_GATES = COL_C_GATE + GMLP_WIDTH
IN_WIDTH_PAD = COL_GATES + N_BRANCHES * D_MODEL

SOFTMAX_SCALE = (QK_NOPE_DIM + QK_ROPE_DIM) ** -0.5
LOG2E = math.log2(math.e)

FRONT_BLOCK = 512
ATTN_BLOCK = 1024
BACK_BLOCK = 1024
ROPE_BLOCK = 1024
HEADS_PER_STEP = 2
SOFTMAX_ROWS = 64

bf16 = jnp.bfloat16
f32 = jnp.float32


def _sigmoid(z):
    return 0.5 * jnp.tanh(0.5 * z) + 0.5


def _silu(z):
    return z * _sigmoid(z)


def _rms(x, g):
    return x * lax.rsqrt(jnp.mean(x * x, axis=-1, keepdims=True) + EPS) * g


def _dot(a, b):
    return jnp.dot(a, b, preferred_element_type=f32)


def _rope_table_kernel(pos_ref, invf_ref, cos_ref, sin_ref):
    lane = lax.broadcasted_iota(jnp.int32, cos_ref.shape, 1)
    ang = pos_ref[...].astype(f32) * invf_ref[...]
    c = jnp.cos(ang)
    s = jnp.sin(ang)
    rope = (lane >= ROPE_LANE0) & (lane < ROPE_LANE0 + QK_ROPE_DIM)
    first_half = lane < ROPE_LANE0 + HALF_ROPE
    cos_ref[...] = jnp.where(lane < ROPE_LANE0, 1.0, jnp.where(rope, c, 0.0))
    sin_ref[...] = jnp.where(rope, jnp.where(first_half, -s, s), 0.0)


def _rope_tables(positions):
    n = positions.size
    inv_freq = ROPE_THETA ** (-jnp.arange(0, QK_ROPE_DIM, 2, dtype=f32) / QK_ROPE_DIM)
    pad = jnp.zeros((HEAD_PAD - ROPE_LANE0 - QK_ROPE_DIM,), f32)
    invf = jnp.concatenate([jnp.zeros((ROPE_LANE0,), f32), inv_freq, inv_freq, pad]).reshape(1, HEAD_PAD)
    out = jax.ShapeDtypeStruct((n, HEAD_PAD), f32)
    return pl.pallas_call(
        _rope_table_kernel,
        grid=(n // ROPE_BLOCK,),
        in_specs=[pl.BlockSpec((ROPE_BLOCK, 1), lambda i: (i, 0)),
                  pl.BlockSpec((1, HEAD_PAD), lambda i: (0, 0))],
        out_specs=[pl.BlockSpec((ROPE_BLOCK, HEAD_PAD), lambda i: (i, 0))] * 2,
        out_shape=[out, out],
        name="rope_tables",
    )(positions.reshape(n, 1), invf)


def _front_kernel(x_ref, cos_ref, sin_ref, norm_pre_ref, w_head_ref, w_kr_ref, w_tail_ref, b_gate_ref, pool_w_ref,
                  pool_scale_ref, q_norm_ref, w_uq_ref, kv_norm_ref, w_ukv_ref, vn_g_ref,
                  vn_b_ref, w_s_ref, bs_ref, w_oa_ref, w_oc_ref,
                  q_ref, k_ref, v_ref, part_ref, g1_ref, sb_ref, abuf, lv_ref):
    tb = x_ref.shape[0]
    j = pl.program_id(1)

    hb = _rms(x_ref[...], norm_pre_ref[...]).astype(bf16)

    def proj(c0, width):
        if c0 + width <= COL_KR:
            w = w_head_ref[:, c0:c0 + width]
        elif c0 == COL_KR and width == HEAD_PAD:
            w = w_kr_ref[...]
        else:
            assert c0 >= COL_B_GATE
            w = w_tail_ref[:, c0 - COL_B_GATE:c0 - COL_B_GATE + width]
        return _dot(hb, w)

    @pl.when(j == 0)
    def _():
        abuf[0:POOL_TOP, :] = jnp.zeros((POOL_TOP, POOL_WIDTH), f32)
        lv_ref[:, 0:POOL_PAD, :] = jnp.zeros((lv_ref.shape[0], POOL_PAD, POOL_GROUP_DIM), f32)

    a_in = proj(COL_A_IN, POOL_WIDTH)
    abuf[POOL_TOP:POOL_TOP + tb, :] = a_in
    cq = proj(COL_CQ, Q_LORA_RANK)
    ckv = proj(COL_CKV, KV_LORA_RANK)
    kr = proj(COL_KR, HEAD_PAD)
    vv = proj(COL_V, GMLP_WIDTH)
    a_gate = proj(COL_A_GATE, POOL_WIDTH)
    b_gate = proj(COL_B_GATE, MLA_WIDTH)
    u = proj(COL_U, GMLP_WIDTH)
    c_gate = proj(COL_C_GATE, GMLP_WIDTH)

    t = j * tb + lax.broadcasted_iota(jnp.int32, (tb, 1), 0)
    ys = []
    level = 0
    for g, w in enumerate(POOL_WINDOWS):
        cols = slice(g * POOL_GROUP_DIM, (g + 1) * POOL_GROUP_DIM)
        src = lambda lo, n: abuf[lo:lo + n, cols]
        shift = 1
        while 2 * shift < w:
            lo, n = POOL_PAD, HALO + tb
            dst = level
            lv_ref[dst, lo:lo + n, :] = src(lo, n) + src(lo - shift, n)
            src = lambda lo, n, dst=dst: lv_ref[dst, lo:lo + n, :]
            level += 1
            shift *= 2
        win = src(POOL_TOP, tb) + src(POOL_TOP - shift, tb)
        count = jnp.minimum(t + 1, w).astype(f32)
        mixed = win / count - a_in[:, cols]
        ys.append(_dot(mixed.astype(bf16), pool_w_ref[g]))
    abuf[POOL_PAD:POOL_TOP, :] = abuf[tb + POOL_PAD:tb + POOL_TOP, :]

    cqn = _rms(cq, q_norm_ref[...]).astype(bf16)
    q2 = _dot(cqn, w_uq_ref[...])
    ckvn = _rms(ckv, kv_norm_ref[...]).astype(bf16)
    kv2 = _dot(ckvn, w_ukv_ref[...])

    vc = vv - jnp.mean(vv, axis=-1, keepdims=True)
    var = jnp.mean(vc * vc, axis=-1, keepdims=True)
    vb = (vc * lax.rsqrt(var + EPS) * vn_g_ref[...] + vn_b_ref[...]).astype(bf16)
    row = lax.broadcasted_iota(jnp.int32, (GMLP_CHUNK, GMLP_CHUNK), 0)
    col = lax.broadcasted_iota(jnp.int32, (GMLP_CHUNK, GMLP_CHUNK), 1)
    ws = [jnp.where(col <= row, w_s_ref[g], jnp.zeros((), bf16)) for g in range(GMLP_GROUPS)]
    groups_per_tile = V7X_LANES // GMLP_GROUP_DIM
    assert groups_per_tile == 2
    chunks = []
    for c in range(tb // GMLP_CHUNK):
        tiles = []
        for tile in range(GMLP_WIDTH // V7X_LANES):
            rhs = vb[c * GMLP_CHUNK:(c + 1) * GMLP_CHUNK, tile * V7X_LANES:(tile + 1) * V7X_LANES]
            lo = _dot(ws[2 * tile], rhs)
            hi = _dot(ws[2 * tile + 1], rhs)
            tiles.append(jnp.where(col < GMLP_GROUP_DIM, lo, hi))
        chunks.append(jnp.concatenate(tiles, axis=1) + bs_ref[...])
    sv = jnp.concatenate(chunks, axis=0)

    gate_pre = {i: proj(COL_GATES + i * D_MODEL, D_MODEL) for i in (0, 2)}

    cos_t = cos_ref[...]
    sin_t = sin_ref[...]
    lane = lax.broadcasted_iota(jnp.int32, (tb, HEAD_PAD), 1)
    first_half = lane < ROPE_LANE0 + HALF_ROPE

    def rotate(t):
        swapped = jnp.where(first_half, pltpu.roll(t, HEAD_PAD - HALF_ROPE, 1), pltpu.roll(t, HALF_ROPE, 1))
        return t * cos_t + swapped * sin_t

    for h in range(MLA_HEADS):
        q_ref[h] = (rotate(q2[:, h * HEAD_PAD:(h + 1) * HEAD_PAD]) * (SOFTMAX_SCALE * LOG2E)).astype(bf16)
    k_rope = rotate(kr)
    low = lane < QK_NOPE_DIM
    ones_lane = jnp.where(lane == ONES_LANE, 1.0, 0.0)
    v0 = MLA_HEADS * QK_NOPE_DIM
    for pair in range(MLA_HEADS // 2):
        k_pair = kv2[:, pair * HEAD_PAD:(pair + 1) * HEAD_PAD]
        v_pair = kv2[:, v0 + pair * HEAD_PAD:v0 + (pair + 1) * HEAD_PAD]
        for odd in range(2):
            k_h = pltpu.roll(k_pair, QK_NOPE_DIM, 1) if odd else k_pair
            v_h = pltpu.roll(v_pair, V_HEAD_DIM, 1) if odd else v_pair
            k_ref[2 * pair + odd] = (jnp.where(low, k_h, 0.0) + k_rope).astype(bf16)
            v_ref[2 * pair + odd] = jnp.where(low, v_h, ones_lane).astype(bf16)
    sb_ref[...] = _silu(b_gate).astype(sb_ref.dtype)

    y_a = jnp.concatenate(ys, axis=1) * pool_scale_ref[...] * _silu(a_gate)
    y_c = u * sv * _silu(c_gate)

    def gate(i):
        return _sigmoid(gate_pre[i] + b_gate_ref[:, i * D_MODEL:(i + 1) * D_MODEL])

    part_ref[...] = (gate(0) * _dot(y_a.astype(bf16), w_oa_ref[...])
                     + gate(2) * _dot(y_c.astype(bf16), w_oc_ref[...])).astype(part_ref.dtype)
    gate_pre[1] = proj(COL_GATES + D_MODEL, D_MODEL)
    g1_ref[...] = gate(1).astype(g1_ref.dtype)


def _const_spec(stacked, l):
    index = (l,) + (0,) * (stacked.ndim - 1)
    return pl.BlockSpec((None,) + stacked.shape[1:], lambda b, j: index, pipeline_mode=pl.Buffered(1))


def _front(x, cos_t, sin_t, p, l):
    B, S, _ = x.shape
    tb = FRONT_BLOCK
    tok = lambda width: pl.BlockSpec((None, tb, width), lambda b, j: (b, j, 0))
    heads = pl.BlockSpec((None, MLA_HEADS, tb, HEAD_PAD), lambda b, j: (b, 0, j, 0))
    consts = [p["norm_pre"], p["w_head"], p["w_kr"], p["w_tail"], p["b_gate"], p["pool_w"],
              p["pool_scale"], p["q_norm"],
              p["w_uq"], p["kv_norm"], p["w_ukv"], p["v_norm_g"], p["v_norm_b"], p["w_s"],
              p["bs_full"], p["w_oa"], p["w_oc"]]
    head_shape = jax.ShapeDtypeStruct((B, MLA_HEADS, S, HEAD_PAD), bf16)
    return pl.pallas_call(
        _front_kernel,
        grid=(B, S // tb),
        in_specs=[tok(D_MODEL), tok(HEAD_PAD), tok(HEAD_PAD)] + [_const_spec(c, l) for c in consts],
        out_specs=[heads, heads, heads, tok(D_MODEL), tok(D_MODEL), tok(MLA_WIDTH)],
        out_shape=[head_shape, head_shape, head_shape,
                   jax.ShapeDtypeStruct((B, S, D_MODEL), bf16),
                   jax.ShapeDtypeStruct((B, S, D_MODEL), bf16),
                   jax.ShapeDtypeStruct((B, S, MLA_WIDTH), bf16)],
        scratch_shapes=[pltpu.VMEM((POOL_TOP + tb, POOL_WIDTH), f32),
                        pltpu.VMEM((POOL_LEVELS, POOL_TOP + tb, POOL_GROUP_DIM), f32)],
        compiler_params=pltpu.CompilerParams(
            dimension_semantics=("arbitrary", "arbitrary"),
            vmem_limit_bytes=V7X_VMEM_BYTES * 7 // 8),
        name="layer_front",
    )(x, cos_t, sin_t, *consts)


def _attn_kernel(q_ref, k_ref, v_ref, o_ref, s_ref, p_ref, m_ref, acc_ref):
    hp, tq, _ = q_ref.shape
    i = pl.program_id(2)
    lane = lax.broadcasted_iota(jnp.int32, (tq, HEAD_PAD), 1)
    n_tiles = tq // V7X_LANES
    half = tq // 2

    m_ref[...] = jnp.full(m_ref.shape, -jnp.inf, f32)
    acc_ref[...] = jnp.zeros(acc_ref.shape, f32)

    nt_dims = (((1,), (1,)), ((), ()))

    def scores(hh, j, slot, diagonal=False):
        start = pl.multiple_of(j * tq, tq)
        if diagonal:
            s_ref[slot, hh, 0:half, 0:half] = lax.dot_general(
                q_ref[hh, 0:half], k_ref[hh, pl.ds(start, half), :], nt_dims, preferred_element_type=f32)
            s_ref[slot, hh, half:tq, :] = lax.dot_general(
                q_ref[hh, half:tq], k_ref[hh, pl.ds(start, tq), :], nt_dims, preferred_element_type=f32)
        else:
            s_ref[slot, hh] = lax.dot_general(q_ref[hh], k_ref[hh, pl.ds(start, tq), :], nt_dims,
                                              preferred_element_type=f32)

    def softmax_pv(hh, j, slot, diagonal):
        for r in range(tq // SOFTMAX_ROWS):
            r0 = r * SOFTMAX_ROWS
            rows = slice(r0, r0 + SOFTMAX_ROWS)
            live = (r0 + SOFTMAX_ROWS - 1) // V7X_LANES + 1 if diagonal else n_tiles
            read = (half if r0 < half else tq) // V7X_LANES if diagonal else n_tiles

            def tile(c):
                s = s_ref[slot, hh, rows, c * V7X_LANES:(c + 1) * V7X_LANES]
                if diagonal and (c + 1) * V7X_LANES - 1 > r0:
                    row = r0 + lax.broadcasted_iota(jnp.int32, s.shape, 0)
                    col = c * V7X_LANES + lax.broadcasted_iota(jnp.int32, s.shape, 1)
                    s = jnp.where(col <= row, s, -jnp.inf)
                return s

            m_blk = tile(0)
            for c in range(1, live):
                m_blk = jnp.maximum(m_blk, tile(c))
            m_old = m_ref[hh, rows]
            m_new = jnp.maximum(m_old, jnp.max(m_blk, axis=-1, keepdims=True))
            m_ref[hh, rows] = m_new
            acc_ref[hh, rows] = jnp.exp2(m_old - m_new) * acc_ref[hh, rows]
            for c in range(live):
                p_ref[hh, rows, c * V7X_LANES:(c + 1) * V7X_LANES] = jnp.exp2(tile(c) - m_new).astype(bf16)
            if live < read:
                p_ref[hh, rows, live * V7X_LANES:read * V7X_LANES] = jnp.zeros(
                    (SOFTMAX_ROWS, (read - live) * V7X_LANES), bf16)
        start = pl.multiple_of(j * tq, tq)
        if diagonal:
            acc_ref[hh, 0:half] += _dot(p_ref[hh, 0:half, 0:half], v_ref[hh, pl.ds(start, half), :])
            acc_ref[hh, half:tq] += _dot(p_ref[hh, half:tq, :], v_ref[hh, pl.ds(start, tq), :])
        else:
            acc_ref[hh] += _dot(p_ref[hh], v_ref[hh, pl.ds(start, tq), :])

    def step(j, slot, next_diagonal=False):
        for hh in range(hp):
            softmax_pv(hh, j, slot, False)
            scores(hh, j + 1, 1 - slot, next_diagonal)

    def last_step(slot):
        for hh in range(hp):
            softmax_pv(hh, i, slot, True)

    @pl.when(i == 0)
    def _():
        for hh in range(hp):
            scores(hh, 0, 0, True)
        last_step(0)

    @pl.when(i > 0)
    def _():
        for hh in range(hp):
            scores(hh, 0, 0)

    def body(t, carry):
        step(2 * t, 0)
        step(2 * t + 1, 1)
        return carry

    lax.fori_loop(0, jnp.maximum(i - 1, 0) // 2, body, 0)

    @pl.when(i % 2 == 1)
    def _():
        step(i - 1, 0, True)
        last_step(1)

    @pl.when((i % 2 == 0) & (i > 0))
    def _():
        step(i - 2, 0)
        step(i - 1, 1, True)
        last_step(0)

    outs = [acc_ref[hh] / acc_ref[hh, :, ONES_LANE:ONES_LANE + 1] for hh in range(hp)]
    o_ref[...] = jnp.where(lane < V_HEAD_DIM, outs[0],
                           pltpu.roll(outs[1], V_HEAD_DIM, 1)).astype(o_ref.dtype)


def _attention(q, k, v):
    B, H, S, _ = q.shape
    tq = ATTN_BLOCK
    hp = HEADS_PER_STEP
    assert hp * V_HEAD_DIM == V7X_LANES
    return pl.pallas_call(
        _attn_kernel,
        grid=(B, H // hp, S // tq),
        in_specs=[pl.BlockSpec((None, hp, tq, HEAD_PAD), lambda b, g, i: (b, g, i, 0)),
                  pl.BlockSpec((None, hp, S, HEAD_PAD), lambda b, g, i: (b, g, 0, 0)),
                  pl.BlockSpec((None, hp, S, HEAD_PAD), lambda b, g, i: (b, g, 0, 0))],
        out_specs=pl.BlockSpec((None, tq, hp * V_HEAD_DIM), lambda b, g, i: (b, i, g)),
        out_shape=jax.ShapeDtypeStruct((B, S, H * V_HEAD_DIM), bf16),
        scratch_shapes=[pltpu.VMEM((2, hp, tq, tq), f32),
                        pltpu.VMEM((hp, tq, tq), bf16),
                        pltpu.VMEM((hp, tq, V7X_LANES), f32),
                        pltpu.VMEM((hp, tq, HEAD_PAD), f32)],
        compiler_params=pltpu.CompilerParams(
            dimension_semantics=("arbitrary", "arbitrary", "arbitrary"),
            vmem_limit_bytes=V7X_VMEM_BYTES * 7 // 8),
        name="mla_attention",
    )(q, k, v)


def _back_kernel(x_ref, o_ref, sb_ref, g1_ref, part_ref, w_ob_ref, w_out_ref, norm_post_ref,
                 out_ref):
    y_b = o_ref[...] * sb_ref[...]
    merged = part_ref[...].astype(f32) + g1_ref[...].astype(f32) * _dot(y_b, w_ob_ref[...])
    z = _dot(merged.astype(bf16), w_out_ref[...])
    out_ref[...] = x_ref[...] + _rms(z, norm_post_ref[...])


def _back(x, o, sb, g1, part, p, l):
    B, S, _ = x.shape
    tb = BACK_BLOCK
    tok = lambda width: pl.BlockSpec((None, tb, width), lambda b, j: (b, j, 0))
    consts = [p["w_ob"], p["w_out"], p["norm_post"]]
    return pl.pallas_call(
        _back_kernel,
        grid=(B, S // tb),
        in_specs=[tok(D_MODEL), tok(MLA_WIDTH), tok(MLA_WIDTH), tok(D_MODEL), tok(D_MODEL)]
        + [_const_spec(c, l) for c in consts],
        out_specs=tok(D_MODEL),
        out_shape=jax.ShapeDtypeStruct((B, S, D_MODEL), f32),
        compiler_params=pltpu.CompilerParams(
            dimension_semantics=("arbitrary", "arbitrary"),
            vmem_limit_bytes=V7X_VMEM_BYTES * 3 // 4),
        name="layer_back",
    )(x, o, sb, g1, part, *consts)


def _stacked_params(norm_pre, norm_post, w_in, b_gate, pool_w, pool_scale, q_norm, w_uq,
                    kv_norm, w_ukv, v_norm_g, v_norm_b, w_s, b_s, w_oa, w_ob, w_oc, w_out):
    L = w_in.shape[0]
    zeros = lambda *shape: jnp.zeros(shape, bf16)
    kr_src = sum(IN_SPLITS[:4])
    assert kr_src == COL_KR
    w_head = w_in[..., :kr_src].astype(bf16)
    w_kr = jnp.concatenate([zeros(L, D_MODEL, ROPE_LANE0),
                            w_in[..., kr_src:kr_src + QK_ROPE_DIM].astype(bf16),
                            zeros(L, D_MODEL, HEAD_PAD - ROPE_LANE0 - QK_ROPE_DIM)], axis=-1)
    w_tail = w_in[..., kr_src + QK_ROPE_DIM:].astype(bf16)
    assert w_tail.shape[-1] == IN_WIDTH_PAD - COL_B_GATE

    uq = w_uq.astype(bf16).reshape(L, Q_LORA_RANK, MLA_HEADS, QK_NOPE_DIM + QK_ROPE_DIM)
    nope, x1, x2 = uq[..., :QK_NOPE_DIM], uq[..., QK_NOPE_DIM:QK_NOPE_DIM + HALF_ROPE], uq[..., QK_NOPE_DIM + HALF_ROPE:]
    pad = zeros(L, Q_LORA_RANK, MLA_HEADS, HEAD_PAD - QK_NOPE_DIM - QK_ROPE_DIM)
    w_uq_p = jnp.concatenate([nope, x1, x2, pad], axis=-1).reshape(L, Q_LORA_RANK, MLA_HEADS * HEAD_PAD)

    assert 2 * QK_NOPE_DIM == HEAD_PAD and V_HEAD_DIM == QK_NOPE_DIM
    ukv = w_ukv.astype(bf16).reshape(L, KV_LORA_RANK, MLA_HEADS, QK_NOPE_DIM + V_HEAD_DIM)
    k_cols = ukv[..., :QK_NOPE_DIM].reshape(L, KV_LORA_RANK, MLA_HEADS * QK_NOPE_DIM)
    v_cols = ukv[..., QK_NOPE_DIM:].reshape(L, KV_LORA_RANK, MLA_HEADS * V_HEAD_DIM)
    w_ukv_p = jnp.concatenate([k_cols, v_cols], axis=-1)

    row = lambda a: a.reshape(L, 1, -1)
    return {
        "norm_pre": row(norm_pre), "norm_post": row(norm_post),
        "w_head": w_head, "w_kr": w_kr, "w_tail": w_tail,
        "b_gate": row(b_gate), "pool_w": pool_w.astype(bf16), "pool_scale": row(pool_scale),
        "q_norm": row(q_norm), "w_uq": w_uq_p, "kv_norm": row(kv_norm), "w_ukv": w_ukv_p,
        "v_norm_g": row(v_norm_g), "v_norm_b": row(v_norm_b), "w_s": w_s.astype(bf16),
        "bs_full": jnp.repeat(jnp.swapaxes(b_s, 1, 2), GMLP_GROUP_DIM, axis=2),
        "w_oa": w_oa.astype(bf16), "w_ob": w_ob.astype(bf16), "w_oc": w_oc.astype(bf16),
        "w_out": w_out.astype(bf16),
    }


def kernel(x, positions, norm_pre, norm_post, w_in, b_gate, pool_w, pool_scale, q_norm, w_uq, kv_norm, w_ukv, v_norm_g, v_norm_b, w_s, b_s, w_oa, w_ob, w_oc, w_out):
    B, S, _ = x.shape
    cos_t, sin_t = _rope_tables(positions)
    cos_t = cos_t.reshape(B, S, HEAD_PAD)
    sin_t = sin_t.reshape(B, S, HEAD_PAD)
    p = _stacked_params(norm_pre, norm_post, w_in, b_gate, pool_w, pool_scale, q_norm, w_uq,
                        kv_norm, w_ukv, v_norm_g, v_norm_b, w_s, b_s, w_oa, w_ob, w_oc, w_out)
    for l in range(norm_pre.shape[0]):
        q, k, v, part, g1, sb = _front(x, cos_t, sin_t, p, l)
        o = _attention(q, k, v)
        x = _back(x, o, sb, g1, part, p, l)
    return x
```

```python
import math

import jax
import jax.numpy as jnp
from jax import lax
from jax.experimental import pallas as pl
from jax.experimental.pallas import tpu as pltpu

D_MODEL = 1024
POOL_WIDTH = 512
POOL_GROUPS = 4
POOL_GROUP_DIM = POOL_WIDTH // POOL_GROUPS
POOL_WINDOWS = (2, 4, 8, 16)
MLA_HEADS = 8
QK_NOPE_DIM = 64
QK_ROPE_DIM = 32
HALF_ROPE = QK_ROPE_DIM // 2
V_HEAD_DIM = 64
Q_LORA_RANK = 256
KV_LORA_RANK = 128
MLA_WIDTH = MLA_HEADS * V_HEAD_DIM
ROPE_THETA = 10000.0
GMLP_WIDTH = 512
GMLP_CHUNK = 128
GMLP_GROUPS = 8
GMLP_GROUP_DIM = GMLP_WIDTH // GMLP_GROUPS
N_BRANCHES = 3
EPS = 1e-6
IN_SPLITS = (POOL_WIDTH, POOL_WIDTH, Q_LORA_RANK, KV_LORA_RANK, QK_ROPE_DIM, MLA_WIDTH,
             GMLP_WIDTH, GMLP_WIDTH, GMLP_WIDTH, N_BRANCHES * D_MODEL)

V7X_LANES = 128
V7X_SUBLANES = 8
V7X_VMEM_BYTES = 64 * 1024 * 1024

HEAD_PAD = V7X_LANES
ROPE_LANE0 = QK_NOPE_DIM
ONES_LANE = V_HEAD_DIM
HALO = max(POOL_WINDOWS)
POOL_PAD = V7X_SUBLANES
POOL_TOP = POOL_PAD + HALO
assert all(w & (w - 1) == 0 and w <= HALO for w in POOL_WINDOWS)
POOL_LEVELS = sum(w.bit_length() - 2 for w in POOL_WINDOWS)

COL_A_IN = 0
COL_A_GATE = COL_A_IN + POOL_WIDTH
COL_CQ = COL_A_GATE + POOL_WIDTH
COL_CKV = COL_CQ + Q_LORA_RANK
COL_KR = COL_CKV + KV_LORA_RANK
COL_B_GATE = COL_KR + HEAD_PAD
COL_U = COL_B_GATE + MLA_WIDTH
COL_V = COL_U + GMLP_WIDTH
COL_C_GATE = COL_V + GMLP_WIDTH
COL_GATES = COL_C_GATE + GMLP_WIDTH
IN_WIDTH_PAD = COL_GATES + N_BRANCHES * D_MODEL

SOFTMAX_SCALE = (QK_NOPE_DIM + QK_ROPE_DIM) ** -0.5
LOG2E = math.log2(math.e)

FRONT_BLOCK = 512
FRONT_VMEM_BYTES = V7X_VMEM_BYTES * 7 // 8
ATTN_BLOCK = 1024
ATTN_VMEM_BYTES = V7X_VMEM_BYTES * 3 // 4
BACK_BLOCK = 1024
BACK_VMEM_BYTES = V7X_VMEM_BYTES * 3 // 4
ROPE_BLOCK = 1024
HEADS_PER_STEP = 2
SOFTMAX_ROWS = 64

bf16 = jnp.bfloat16
f32 = jnp.float32


def _sigmoid(z):
    return 0.5 * jnp.tanh(0.5 * z) + 0.5


def _silu(z):
    return z * _sigmoid(z)


def _rms(x, g):
    return x * lax.rsqrt(jnp.mean(x * x, axis=-1, keepdims=True) + EPS) * g


def _dot(a, b):
    return jnp.dot(a, b, preferred_element_type=f32)


def _rope_table_kernel(pos_ref, invf_ref, cos_ref, sin_ref):
    lane = lax.broadcasted_iota(jnp.int32, cos_ref.shape, 1)
    ang = pos_ref[...].astype(f32) * invf_ref[...]
    c = jnp.cos(ang)
    s = jnp.sin(ang)
    rope = (lane >= ROPE_LANE0) & (lane < ROPE_LANE0 + QK_ROPE_DIM)
    first_half = lane < ROPE_LANE0 + HALF_ROPE
    cos_ref[...] = jnp.where(lane < ROPE_LANE0, 1.0, jnp.where(rope, c, 0.0))
    sin_ref[...] = jnp.where(rope, jnp.where(first_half, -s, s), 0.0)


def _rope_tables(positions):
    n = positions.size
    inv_freq = ROPE_THETA ** (-jnp.arange(0, QK_ROPE_DIM, 2, dtype=f32) / QK_ROPE_DIM)
    pad = jnp.zeros((HEAD_PAD - ROPE_LANE0 - QK_ROPE_DIM,), f32)
    invf = jnp.concatenate([jnp.zeros((ROPE_LANE0,), f32), inv_freq, inv_freq, pad]).reshape(1, HEAD_PAD)
    out = jax.ShapeDtypeStruct((n, HEAD_PAD), f32)
    return pl.pallas_call(
        _rope_table_kernel,
        grid=(n // ROPE_BLOCK,),
        in_specs=[pl.BlockSpec((ROPE_BLOCK, 1), lambda i: (i, 0)),
                  pl.BlockSpec((1, HEAD_PAD), lambda i: (0, 0))],
        out_specs=[pl.BlockSpec((ROPE_BLOCK, HEAD_PAD), lambda i: (i, 0))] * 2,
        out_shape=[out, out],
        name="rope_tables",
    )(positions.reshape(n, 1), invf)


def _front_kernel(x_ref, cos_ref, sin_ref, norm_pre_ref, w_head_ref, w_kr_ref, w_tail_ref, b_gate_ref, pool_w_ref,
                  pool_scale_ref, q_norm_ref, w_uq_ref, kv_norm_ref, w_ukv_ref, vn_g_ref,
                  vn_b_ref, w_s_ref, bs_ref, w_oa_ref, w_oc_ref,
                  q_ref, k_ref, v_ref, part_ref, g1_ref, sb_ref, abuf, lv_ref):
    tb = x_ref.shape[0]
    j = pl.program_id(1)

    hb = _rms(x_ref[...], norm_pre_ref[...]).astype(bf16)

    def proj(c0, width):
        if c0 + width <= COL_KR:
            w = w_head_ref[:, c0:c0 + width]
        elif c0 == COL_KR and width == HEAD_PAD:
            w = w_kr_ref[...]
        else:
            assert c0 >= COL_B_GATE
            w = w_tail_ref[:, c0 - COL_B_GATE:c0 - COL_B_GATE + width]
        return _dot(hb, w)

    @pl.when(j == 0)
    def _():
        abuf[0:POOL_TOP, :] = jnp.zeros((POOL_TOP, POOL_WIDTH), f32)
        lv_ref[:, 0:POOL_PAD, :] = jnp.zeros((lv_ref.shape[0], POOL_PAD, POOL_GROUP_DIM), f32)

    a_in = proj(COL_A_IN, POOL_WIDTH)
    abuf[POOL_TOP:POOL_TOP + tb, :] = a_in
    cq = proj(COL_CQ, Q_LORA_RANK)
    ckv = proj(COL_CKV, KV_LORA_RANK)
    kr = proj(COL_KR, HEAD_PAD)
    vv = proj(COL_V, GMLP_WIDTH)
    a_gate = proj(COL_A_GATE, POOL_WIDTH)
    b_gate = proj(COL_B_GATE, MLA_WIDTH)
    u = proj(COL_U, GMLP_WIDTH)
    c_gate = proj(COL_C_GATE, GMLP_WIDTH)

    t = j * tb + lax.broadcasted_iota(jnp.int32, (tb, 1), 0)
    ys = []
    level = 0
    for g, w in enumerate(POOL_WINDOWS):
        cols = slice(g * POOL_GROUP_DIM, (g + 1) * POOL_GROUP_DIM)
        src = lambda lo, n: abuf[lo:lo + n, cols]
        shift = 1
        while 2 * shift < w:
            lo, n = POOL_PAD, HALO + tb
            dst = level
            lv_ref[dst, lo:lo + n, :] = src(lo, n) + src(lo - shift, n)
            src = lambda lo, n, dst=dst: lv_ref[dst, lo:lo + n, :]
            level += 1
            shift *= 2
        win = src(POOL_TOP, tb) + src(POOL_TOP - shift, tb)
        count = jnp.minimum(t + 1, w).astype(f32)
        mixed = win / count - a_in[:, cols]
        ys.append(_dot(mixed.astype(bf16), pool_w_ref[g]))
    abuf[POOL_PAD:POOL_TOP, :] = abuf[tb + POOL_PAD:tb + POOL_TOP, :]

    cqn = _rms(cq, q_norm_ref[...]).astype(bf16)
    q2 = _dot(cqn, w_uq_ref[...])
    ckvn = _rms(ckv, kv_norm_ref[...]).astype(bf16)
    kv2 = _dot(ckvn, w_ukv_ref[...])

    vc = vv - jnp.mean(vv, axis=-1, keepdims=True)
    var = jnp.mean(vc * vc, axis=-1, keepdims=True)
    vb = (vc * lax.rsqrt(var + EPS) * vn_g_ref[...] + vn_b_ref[...]).astype(bf16)
    row = lax.broadcasted_iota(jnp.int32, (GMLP_CHUNK, GMLP_CHUNK), 0)
    col = lax.broadcasted_iota(jnp.int32, (GMLP_CHUNK, GMLP_CHUNK), 1)
    ws = [jnp.where(col <= row, w_s_ref[g], jnp.zeros((), bf16)) for g in range(GMLP_GROUPS)]
    groups_per_tile = V7X_LANES // GMLP_GROUP_DIM
    assert groups_per_tile == 2
    chunks = []
    for c in range(tb // GMLP_CHUNK):
        tiles = []
        for tile in range(GMLP_WIDTH // V7X_LANES):
            rhs = vb[c * GMLP_CHUNK:(c + 1) * GMLP_CHUNK, tile * V7X_LANES:(tile + 1) * V7X_LANES]
            lo = _dot(ws[2 * tile], rhs)
            hi = _dot(ws[2 * tile + 1], rhs)
            tiles.append(jnp.where(col < GMLP_GROUP_DIM, lo, hi))
        chunks.append(jnp.concatenate(tiles, axis=1) + bs_ref[...])
    sv = jnp.concatenate(chunks, axis=0)

    gate_pre = {i: proj(COL_GATES + i * D_MODEL, D_MODEL) for i in (0, 2)}

    cos_t = cos_ref[...]
    sin_t = sin_ref[...]
    lane = lax.broadcasted_iota(jnp.int32, (tb, HEAD_PAD), 1)
    first_half = lane < ROPE_LANE0 + HALF_ROPE

    def rotate(t):
        swapped = jnp.where(first_half, pltpu.roll(t, HEAD_PAD - HALF_ROPE, 1), pltpu.roll(t, HALF_ROPE, 1))
        return t * cos_t + swapped * sin_t

    for h in range(MLA_HEADS):
        q_ref[h] = (rotate(q2[:, h * HEAD_PAD:(h + 1) * HEAD_PAD]) * (SOFTMAX_SCALE * LOG2E)).astype(bf16)
    k_rope = rotate(kr)
    low = lane < QK_NOPE_DIM
    ones_lane = jnp.where(lane == ONES_LANE, 1.0, 0.0)
    v0 = MLA_HEADS * QK_NOPE_DIM
    for pair in range(MLA_HEADS // 2):
        k_pair = kv2[:, pair * HEAD_PAD:(pair + 1) * HEAD_PAD]
        v_pair = kv2[:, v0 + pair * HEAD_PAD:v0 + (pair + 1) * HEAD_PAD]
        for odd in range(2):
            k_h = pltpu.roll(k_pair, QK_NOPE_DIM, 1) if odd else k_pair
            v_h = pltpu.roll(v_pair, V_HEAD_DIM, 1) if odd else v_pair
            k_ref[2 * pair + odd] = (jnp.where(low, k_h, 0.0) + k_rope).astype(bf16)
            v_ref[2 * pair + odd] = jnp.where(low, v_h, ones_lane).astype(bf16)
    sb_ref[...] = _silu(b_gate).astype(sb_ref.dtype)

    y_a = jnp.concatenate(ys, axis=1) * pool_scale_ref[...] * _silu(a_gate)
    y_c = u * sv * _silu(c_gate)

    def gate(i):
        return _sigmoid(gate_pre[i] + b_gate_ref[:, i * D_MODEL:(i + 1) * D_MODEL])

    part_ref[...] = (gate(0) * _dot(y_a.astype(bf16), w_oa_ref[...])
                     + gate(2) * _dot(y_c.astype(bf16), w_oc_ref[...])).astype(part_ref.dtype)
    gate_pre[1] = proj(COL_GATES + D_MODEL, D_MODEL)
    g1_ref[...] = gate(1).astype(g1_ref.dtype)


def _const_spec(stacked, l):
    index = (l,) + (0,) * (stacked.ndim - 1)
    return pl.BlockSpec((None,) + stacked.shape[1:], lambda b, j: index, pipeline_mode=pl.Buffered(1))


def _front(x, cos_t, sin_t, p, l):
    B, S, _ = x.shape
    tb = FRONT_BLOCK
    tok = lambda width: pl.BlockSpec((None, tb, width), lambda b, j: (b, j, 0))
    heads = pl.BlockSpec((None, MLA_HEADS, tb, HEAD_PAD), lambda b, j: (b, 0, j, 0))
    consts = [p["norm_pre"], p["w_head"], p["w_kr"], p["w_tail"], p["b_gate"], p["pool_w"],
              p["pool_scale"], p["q_norm"],
              p["w_uq"], p["kv_norm"], p["w_ukv"], p["v_norm_g"], p["v_norm_b"], p["w_s"],
              p["bs_full"], p["w_oa"], p["w_oc"]]
    head_shape = jax.ShapeDtypeStruct((B, MLA_HEADS, S, HEAD_PAD), bf16)
    return pl.pallas_call(
        _front_kernel,
        grid=(B, S // tb),
        in_specs=[tok(D_MODEL), tok(HEAD_PAD), tok(HEAD_PAD)] + [_const_spec(c, l) for c in consts],
        out_specs=[heads, heads, heads, tok(D_MODEL), tok(D_MODEL), tok(MLA_WIDTH)],
        out_shape=[head_shape, head_shape, head_shape,
                   jax.ShapeDtypeStruct((B, S, D_MODEL), bf16),
                   jax.ShapeDtypeStruct((B, S, D_MODEL), bf16),
                   jax.ShapeDtypeStruct((B, S, MLA_WIDTH), bf16)],
        scratch_shapes=[pltpu.VMEM((POOL_TOP + tb, POOL_WIDTH), f32),
                        pltpu.VMEM((POOL_LEVELS, POOL_TOP + tb, POOL_GROUP_DIM), f32)],
        compiler_params=pltpu.CompilerParams(
            dimension_semantics=("arbitrary", "arbitrary"),
            vmem_limit_bytes=FRONT_VMEM_BYTES),
        name="layer_front",
    )(x, cos_t, sin_t, *consts)


def _attn_kernel(q_ref, k_ref, v_ref, o_ref, s_ref, p_ref, m_ref, acc_ref):
    hp, tq, _ = q_ref.shape
    i = pl.program_id(2)
    lane = lax.broadcasted_iota(jnp.int32, (tq, HEAD_PAD), 1)
    n_tiles = tq // V7X_LANES
    half = tq // 2

    m_ref[...] = jnp.full(m_ref.shape, -jnp.inf, f32)
    acc_ref[...] = jnp.zeros(acc_ref.shape, f32)

    nt_dims = (((1,), (1,)), ((), ()))

    def scores(hh, j, slot, diagonal=False):
        start = pl.multiple_of(j * tq, tq)
        if diagonal:
            s_ref[slot, hh, 0:half, 0:half] = lax.dot_general(
                q_ref[hh, 0:half], k_ref[hh, pl.ds(start, half), :], nt_dims, preferred_element_type=f32)
            s_ref[slot, hh, half:tq, :] = lax.dot_general(
                q_ref[hh, half:tq], k_ref[hh, pl.ds(start, tq), :], nt_dims, preferred_element_type=f32)
        else:
            s_ref[slot, hh] = lax.dot_general(q_ref[hh], k_ref[hh, pl.ds(start, tq), :], nt_dims,
                                              preferred_element_type=f32)

    def softmax_pv(hh, j, slot, diagonal):
        for r in range(tq // SOFTMAX_ROWS):
            r0 = r * SOFTMAX_ROWS
            rows = slice(r0, r0 + SOFTMAX_ROWS)
            live = (r0 + SOFTMAX_ROWS - 1) // V7X_LANES + 1 if diagonal else n_tiles
            read = (half if r0 < half else tq) // V7X_LANES if diagonal else n_tiles

            def tile(c):
                s = s_ref[slot, hh, rows, c * V7X_LANES:(c + 1) * V7X_LANES]
                if diagonal and (c + 1) * V7X_LANES - 1 > r0:
                    row = r0 + lax.broadcasted_iota(jnp.int32, s.shape, 0)
                    col = c * V7X_LANES + lax.broadcasted_iota(jnp.int32, s.shape, 1)
                    s = jnp.where(col <= row, s, -jnp.inf)
                return s

            m_blk = tile(0)
            for c in range(1, live):
                m_blk = jnp.maximum(m_blk, tile(c))
            m_old = m_ref[hh, rows]
            m_new = jnp.maximum(m_old, jnp.max(m_blk, axis=-1, keepdims=True))
            m_ref[hh, rows] = m_new
            acc_ref[hh, rows] = jnp.exp2(m_old - m_new) * acc_ref[hh, rows]
            for c in range(live):
                p_ref[hh, rows, c * V7X_LANES:(c + 1) * V7X_LANES] = jnp.exp2(tile(c) - m_new).astype(bf16)
            if live < read:
                p_ref[hh, rows, live * V7X_LANES:read * V7X_LANES] = jnp.zeros(
                    (SOFTMAX_ROWS, (read - live) * V7X_LANES), bf16)
        start = pl.multiple_of(j * tq, tq)
        if diagonal:
            acc_ref[hh, 0:half] += _dot(p_ref[hh, 0:half, 0:half], v_ref[hh, pl.ds(start, half), :])
            acc_ref[hh, half:tq] += _dot(p_ref[hh, half:tq, :], v_ref[hh, pl.ds(start, tq), :])
        else:
            acc_ref[hh] += _dot(p_ref[hh], v_ref[hh, pl.ds(start, tq), :])

    def step(j, slot, next_diagonal=False):
        for hh in range(hp):
            softmax_pv(hh, j, slot, False)
            scores(hh, j + 1, 1 - slot, next_diagonal)

    def last_step(slot):
        for hh in range(hp):
            softmax_pv(hh, i, slot, True)

    @pl.when(i == 0)
    def _():
        for hh in range(hp):
            scores(hh, 0, 0, True)
        last_step(0)

    @pl.when(i > 0)
    def _():
        for hh in range(hp):
            scores(hh, 0, 0)

    def body(t, carry):
        step(2 * t, 0)
        step(2 * t + 1, 1)
        return carry

    lax.fori_loop(0, jnp.maximum(i - 1, 0) // 2, body, 0)

    @pl.when(i % 2 == 1)
    def _():
        step(i - 1, 0, True)
        last_step(1)

    @pl.when((i % 2 == 0) & (i > 0))
    def _():
        step(i - 2, 0)
        step(i - 1, 1, True)
        last_step(0)

    outs = [acc_ref[hh] / acc_ref[hh, :, ONES_LANE:ONES_LANE + 1] for hh in range(hp)]
    o_ref[...] = jnp.where(lane < V_HEAD_DIM, outs[0],
                           pltpu.roll(outs[1], V_HEAD_DIM, 1)).astype(o_ref.dtype)


def _attention(q, k, v):
    B, H, S, _ = q.shape
    tq = ATTN_BLOCK
    hp = HEADS_PER_STEP
    assert hp * V_HEAD_DIM == V7X_LANES
    return pl.pallas_call(
        _attn_kernel,
        grid=(B, H // hp, S // tq),
        in_specs=[pl.BlockSpec((None, hp, tq, HEAD_PAD), lambda b, g, i: (b, g, i, 0)),
                  pl.BlockSpec((None, hp, S, HEAD_PAD), lambda b, g, i: (b, g, 0, 0)),
                  pl.BlockSpec((None, hp, S, HEAD_PAD), lambda b, g, i: (b, g, 0, 0))],
        out_specs=pl.BlockSpec((None, tq, hp * V_HEAD_DIM), lambda b, g, i: (b, i, g)),
        out_shape=jax.ShapeDtypeStruct((B, S, H * V_HEAD_DIM), bf16),
        scratch_shapes=[pltpu.VMEM((2, hp, tq, tq), f32),
                        pltpu.VMEM((hp, tq, tq), bf16),
                        pltpu.VMEM((hp, tq, V7X_LANES), f32),
                        pltpu.VMEM((hp, tq, HEAD_PAD), f32)],
        compiler_params=pltpu.CompilerParams(
            dimension_semantics=("arbitrary", "arbitrary", "arbitrary"),
            vmem_limit_bytes=ATTN_VMEM_BYTES),
        name="mla_attention",
    )(q, k, v)


def _back_kernel(x_ref, o_ref, sb_ref, g1_ref, part_ref, w_ob_ref, w_out_ref, norm_post_ref,
                 out_ref):
    y_b = o_ref[...] * sb_ref[...]
    merged = part_ref[...].astype(f32) + g1_ref[...].astype(f32) * _dot(y_b, w_ob_ref[...])
    z = _dot(merged.astype(bf16), w_out_ref[...])
    out_ref[...] = x_ref[...] + _rms(z, norm_post_ref[...])


def _back(x, o, sb, g1, part, p, l):
    B, S, _ = x.shape
    tb = BACK_BLOCK
    tok = lambda width: pl.BlockSpec((None, tb, width), lambda b, j: (b, j, 0))
    consts = [p["w_ob"], p["w_out"], p["norm_post"]]
    return pl.pallas_call(
        _back_kernel,
        grid=(B, S // tb),
        in_specs=[tok(D_MODEL), tok(MLA_WIDTH), tok(MLA_WIDTH), tok(D_MODEL), tok(D_MODEL)]
        + [_const_spec(c, l) for c in consts],
        out_specs=tok(D_MODEL),
        out_shape=jax.ShapeDtypeStruct((B, S, D_MODEL), f32),
        compiler_params=pltpu.CompilerParams(
            dimension_semantics=("arbitrary", "arbitrary"),
            vmem_limit_bytes=BACK_VMEM_BYTES),
        name="layer_back",
    )(x, o, sb, g1, part, *consts)


def _stacked_params(norm_pre, norm_post, w_in, b_gate, pool_w, pool_scale, q_norm, w_uq,
                    kv_norm, w_ukv, v_norm_g, v_norm_b, w_s, b_s, w_oa, w_ob, w_oc, w_out):
    L = w_in.shape[0]
    zeros = lambda *shape: jnp.zeros(shape, bf16)
    kr_src = sum(IN_SPLITS[:4])
    assert kr_src == COL_KR
    w_head = w_in[..., :kr_src].astype(bf16)
    w_kr = jnp.concatenate([zeros(L, D_MODEL, ROPE_LANE0),
                            w_in[..., kr_src:kr_src + QK_ROPE_DIM].astype(bf16),
                            zeros(L, D_MODEL, HEAD_PAD - ROPE_LANE0 - QK_ROPE_DIM)], axis=-1)
    w_tail = w_in[..., kr_src + QK_ROPE_DIM:].astype(bf16)
    assert w_tail.shape[-1] == IN_WIDTH_PAD - COL_B_GATE

    uq = w_uq.astype(bf16).reshape(L, Q_LORA_RANK, MLA_HEADS, QK_NOPE_DIM + QK_ROPE_DIM)
    nope, x1, x2 = uq[..., :QK_NOPE_DIM], uq[..., QK_NOPE_DIM:QK_NOPE_DIM + HALF_ROPE], uq[..., QK_NOPE_DIM + HALF_ROPE:]
    pad = zeros(L, Q_LORA_RANK, MLA_HEADS, HEAD_PAD - QK_NOPE_DIM - QK_ROPE_DIM)
    w_uq_p = jnp.concatenate([nope, x1, x2, pad], axis=-1).reshape(L, Q_LORA_RANK, MLA_HEADS * HEAD_PAD)

    assert 2 * QK_NOPE_DIM == HEAD_PAD and V_HEAD_DIM == QK_NOPE_DIM
    ukv = w_ukv.astype(bf16).reshape(L, KV_LORA_RANK, MLA_HEADS, QK_NOPE_DIM + V_HEAD_DIM)
    k_cols = ukv[..., :QK_NOPE_DIM].reshape(L, KV_LORA_RANK, MLA_HEADS * QK_NOPE_DIM)
    v_cols = ukv[..., QK_NOPE_DIM:].reshape(L, KV_LORA_RANK, MLA_HEADS * V_HEAD_DIM)
    w_ukv_p = jnp.concatenate([k_cols, v_cols], axis=-1)

    row = lambda a: a.reshape(L, 1, -1)
    return {
        "norm_pre": row(norm_pre), "norm_post": row(norm_post),
        "w_head": w_head, "w_kr": w_kr, "w_tail": w_tail,
        "b_gate": row(b_gate), "pool_w": pool_w.astype(bf16), "pool_scale": row(pool_scale),
        "q_norm": row(q_norm), "w_uq": w_uq_p, "kv_norm": row(kv_norm), "w_ukv": w_ukv_p,
        "v_norm_g": row(v_norm_g), "v_norm_b": row(v_norm_b), "w_s": w_s.astype(bf16),
        "bs_full": jnp.repeat(jnp.swapaxes(b_s, 1, 2), GMLP_GROUP_DIM, axis=2),
        "w_oa": w_oa.astype(bf16), "w_ob": w_ob.astype(bf16), "w_oc": w_oc.astype(bf16),
        "w_out": w_out.astype(bf16),
    }


def kernel(x, positions, norm_pre, norm_post, w_in, b_gate, pool_w, pool_scale, q_norm, w_uq, kv_norm, w_ukv, v_norm_g, v_norm_b, w_s, b_s, w_oa, w_ob, w_oc, w_out):
    B, S, _ = x.shape
    cos_t, sin_t = _rope_tables(positions)
    cos_t = cos_t.reshape(B, S, HEAD_PAD)
    sin_t = sin_t.reshape(B, S, HEAD_PAD)
    p = _stacked_params(norm_pre, norm_post, w_in, b_gate, pool_w, pool_scale, q_norm, w_uq,
                        kv_norm, w_ukv, v_norm_g, v_norm_b, w_s, b_s, w_oa, w_ob, w_oc, w_out)
    for l in range(norm_pre.shape[0]):
        q, k, v, part, g1, sb = _front(x, cos_t, sin_t, p, l)
        o = _attention(q, k, v)
        x = _back(x, o, sb, g1, part, p, l)
    return x
```

```python
import math

import jax
import jax.numpy as jnp
from jax import lax
from jax.experimental import pallas as pl
from jax.experimental.pallas import tpu as pltpu

D_MODEL = 1024
POOL_WIDTH = 512
POOL_GROUPS = 4
POOL_GROUP_DIM = POOL_WIDTH // POOL_GROUPS
POOL_WINDOWS = (2, 4, 8, 16)
MLA_HEADS = 8
QK_NOPE_DIM = 64
QK_ROPE_DIM = 32
HALF_ROPE = QK_ROPE_DIM // 2
V_HEAD_DIM = 64
Q_LORA_RANK = 256
KV_LORA_RANK = 128
MLA_WIDTH = MLA_HEADS * V_HEAD_DIM
ROPE_THETA = 10000.0
GMLP_WIDTH = 512
GMLP_CHUNK = 128
GMLP_GROUPS = 8
GMLP_GROUP_DIM = GMLP_WIDTH // GMLP_GROUPS
N_BRANCHES = 3
EPS = 1e-6
IN_SPLITS = (POOL_WIDTH, POOL_WIDTH, Q_LORA_RANK, KV_LORA_RANK, QK_ROPE_DIM, MLA_WIDTH,
             GMLP_WIDTH, GMLP_WIDTH, GMLP_WIDTH, N_BRANCHES * D_MODEL)

V7X_LANES = 128
V7X_SUBLANES = 8
V7X_VMEM_BYTES = 64 * 1024 * 1024

HEAD_PAD = V7X_LANES
ROPE_LANE0 = QK_NOPE_DIM
ONES_LANE = V_HEAD_DIM
HALO = max(POOL_WINDOWS)
POOL_PAD = V7X_SUBLANES
POOL_TOP = POOL_PAD + HALO
assert all(w & (w - 1) == 0 and w <= HALO for w in POOL_WINDOWS)
POOL_LEVELS = sum(w.bit_length() - 2 for w in POOL_WINDOWS)

COL_A_IN = 0
COL_A_GATE = COL_A_IN + POOL_WIDTH
COL_CQ = COL_A_GATE + POOL_WIDTH
COL_CKV = COL_CQ + Q_LORA_RANK
COL_KR = COL_CKV + KV_LORA_RANK
COL_B_GATE = COL_KR + HEAD_PAD
COL_U = COL_B_GATE + MLA_WIDTH
COL_V = COL_U + GMLP_WIDTH
COL_C_GATE = COL_V + GMLP_WIDTH
COL_GATES = COL_C_GATE + GMLP_WIDTH
IN_WIDTH_PAD = COL_GATES + N_BRANCHES * D_MODEL

SOFTMAX_SCALE = (QK_NOPE_DIM + QK_ROPE_DIM) ** -0.5
LOG2E = math.log2(math.e)

FRONT_BLOCK = 512
FRONT_VMEM_BYTES = V7X_VMEM_BYTES * 7 // 8
ATTN_BLOCK = 1024
ATTN_VMEM_BYTES = V7X_VMEM_BYTES * 7 // 8
BACK_BLOCK = 1024
BACK_VMEM_BYTES = V7X_VMEM_BYTES * 3 // 4
ROPE_BLOCK = 1024
HEADS_PER_STEP = 2
SOFTMAX_ROWS = 64

bf16 = jnp.bfloat16
f32 = jnp.float32


def _sigmoid(z):
    return 0.5 * jnp.tanh(0.5 * z) + 0.5


def _silu(z):
    return z * _sigmoid(z)


def _rms(x, g):
    return x * lax.rsqrt(jnp.mean(x * x, axis=-1, keepdims=True) + EPS) * g


def _dot(a, b):
    return jnp.dot(a, b, preferred_element_type=f32)


def _rope_table_kernel(pos_ref, invf_ref, cos_ref, sin_ref):
    lane = lax.broadcasted_iota(jnp.int32, cos_ref.shape, 1)
    ang = pos_ref[...].astype(f32) * invf_ref[...]
    c = jnp.cos(ang)
    s = jnp.sin(ang)
    rope = (lane >= ROPE_LANE0) & (lane < ROPE_LANE0 + QK_ROPE_DIM)
    first_half = lane < ROPE_LANE0 + HALF_ROPE
    cos_ref[...] = jnp.where(lane < ROPE_LANE0, 1.0, jnp.where(rope, c, 0.0))
    sin_ref[...] = jnp.where(rope, jnp.where(first_half, -s, s), 0.0)


def _rope_tables(positions):
    n = positions.size
    inv_freq = ROPE_THETA ** (-jnp.arange(0, QK_ROPE_DIM, 2, dtype=f32) / QK_ROPE_DIM)
    pad = jnp.zeros((HEAD_PAD - ROPE_LANE0 - QK_ROPE_DIM,), f32)
    invf = jnp.concatenate([jnp.zeros((ROPE_LANE0,), f32), inv_freq, inv_freq, pad]).reshape(1, HEAD_PAD)
    out = jax.ShapeDtypeStruct((n, HEAD_PAD), f32)
    return pl.pallas_call(
        _rope_table_kernel,
        grid=(n // ROPE_BLOCK,),
        in_specs=[pl.BlockSpec((ROPE_BLOCK, 1), lambda i: (i, 0)),
                  pl.BlockSpec((1, HEAD_PAD), lambda i: (0, 0))],
        out_specs=[pl.BlockSpec((ROPE_BLOCK, HEAD_PAD), lambda i: (i, 0))] * 2,
        out_shape=[out, out],
        name="rope_tables",
    )(positions.reshape(n, 1), invf)


def _front_kernel(x_ref, cos_ref, sin_ref, norm_pre_ref, w_head_ref, w_kr_ref, w_tail_ref, b_gate_ref, pool_w_ref,
                  pool_scale_ref, q_norm_ref, w_uq_ref, kv_norm_ref, w_ukv_ref, vn_g_ref,
                  vn_b_ref, w_s_ref, bs_ref, w_oa_ref, w_oc_ref,
                  q_ref, k_ref, v_ref, part_ref, g1_ref, sb_ref, abuf, lv_ref):
    tb = x_ref.shape[0]
    j = pl.program_id(1)

    hb = _rms(x_ref[...], norm_pre_ref[...]).astype(bf16)

    def proj(c0, width):
        if c0 + width <= COL_KR:
            w = w_head_ref[:, c0:c0 + width]
        elif c0 == COL_KR and width == HEAD_PAD:
            w = w_kr_ref[...]
        else:
            assert c0 >= COL_B_GATE
            w = w_tail_ref[:, c0 - COL_B_GATE:c0 - COL_B_GATE + width]
        return _dot(hb, w)

    @pl.when(j == 0)
    def _():
        abuf[0:POOL_TOP, :] = jnp.zeros((POOL_TOP, POOL_WIDTH), f32)
        lv_ref[:, 0:POOL_PAD, :] = jnp.zeros((lv_ref.shape[0], POOL_PAD, POOL_GROUP_DIM), f32)

    a_in = proj(COL_A_IN, POOL_WIDTH)
    abuf[POOL_TOP:POOL_TOP + tb, :] = a_in
    cq = proj(COL_CQ, Q_LORA_RANK)
    ckv = proj(COL_CKV, KV_LORA_RANK)
    kr = proj(COL_KR, HEAD_PAD)
    vv = proj(COL_V, GMLP_WIDTH)
    a_gate = proj(COL_A_GATE, POOL_WIDTH)
    b_gate = proj(COL_B_GATE, MLA_WIDTH)
    u = proj(COL_U, GMLP_WIDTH)
    c_gate = proj(COL_C_GATE, GMLP_WIDTH)

    t = j * tb + lax.broadcasted_iota(jnp.int32, (tb, 1), 0)
    ys = []
    level = 0
    for g, w in enumerate(POOL_WINDOWS):
        cols = slice(g * POOL_GROUP_DIM, (g + 1) * POOL_GROUP_DIM)
        src = lambda lo, n: abuf[lo:lo + n, cols]
        shift = 1
        while 2 * shift < w:
            lo, n = POOL_PAD, HALO + tb
            dst = level
            lv_ref[dst, lo:lo + n, :] = src(lo, n) + src(lo - shift, n)
            src = lambda lo, n, dst=dst: lv_ref[dst, lo:lo + n, :]
            level += 1
            shift *= 2
        win = src(POOL_TOP, tb) + src(POOL_TOP - shift, tb)
        count = jnp.minimum(t + 1, w).astype(f32)
        mixed = win / count - a_in[:, cols]
        ys.append(_dot(mixed.astype(bf16), pool_w_ref[g]))
    abuf[POOL_PAD:POOL_TOP, :] = abuf[tb + POOL_PAD:tb + POOL_TOP, :]

    cqn = _rms(cq, q_norm_ref[...]).astype(bf16)
    q2 = _dot(cqn, w_uq_ref[...])
    ckvn = _rms(ckv, kv_norm_ref[...]).astype(bf16)
    kv2 = _dot(ckvn, w_ukv_ref[...])

    vc = vv - jnp.mean(vv, axis=-1, keepdims=True)
    var = jnp.mean(vc * vc, axis=-1, keepdims=True)
    vb = (vc * lax.rsqrt(var + EPS) * vn_g_ref[...] + vn_b_ref[...]).astype(bf16)
    row = lax.broadcasted_iota(jnp.int32, (GMLP_CHUNK, GMLP_CHUNK), 0)
    col = lax.broadcasted_iota(jnp.int32, (GMLP_CHUNK, GMLP_CHUNK), 1)
    ws = [jnp.where(col <= row, w_s_ref[g], jnp.zeros((), bf16)) for g in range(GMLP_GROUPS)]
    groups_per_tile = V7X_LANES // GMLP_GROUP_DIM
    assert groups_per_tile == 2
    chunks = []
    for c in range(tb // GMLP_CHUNK):
        tiles = []
        for tile in range(GMLP_WIDTH // V7X_LANES):
            rhs = vb[c * GMLP_CHUNK:(c + 1) * GMLP_CHUNK, tile * V7X_LANES:(tile + 1) * V7X_LANES]
            lo = _dot(ws[2 * tile], rhs)
            hi = _dot(ws[2 * tile + 1], rhs)
            tiles.append(jnp.where(col < GMLP_GROUP_DIM, lo, hi))
        chunks.append(jnp.concatenate(tiles, axis=1) + bs_ref[...])
    sv = jnp.concatenate(chunks, axis=0)

    gate_pre = {i: proj(COL_GATES + i * D_MODEL, D_MODEL) for i in (0, 2)}

    cos_t = cos_ref[...]
    sin_t = sin_ref[...]
    lane = lax.broadcasted_iota(jnp.int32, (tb, HEAD_PAD), 1)
    first_half = lane < ROPE_LANE0 + HALF_ROPE

    def rotate(t):
        swapped = jnp.where(first_half, pltpu.roll(t, HEAD_PAD - HALF_ROPE, 1), pltpu.roll(t, HALF_ROPE, 1))
        return t * cos_t + swapped * sin_t

    for h in range(MLA_HEADS):
        q_ref[h] = (rotate(q2[:, h * HEAD_PAD:(h + 1) * HEAD_PAD]) * (SOFTMAX_SCALE * LOG2E)).astype(bf16)
    k_rope = rotate(kr)
    low = lane < QK_NOPE_DIM
    ones_lane = jnp.where(lane == ONES_LANE, 1.0, 0.0)
    v0 = MLA_HEADS * QK_NOPE_DIM
    for pair in range(MLA_HEADS // 2):
        k_pair = kv2[:, pair * HEAD_PAD:(pair + 1) * HEAD_PAD]
        v_pair = kv2[:, v0 + pair * HEAD_PAD:v0 + (pair + 1) * HEAD_PAD]
        for odd in range(2):
            k_h = pltpu.roll(k_pair, QK_NOPE_DIM, 1) if odd else k_pair
            v_h = pltpu.roll(v_pair, V_HEAD_DIM, 1) if odd else v_pair
            k_ref[2 * pair + odd] = (jnp.where(low, k_h, 0.0) + k_rope).astype(bf16)
            v_ref[2 * pair + odd] = jnp.where(low, v_h, ones_lane).astype(bf16)
    sb_ref[...] = _silu(b_gate).astype(sb_ref.dtype)

    y_a = jnp.concatenate(ys, axis=1) * pool_scale_ref[...] * _silu(a_gate)
    y_c = u * sv * _silu(c_gate)

    def gate(i):
        return _sigmoid(gate_pre[i] + b_gate_ref[:, i * D_MODEL:(i + 1) * D_MODEL])

    part_ref[...] = (gate(0) * _dot(y_a.astype(bf16), w_oa_ref[...])
                     + gate(2) * _dot(y_c.astype(bf16), w_oc_ref[...])).astype(part_ref.dtype)
    gate_pre[1] = proj(COL_GATES + D_MODEL, D_MODEL)
    g1_ref[...] = gate(1).astype(g1_ref.dtype)


def _const_spec(stacked, l):
    index = (l,) + (0,) * (stacked.ndim - 1)
    return pl.BlockSpec((None,) + stacked.shape[1:], lambda b, j: index, pipeline_mode=pl.Buffered(1))


def _front(x, cos_t, sin_t, p, l):
    B, S, _ = x.shape
    tb = FRONT_BLOCK
    tok = lambda width: pl.BlockSpec((None, tb, width), lambda b, j: (b, j, 0))
    heads = pl.BlockSpec((None, MLA_HEADS, tb, HEAD_PAD), lambda b, j: (b, 0, j, 0))
    consts = [p["norm_pre"], p["w_head"], p["w_kr"], p["w_tail"], p["b_gate"], p["pool_w"],
              p["pool_scale"], p["q_norm"],
              p["w_uq"], p["kv_norm"], p["w_ukv"], p["v_norm_g"], p["v_norm_b"], p["w_s"],
              p["bs_full"], p["w_oa"], p["w_oc"]]
    head_shape = jax.ShapeDtypeStruct((B, MLA_HEADS, S, HEAD_PAD), bf16)
    return pl.pallas_call(
        _front_kernel,
        grid=(B, S // tb),
        in_specs=[tok(D_MODEL), tok(HEAD_PAD), tok(HEAD_PAD)] + [_const_spec(c, l) for c in consts],
        out_specs=[heads, heads, heads, tok(D_MODEL), tok(D_MODEL), tok(MLA_WIDTH)],
        out_shape=[head_shape, head_shape, head_shape,
                   jax.ShapeDtypeStruct((B, S, D_MODEL), bf16),
                   jax.ShapeDtypeStruct((B, S, D_MODEL), bf16),
                   jax.ShapeDtypeStruct((B, S, MLA_WIDTH), bf16)],
        scratch_shapes=[pltpu.VMEM((POOL_TOP + tb, POOL_WIDTH), f32),
                        pltpu.VMEM((POOL_LEVELS, POOL_TOP + tb, POOL_GROUP_DIM), f32)],
        compiler_params=pltpu.CompilerParams(
            dimension_semantics=("arbitrary", "arbitrary"),
            vmem_limit_bytes=FRONT_VMEM_BYTES),
        name="layer_front",
    )(x, cos_t, sin_t, *consts)


def _attn_kernel(q_ref, k_ref, v_ref, o_ref, s_ref, p_ref, m_ref, acc_ref):
    hp, tq2, _ = q_ref.shape
    tq = tq2 // 2
    u = pl.program_id(2)
    lane = lax.broadcasted_iota(jnp.int32, (tq, HEAD_PAD), 1)
    n_tiles = tq // V7X_LANES
    half = tq // 2
    nt_dims = (((1,), (1,)), ((), ()))

    def init_state():
        m_ref[...] = jnp.full(m_ref.shape, -jnp.inf, f32)
        acc_ref[...] = jnp.zeros(acc_ref.shape, f32)

    def scores(q0, hh, j, slot, diagonal=False):
        start = pl.multiple_of(j * tq, tq)
        if diagonal:
            s_ref[slot, hh, 0:half, 0:half] = lax.dot_general(
                q_ref[hh, q0:q0 + half], k_ref[hh, pl.ds(start, half), :], nt_dims, preferred_element_type=f32)
            s_ref[slot, hh, half:tq, :] = lax.dot_general(
                q_ref[hh, q0 + half:q0 + tq], k_ref[hh, pl.ds(start, tq), :], nt_dims, preferred_element_type=f32)
        else:
            s_ref[slot, hh] = lax.dot_general(q_ref[hh, q0:q0 + tq], k_ref[hh, pl.ds(start, tq), :], nt_dims,
                                              preferred_element_type=f32)

    def softmax_pv(hh, j, slot, diagonal):
        for r in range(tq // SOFTMAX_ROWS):
            r0 = r * SOFTMAX_ROWS
            rows = slice(r0, r0 + SOFTMAX_ROWS)
            live = (r0 + SOFTMAX_ROWS - 1) // V7X_LANES + 1 if diagonal else n_tiles
            read = (half if r0 < half else tq) // V7X_LANES if diagonal else n_tiles

            def tile(c):
                s = s_ref[slot, hh, rows, c * V7X_LANES:(c + 1) * V7X_LANES]
                if diagonal and (c + 1) * V7X_LANES - 1 > r0:
                    row = r0 + lax.broadcasted_iota(jnp.int32, s.shape, 0)
                    col = c * V7X_LANES + lax.broadcasted_iota(jnp.int32, s.shape, 1)
                    s = jnp.where(col <= row, s, -jnp.inf)
                return s

            m_blk = tile(0)
            for c in range(1, live):
                m_blk = jnp.maximum(m_blk, tile(c))
            m_old = m_ref[hh, rows]
            m_new = jnp.maximum(m_old, jnp.max(m_blk, axis=-1, keepdims=True))
            m_ref[hh, rows] = m_new
            acc_ref[hh, rows] = jnp.exp2(m_old - m_new) * acc_ref[hh, rows]
            for c in range(live):
                p_ref[hh, rows, c * V7X_LANES:(c + 1) * V7X_LANES] = jnp.exp2(tile(c) - m_new).astype(bf16)
            if live < read:
                p_ref[hh, rows, live * V7X_LANES:read * V7X_LANES] = jnp.zeros(
                    (SOFTMAX_ROWS, (read - live) * V7X_LANES), bf16)
        start = pl.multiple_of(j * tq, tq)
        if diagonal:
            acc_ref[hh, 0:half] += _dot(p_ref[hh, 0:half, 0:half], v_ref[hh, pl.ds(start, half), :])
            acc_ref[hh, half:tq] += _dot(p_ref[hh, half:tq, :], v_ref[hh, pl.ds(start, tq), :])
        else:
            acc_ref[hh] += _dot(p_ref[hh], v_ref[hh, pl.ds(start, tq), :])

    def step(q0, j, slot, next_diagonal=False):
        for hh in range(hp):
            softmax_pv(hh, j, slot, False)
            scores(q0, hh, j + 1, 1 - slot, next_diagonal)

    def last_step(i, slot, next_pass_scores=None):
        for hh in range(hp):
            softmax_pv(hh, i, slot, True)
            if next_pass_scores is not None:
                next_pass_scores(hh)

    def finish(q0):
        outs = [acc_ref[hh] / acc_ref[hh, :, ONES_LANE:ONES_LANE + 1] for hh in range(hp)]
        o_ref[q0:q0 + tq, :] = jnp.where(lane < V_HEAD_DIM, outs[0],
                                         pltpu.roll(outs[1], V_HEAD_DIM, 1)).astype(o_ref.dtype)

    def pairs(q0, first_slot, trips):
        def body(t, carry):
            step(q0, 2 * t, first_slot)
            step(q0, 2 * t + 1, 1 - first_slot)
            return carry

        lax.fori_loop(0, trips, body, 0)

    i_even = 2 * u
    odd_first_scores = lambda hh: scores(tq, hh, 0, 1)
    init_state()

    @pl.when(u == 0)
    def _():
        for hh in range(hp):
            scores(0, hh, 0, 0, True)
        last_step(i_even, 0, odd_first_scores)

    @pl.when(u > 0)
    def _():
        for hh in range(hp):
            scores(0, hh, 0, 0)
        pairs(0, 0, u - 1)
        step(0, i_even - 2, 0)
        step(0, i_even - 1, 1, True)
        last_step(i_even, 0, odd_first_scores)

    finish(0)

    i_odd = i_even + 1
    init_state()
    pairs(tq, 1, u)
    step(tq, i_odd - 1, 1, True)
    last_step(i_odd, 0)
    finish(tq)


def _attention(q, k, v):
    B, H, S, _ = q.shape
    tq = ATTN_BLOCK
    hp = HEADS_PER_STEP
    assert hp * V_HEAD_DIM == V7X_LANES
    return pl.pallas_call(
        _attn_kernel,
        grid=(B, H // hp, S // (2 * tq)),
        in_specs=[pl.BlockSpec((None, hp, 2 * tq, HEAD_PAD), lambda b, g, u: (b, g, u, 0)),
                  pl.BlockSpec((None, hp, S, HEAD_PAD), lambda b, g, u: (b, g, 0, 0)),
                  pl.BlockSpec((None, hp, S, HEAD_PAD), lambda b, g, u: (b, g, 0, 0))],
        out_specs=pl.BlockSpec((None, 2 * tq, hp * V_HEAD_DIM), lambda b, g, u: (b, u, g)),
        out_shape=jax.ShapeDtypeStruct((B, S, H * V_HEAD_DIM), bf16),
        scratch_shapes=[pltpu.VMEM((2, hp, tq, tq), f32),
                        pltpu.VMEM((hp, tq, tq), bf16),
                        pltpu.VMEM((hp, tq, V7X_LANES), f32),
                        pltpu.VMEM((hp, tq, HEAD_PAD), f32)],
        compiler_params=pltpu.CompilerParams(
            dimension_semantics=("arbitrary", "arbitrary", "arbitrary"),
            vmem_limit_bytes=ATTN_VMEM_BYTES),
        name="mla_attention",
    )(q, k, v)


def _back_kernel(x_ref, o_ref, sb_ref, g1_ref, part_ref, w_ob_ref, w_out_ref, norm_post_ref,
                 out_ref):
    y_b = o_ref[...] * sb_ref[...]
    merged = part_ref[...].astype(f32) + g1_ref[...].astype(f32) * _dot(y_b, w_ob_ref[...])
    z = _dot(merged.astype(bf16), w_out_ref[...])
    out_ref[...] = x_ref[...] + _rms(z, norm_post_ref[...])


def _back(x, o, sb, g1, part, p, l):
    B, S, _ = x.shape
    tb = BACK_BLOCK
    tok = lambda width: pl.BlockSpec((None, tb, width), lambda b, j: (b, j, 0))
    consts = [p["w_ob"], p["w_out"], p["norm_post"]]
    return pl.pallas_call(
        _back_kernel,
        grid=(B, S // tb),
        in_specs=[tok(D_MODEL), tok(MLA_WIDTH), tok(MLA_WIDTH), tok(D_MODEL), tok(D_MODEL)]
        + [_const_spec(c, l) for c in consts],
        out_specs=tok(D_MODEL),
        out_shape=jax.ShapeDtypeStruct((B, S, D_MODEL), f32),
        compiler_params=pltpu.CompilerParams(
            dimension_semantics=("arbitrary", "arbitrary"),
            vmem_limit_bytes=BACK_VMEM_BYTES),
        name="layer_back",
    )(x, o, sb, g1, part, *consts)


def _stacked_params(norm_pre, norm_post, w_in, b_gate, pool_w, pool_scale, q_norm, w_uq,
                    kv_norm, w_ukv, v_norm_g, v_norm_b, w_s, b_s, w_oa, w_ob, w_oc, w_out):
    L = w_in.shape[0]
    zeros = lambda *shape: jnp.zeros(shape, bf16)
    kr_src = sum(IN_SPLITS[:4])
    assert kr_src == COL_KR
    w_head = w_in[..., :kr_src].astype(bf16)
    w_kr = jnp.concatenate([zeros(L, D_MODEL, ROPE_LANE0),
                            w_in[..., kr_src:kr_src + QK_ROPE_DIM].astype(bf16),
                            zeros(L, D_MODEL, HEAD_PAD - ROPE_LANE0 - QK_ROPE_DIM)], axis=-1)
    w_tail = w_in[..., kr_src + QK_ROPE_DIM:].astype(bf16)
    assert w_tail.shape[-1] == IN_WIDTH_PAD - COL_B_GATE

    uq = w_uq.astype(bf16).reshape(L, Q_LORA_RANK, MLA_HEADS, QK_NOPE_DIM + QK_ROPE_DIM)
    nope, x1, x2 = uq[..., :QK_NOPE_DIM], uq[..., QK_NOPE_DIM:QK_NOPE_DIM + HALF_ROPE], uq[..., QK_NOPE_DIM + HALF_ROPE:]
    pad = zeros(L, Q_LORA_RANK, MLA_HEADS, HEAD_PAD - QK_NOPE_DIM - QK_ROPE_DIM)
    w_uq_p = jnp.concatenate([nope, x1, x2, pad], axis=-1).reshape(L, Q_LORA_RANK, MLA_HEADS * HEAD_PAD)

    assert 2 * QK_NOPE_DIM == HEAD_PAD and V_HEAD_DIM == QK_NOPE_DIM
    ukv = w_ukv.astype(bf16).reshape(L, KV_LORA_RANK, MLA_HEADS, QK_NOPE_DIM + V_HEAD_DIM)
    k_cols = ukv[..., :QK_NOPE_DIM].reshape(L, KV_LORA_RANK, MLA_HEADS * QK_NOPE_DIM)
    v_cols = ukv[..., QK_NOPE_DIM:].reshape(L, KV_LORA_RANK, MLA_HEADS * V_HEAD_DIM)
    w_ukv_p = jnp.concatenate([k_cols, v_cols], axis=-1)

    row = lambda a: a.reshape(L, 1, -1)
    return {
        "norm_pre": row(norm_pre), "norm_post": row(norm_post),
        "w_head": w_head, "w_kr": w_kr, "w_tail": w_tail,
        "b_gate": row(b_gate), "pool_w": pool_w.astype(bf16), "pool_scale": row(pool_scale),
        "q_norm": row(q_norm), "w_uq": w_uq_p, "kv_norm": row(kv_norm), "w_ukv": w_ukv_p,
        "v_norm_g": row(v_norm_g), "v_norm_b": row(v_norm_b), "w_s": w_s.astype(bf16),
        "bs_full": jnp.repeat(jnp.swapaxes(b_s, 1, 2), GMLP_GROUP_DIM, axis=2),
        "w_oa": w_oa.astype(bf16), "w_ob": w_ob.astype(bf16), "w_oc": w_oc.astype(bf16),
        "w_out": w_out.astype(bf16),
    }


def kernel(x, positions, norm_pre, norm_post, w_in, b_gate, pool_w, pool_scale, q_norm, w_uq, kv_norm, w_ukv, v_norm_g, v_norm_b, w_s, b_s, w_oa, w_ob, w_oc, w_out):
    B, S, _ = x.shape
    cos_t, sin_t = _rope_tables(positions)
    cos_t = cos_t.reshape(B, S, HEAD_PAD)
    sin_t = sin_t.reshape(B, S, HEAD_PAD)
    p = _stacked_params(norm_pre, norm_post, w_in, b_gate, pool_w, pool_scale, q_norm, w_uq,
                        kv_norm, w_ukv, v_norm_g, v_norm_b, w_s, b_s, w_oa, w_ob, w_oc, w_out)
    for l in range(norm_pre.shape[0]):
        q, k, v, part, g1, sb = _front(x, cos_t, sin_t, p, l)
        o = _attention(q, k, v)
        x = _back(x, o, sb, g1, part, p, l)
    return x
```

```python
import math

import jax
import jax.numpy as jnp
from jax import lax
from jax.experimental import pallas as pl
from jax.experimental.pallas import tpu as pltpu

D_MODEL = 1024
POOL_WIDTH = 512
POOL_GROUPS = 4
POOL_GROUP_DIM = POOL_WIDTH // POOL_GROUPS
POOL_WINDOWS = (2, 4, 8, 16)
MLA_HEADS = 8
QK_NOPE_DIM = 64
QK_ROPE_DIM = 32
HALF_ROPE = QK_ROPE_DIM // 2
V_HEAD_DIM = 64
Q_LORA_RANK = 256
KV_LORA_RANK = 128
MLA_WIDTH = MLA_HEADS * V_HEAD_DIM
ROPE_THETA = 10000.0
GMLP_WIDTH = 512
GMLP_CHUNK = 128
GMLP_GROUPS = 8
GMLP_GROUP_DIM = GMLP_WIDTH // GMLP_GROUPS
N_BRANCHES = 3
EPS = 1e-6
IN_SPLITS = (POOL_WIDTH, POOL_WIDTH, Q_LORA_RANK, KV_LORA_RANK, QK_ROPE_DIM, MLA_WIDTH,
             GMLP_WIDTH, GMLP_WIDTH, GMLP_WIDTH, N_BRANCHES * D_MODEL)

V7X_LANES = 128
V7X_SUBLANES = 8
V7X_VMEM_BYTES = 64 * 1024 * 1024

HEAD_PAD = V7X_LANES
ROPE_LANE0 = QK_NOPE_DIM
ONES_LANE = V_HEAD_DIM
HALO = max(POOL_WINDOWS)
POOL_PAD = V7X_SUBLANES
POOL_TOP = POOL_PAD + HALO
assert all(w & (w - 1) == 0 and w <= HALO for w in POOL_WINDOWS)
POOL_LEVELS = sum(w.bit_length() - 2 for w in POOL_WINDOWS)

COL_A_IN = 0
COL_A_GATE = COL_A_IN + POOL_WIDTH
COL_CQ = COL_A_GATE + POOL_WIDTH
COL_CKV = COL_CQ + Q_LORA_RANK
COL_KR = COL_CKV + KV_LORA_RANK
COL_B_GATE = COL_KR + HEAD_PAD
COL_U = COL_B_GATE + MLA_WIDTH
COL_V = COL_U + GMLP_WIDTH
COL_C_GATE = COL_V + GMLP_WIDTH
COL_GATES = COL_C_GATE + GMLP_WIDTH
IN_WIDTH_PAD = COL_GATES + N_BRANCHES * D_MODEL

SOFTMAX_SCALE = (QK_NOPE_DIM + QK_ROPE_DIM) ** -0.5
LOG2E = math.log2(math.e)

FRONT_BLOCK = 512
FRONT_VMEM_BYTES = V7X_VMEM_BYTES * 7 // 8
ATTN_BLOCK = 1024
ATTN_PASSES = 4
ATTN_VMEM_BYTES = V7X_VMEM_BYTES * 15 // 16
BACK_BLOCK = 1024
BACK_VMEM_BYTES = V7X_VMEM_BYTES * 3 // 4
ROPE_BLOCK = 1024
HEADS_PER_STEP = 2
SOFTMAX_ROWS = 64

bf16 = jnp.bfloat16
f32 = jnp.float32


def _sigmoid(z):
    return 0.5 * jnp.tanh(0.5 * z) + 0.5


def _silu(z):
    return z * _sigmoid(z)


def _rms(x, g):
    return x * lax.rsqrt(jnp.mean(x * x, axis=-1, keepdims=True) + EPS) * g


def _dot(a, b):
    return jnp.dot(a, b, preferred_element_type=f32)


def _rope_table_kernel(pos_ref, invf_ref, cos_ref, sin_ref):
    lane = lax.broadcasted_iota(jnp.int32, cos_ref.shape, 1)
    ang = pos_ref[...].astype(f32) * invf_ref[...]
    c = jnp.cos(ang)
    s = jnp.sin(ang)
    rope = (lane >= ROPE_LANE0) & (lane < ROPE_LANE0 + QK_ROPE_DIM)
    first_half = lane < ROPE_LANE0 + HALF_ROPE
    cos_ref[...] = jnp.where(lane < ROPE_LANE0, 1.0, jnp.where(rope, c, 0.0))
    sin_ref[...] = jnp.where(rope, jnp.where(first_half, -s, s), 0.0)


def _rope_tables(positions):
    n = positions.size
    inv_freq = ROPE_THETA ** (-jnp.arange(0, QK_ROPE_DIM, 2, dtype=f32) / QK_ROPE_DIM)
    pad = jnp.zeros((HEAD_PAD - ROPE_LANE0 - QK_ROPE_DIM,), f32)
    invf = jnp.concatenate([jnp.zeros((ROPE_LANE0,), f32), inv_freq, inv_freq, pad]).reshape(1, HEAD_PAD)
    out = jax.ShapeDtypeStruct((n, HEAD_PAD), f32)
    return pl.pallas_call(
        _rope_table_kernel,
        grid=(n // ROPE_BLOCK,),
        in_specs=[pl.BlockSpec((ROPE_BLOCK, 1), lambda i: (i, 0)),
                  pl.BlockSpec((1, HEAD_PAD), lambda i: (0, 0))],
        out_specs=[pl.BlockSpec((ROPE_BLOCK, HEAD_PAD), lambda i: (i, 0))] * 2,
        out_shape=[out, out],
        name="rope_tables",
    )(positions.reshape(n, 1), invf)


def _front_kernel(x_ref, cos_ref, sin_ref, norm_pre_ref, w_head_ref, w_kr_ref, w_tail_ref, b_gate_ref, pool_w_ref,
                  pool_scale_ref, q_norm_ref, w_uq_ref, kv_norm_ref, w_ukv_ref, vn_g_ref,
                  vn_b_ref, w_s_ref, bs_ref, w_oa_ref, w_oc_ref,
                  q_ref, k_ref, v_ref, part_ref, g1_ref, sb_ref, abuf, lv_ref):
    tb = x_ref.shape[0]
    j = pl.program_id(1)

    hb = _rms(x_ref[...], norm_pre_ref[...]).astype(bf16)

    def proj(c0, width):
        if c0 + width <= COL_KR:
            w = w_head_ref[:, c0:c0 + width]
        elif c0 == COL_KR and width == HEAD_PAD:
            w = w_kr_ref[...]
        else:
            assert c0 >= COL_B_GATE
            w = w_tail_ref[:, c0 - COL_B_GATE:c0 - COL_B_GATE + width]
        return _dot(hb, w)

    @pl.when(j == 0)
    def _():
        abuf[0:POOL_TOP, :] = jnp.zeros((POOL_TOP, POOL_WIDTH), f32)
        lv_ref[:, 0:POOL_PAD, :] = jnp.zeros((lv_ref.shape[0], POOL_PAD, POOL_GROUP_DIM), f32)

    a_in = proj(COL_A_IN, POOL_WIDTH)
    abuf[POOL_TOP:POOL_TOP + tb, :] = a_in
    cq = proj(COL_CQ, Q_LORA_RANK)
    ckv = proj(COL_CKV, KV_LORA_RANK)
    kr = proj(COL_KR, HEAD_PAD)
    vv = proj(COL_V, GMLP_WIDTH)
    a_gate = proj(COL_A_GATE, POOL_WIDTH)
    b_gate = proj(COL_B_GATE, MLA_WIDTH)
    u = proj(COL_U, GMLP_WIDTH)
    c_gate = proj(COL_C_GATE, GMLP_WIDTH)

    t = j * tb + lax.broadcasted_iota(jnp.int32, (tb, 1), 0)
    ys = []
    level = 0
    for g, w in enumerate(POOL_WINDOWS):
        cols = slice(g * POOL_GROUP_DIM, (g + 1) * POOL_GROUP_DIM)
        src = lambda lo, n: abuf[lo:lo + n, cols]
        shift = 1
        while 2 * shift < w:
            lo, n = POOL_PAD, HALO + tb
            dst = level
            lv_ref[dst, lo:lo + n, :] = src(lo, n) + src(lo - shift, n)
            src = lambda lo, n, dst=dst: lv_ref[dst, lo:lo + n, :]
            level += 1
            shift *= 2
        win = src(POOL_TOP, tb) + src(POOL_TOP - shift, tb)
        count = jnp.minimum(t + 1, w).astype(f32)
        mixed = win / count - a_in[:, cols]
        ys.append(_dot(mixed.astype(bf16), pool_w_ref[g]))
    abuf[POOL_PAD:POOL_TOP, :] = abuf[tb + POOL_PAD:tb + POOL_TOP, :]

    cqn = _rms(cq, q_norm_ref[...]).astype(bf16)
    q2 = _dot(cqn, w_uq_ref[...])
    ckvn = _rms(ckv, kv_norm_ref[...]).astype(bf16)
    kv2 = _dot(ckvn, w_ukv_ref[...])

    vc = vv - jnp.mean(vv, axis=-1, keepdims=True)
    var = jnp.mean(vc * vc, axis=-1, keepdims=True)
    vb = (vc * lax.rsqrt(var + EPS) * vn_g_ref[...] + vn_b_ref[...]).astype(bf16)
    row = lax.broadcasted_iota(jnp.int32, (GMLP_CHUNK, GMLP_CHUNK), 0)
    col = lax.broadcasted_iota(jnp.int32, (GMLP_CHUNK, GMLP_CHUNK), 1)
    ws = [jnp.where(col <= row, w_s_ref[g], jnp.zeros((), bf16)) for g in range(GMLP_GROUPS)]
    groups_per_tile = V7X_LANES // GMLP_GROUP_DIM
    assert groups_per_tile == 2
    chunks = []
    for c in range(tb // GMLP_CHUNK):
        tiles = []
        for tile in range(GMLP_WIDTH // V7X_LANES):
            rhs = vb[c * GMLP_CHUNK:(c + 1) * GMLP_CHUNK, tile * V7X_LANES:(tile + 1) * V7X_LANES]
            lo = _dot(ws[2 * tile], rhs)
            hi = _dot(ws[2 * tile + 1], rhs)
            tiles.append(jnp.where(col < GMLP_GROUP_DIM, lo, hi))
        chunks.append(jnp.concatenate(tiles, axis=1) + bs_ref[...])
    sv = jnp.concatenate(chunks, axis=0)

    gate_pre = {i: proj(COL_GATES + i * D_MODEL, D_MODEL) for i in (0, 2)}

    cos_t = cos_ref[...]
    sin_t = sin_ref[...]
    lane = lax.broadcasted_iota(jnp.int32, (tb, HEAD_PAD), 1)
    first_half = lane < ROPE_LANE0 + HALF_ROPE

    def rotate(t):
        swapped = jnp.where(first_half, pltpu.roll(t, HEAD_PAD - HALF_ROPE, 1), pltpu.roll(t, HALF_ROPE, 1))
        return t * cos_t + swapped * sin_t

    for h in range(MLA_HEADS):
        q_ref[h] = (rotate(q2[:, h * HEAD_PAD:(h + 1) * HEAD_PAD]) * (SOFTMAX_SCALE * LOG2E)).astype(bf16)
    k_rope = rotate(kr)
    low = lane < QK_NOPE_DIM
    ones_lane = jnp.where(lane == ONES_LANE, 1.0, 0.0)
    v0 = MLA_HEADS * QK_NOPE_DIM
    for pair in range(MLA_HEADS // 2):
        k_pair = kv2[:, pair * HEAD_PAD:(pair + 1) * HEAD_PAD]
        v_pair = kv2[:, v0 + pair * HEAD_PAD:v0 + (pair + 1) * HEAD_PAD]
        for odd in range(2):
            k_h = pltpu.roll(k_pair, QK_NOPE_DIM, 1) if odd else k_pair
            v_h = pltpu.roll(v_pair, V_HEAD_DIM, 1) if odd else v_pair
            k_ref[2 * pair + odd] = (jnp.where(low, k_h, 0.0) + k_rope).astype(bf16)
            v_ref[2 * pair + odd] = jnp.where(low, v_h, ones_lane).astype(bf16)
    sb_ref[...] = _silu(b_gate).astype(sb_ref.dtype)

    y_a = jnp.concatenate(ys, axis=1) * pool_scale_ref[...] * _silu(a_gate)
    y_c = u * sv * _silu(c_gate)

    def gate(i):
        return _sigmoid(gate_pre[i] + b_gate_ref[:, i * D_MODEL:(i + 1) * D_MODEL])

    part_ref[...] = (gate(0) * _dot(y_a.astype(bf16), w_oa_ref[...])
                     + gate(2) * _dot(y_c.astype(bf16), w_oc_ref[...])).astype(part_ref.dtype)
    gate_pre[1] = proj(COL_GATES + D_MODEL, D_MODEL)
    g1_ref[...] = gate(1).astype(g1_ref.dtype)


def _const_spec(stacked, l):
    index = (l,) + (0,) * (stacked.ndim - 1)
    return pl.BlockSpec((None,) + stacked.shape[1:], lambda b, j: index, pipeline_mode=pl.Buffered(1))


def _front(x, cos_t, sin_t, p, l):
    B, S, _ = x.shape
    tb = FRONT_BLOCK
    tok = lambda width: pl.BlockSpec((None, tb, width), lambda b, j: (b, j, 0))
    heads = pl.BlockSpec((None, MLA_HEADS, tb, HEAD_PAD), lambda b, j: (b, 0, j, 0))
    consts = [p["norm_pre"], p["w_head"], p["w_kr"], p["w_tail"], p["b_gate"], p["pool_w"],
              p["pool_scale"], p["q_norm"],
              p["w_uq"], p["kv_norm"], p["w_ukv"], p["v_norm_g"], p["v_norm_b"], p["w_s"],
              p["bs_full"], p["w_oa"], p["w_oc"]]
    head_shape = jax.ShapeDtypeStruct((B, MLA_HEADS, S, HEAD_PAD), bf16)
    return pl.pallas_call(
        _front_kernel,
        grid=(B, S // tb),
        in_specs=[tok(D_MODEL), tok(HEAD_PAD), tok(HEAD_PAD)] + [_const_spec(c, l) for c in consts],
        out_specs=[heads, heads, heads, tok(D_MODEL), tok(D_MODEL), tok(MLA_WIDTH)],
        out_shape=[head_shape, head_shape, head_shape,
                   jax.ShapeDtypeStruct((B, S, D_MODEL), bf16),
                   jax.ShapeDtypeStruct((B, S, D_MODEL), bf16),
                   jax.ShapeDtypeStruct((B, S, MLA_WIDTH), bf16)],
        scratch_shapes=[pltpu.VMEM((POOL_TOP + tb, POOL_WIDTH), f32),
                        pltpu.VMEM((POOL_LEVELS, POOL_TOP + tb, POOL_GROUP_DIM), f32)],
        compiler_params=pltpu.CompilerParams(
            dimension_semantics=("arbitrary", "arbitrary"),
            vmem_limit_bytes=FRONT_VMEM_BYTES),
        name="layer_front",
    )(x, cos_t, sin_t, *consts)


def _attn_kernel(q_ref, k_ref, v_ref, o_ref, s_ref, p_ref, m_ref, acc_ref):
    passes = ATTN_PASSES
    hp, rows, _ = q_ref.shape
    tq = rows // passes
    u = pl.program_id(2)
    lane = lax.broadcasted_iota(jnp.int32, (tq, HEAD_PAD), 1)
    n_tiles = tq // V7X_LANES
    half = tq // 2
    nt_dims = (((1,), (1,)), ((), ()))

    def init_state():
        m_ref[...] = jnp.full(m_ref.shape, -jnp.inf, f32)
        acc_ref[...] = jnp.zeros(acc_ref.shape, f32)

    def scores(q0, hh, j, slot, diagonal=False):
        start = pl.multiple_of(j * tq, tq)
        if diagonal:
            s_ref[slot, hh, 0:half, 0:half] = lax.dot_general(
                q_ref[hh, q0:q0 + half], k_ref[hh, pl.ds(start, half), :], nt_dims, preferred_element_type=f32)
            s_ref[slot, hh, half:tq, :] = lax.dot_general(
                q_ref[hh, q0 + half:q0 + tq], k_ref[hh, pl.ds(start, tq), :], nt_dims, preferred_element_type=f32)
        else:
            s_ref[slot, hh] = lax.dot_general(q_ref[hh, q0:q0 + tq], k_ref[hh, pl.ds(start, tq), :], nt_dims,
                                              preferred_element_type=f32)

    def softmax_pv(hh, j, slot, diagonal):
        for r in range(tq // SOFTMAX_ROWS):
            r0 = r * SOFTMAX_ROWS
            rows = slice(r0, r0 + SOFTMAX_ROWS)
            live = (r0 + SOFTMAX_ROWS - 1) // V7X_LANES + 1 if diagonal else n_tiles
            read = (half if r0 < half else tq) // V7X_LANES if diagonal else n_tiles

            def tile(c):
                s = s_ref[slot, hh, rows, c * V7X_LANES:(c + 1) * V7X_LANES]
                if diagonal and (c + 1) * V7X_LANES - 1 > r0:
                    row = r0 + lax.broadcasted_iota(jnp.int32, s.shape, 0)
                    col = c * V7X_LANES + lax.broadcasted_iota(jnp.int32, s.shape, 1)
                    s = jnp.where(col <= row, s, -jnp.inf)
                return s

            m_blk = tile(0)
            for c in range(1, live):
                m_blk = jnp.maximum(m_blk, tile(c))
            m_old = m_ref[hh, rows]
            m_new = jnp.maximum(m_old, jnp.max(m_blk, axis=-1, keepdims=True))
            m_ref[hh, rows] = m_new
            acc_ref[hh, rows] = jnp.exp2(m_old - m_new) * acc_ref[hh, rows]
            for c in range(live):
                p_ref[hh, rows, c * V7X_LANES:(c + 1) * V7X_LANES] = jnp.exp2(tile(c) - m_new).astype(bf16)
            if live < read:
                p_ref[hh, rows, live * V7X_LANES:read * V7X_LANES] = jnp.zeros(
                    (SOFTMAX_ROWS, (read - live) * V7X_LANES), bf16)
        start = pl.multiple_of(j * tq, tq)
        if diagonal:
            acc_ref[hh, 0:half] += _dot(p_ref[hh, 0:half, 0:half], v_ref[hh, pl.ds(start, half), :])
            acc_ref[hh, half:tq] += _dot(p_ref[hh, half:tq, :], v_ref[hh, pl.ds(start, tq), :])
        else:
            acc_ref[hh] += _dot(p_ref[hh], v_ref[hh, pl.ds(start, tq), :])

    def step(q0, j, slot, next_diagonal=False):
        for hh in range(hp):
            softmax_pv(hh, j, slot, False)
            scores(q0, hh, j + 1, 1 - slot, next_diagonal)

    def last_step(i, slot, next_pass_scores=None):
        for hh in range(hp):
            softmax_pv(hh, i, slot, True)
            if next_pass_scores is not None:
                next_pass_scores(hh)

    def finish(q0):
        outs = [acc_ref[hh] / acc_ref[hh, :, ONES_LANE:ONES_LANE + 1] for hh in range(hp)]
        o_ref[q0:q0 + tq, :] = jnp.where(lane < V_HEAD_DIM, outs[0],
                                         pltpu.roll(outs[1], V_HEAD_DIM, 1)).astype(o_ref.dtype)

    def pairs(q0, first_slot, trips):
        def body(t, carry):
            step(q0, 2 * t, first_slot)
            step(q0, 2 * t + 1, 1 - first_slot)
            return carry

        lax.fori_loop(0, trips, body, 0)

    flip = 0
    for k in range(passes):
        q0 = k * tq
        i = passes * u + k
        diag_slot = (k + flip) % 2
        next_flip = 1 - diag_slot
        if k + 1 < passes:
            next_first_scores = lambda hh, q1=q0 + tq, slot=next_flip: scores(q1, hh, 0, slot)
        else:
            next_first_scores = None
        init_state()

        def attend(i=i, q0=q0, flip=flip, diag_slot=diag_slot, hook=next_first_scores, odd=k % 2):
            if odd:
                pairs(q0, flip, (i - 1) // 2)
            else:
                pairs(q0, flip, (i - 2) // 2)
                step(q0, i - 2, flip)
            step(q0, i - 1, 1 - diag_slot, True)
            last_step(i, diag_slot, hook)

        if k == 0:
            @pl.when(u == 0)
            def _(i=i, q0=q0, diag_slot=diag_slot, hook=next_first_scores):
                for hh in range(hp):
                    scores(q0, hh, i, diag_slot, True)
                last_step(i, diag_slot, hook)

            @pl.when(u > 0)
            def _(q0=q0, flip=flip, attend=attend):
                for hh in range(hp):
                    scores(q0, hh, 0, flip)
                attend()
        else:
            attend()
        finish(q0)
        flip = next_flip


def _attention(q, k, v):
    B, H, S, _ = q.shape
    tq = ATTN_BLOCK
    hp = HEADS_PER_STEP
    assert hp * V_HEAD_DIM == V7X_LANES
    return pl.pallas_call(
        _attn_kernel,
        grid=(B, H // hp, S // (ATTN_PASSES * tq)),
        in_specs=[pl.BlockSpec((None, hp, ATTN_PASSES * tq, HEAD_PAD), lambda b, g, u: (b, g, u, 0)),
                  pl.BlockSpec((None, hp, S, HEAD_PAD), lambda b, g, u: (b, g, 0, 0)),
                  pl.BlockSpec((None, hp, S, HEAD_PAD), lambda b, g, u: (b, g, 0, 0))],
        out_specs=pl.BlockSpec((None, ATTN_PASSES * tq, hp * V_HEAD_DIM), lambda b, g, u: (b, u, g)),
        out_shape=jax.ShapeDtypeStruct((B, S, H * V_HEAD_DIM), bf16),
        scratch_shapes=[pltpu.VMEM((2, hp, tq, tq), f32),
                        pltpu.VMEM((hp, tq, tq), bf16),
                        pltpu.VMEM((hp, tq, V7X_LANES), f32),
                        pltpu.VMEM((hp, tq, HEAD_PAD), f32)],
        compiler_params=pltpu.CompilerParams(
            dimension_semantics=("arbitrary", "arbitrary", "arbitrary"),
            vmem_limit_bytes=ATTN_VMEM_BYTES),
        name="mla_attention",
    )(q, k, v)


def _back_kernel(x_ref, o_ref, sb_ref, g1_ref, part_ref, w_ob_ref, w_out_ref, norm_post_ref,
                 out_ref):
    y_b = o_ref[...] * sb_ref[...]
    merged = part_ref[...].astype(f32) + g1_ref[...].astype(f32) * _dot(y_b, w_ob_ref[...])
    z = _dot(merged.astype(bf16), w_out_ref[...])
    out_ref[...] = x_ref[...] + _rms(z, norm_post_ref[...])


def _back(x, o, sb, g1, part, p, l):
    B, S, _ = x.shape
    tb = BACK_BLOCK
    tok = lambda width: pl.BlockSpec((None, tb, width), lambda b, j: (b, j, 0))
    consts = [p["w_ob"], p["w_out"], p["norm_post"]]
    return pl.pallas_call(
        _back_kernel,
        grid=(B, S // tb),
        in_specs=[tok(D_MODEL), tok(MLA_WIDTH), tok(MLA_WIDTH), tok(D_MODEL), tok(D_MODEL)]
        + [_const_spec(c, l) for c in consts],
        out_specs=tok(D_MODEL),
        out_shape=jax.ShapeDtypeStruct((B, S, D_MODEL), f32),
        compiler_params=pltpu.CompilerParams(
            dimension_semantics=("arbitrary", "arbitrary"),
            vmem_limit_bytes=BACK_VMEM_BYTES),
        name="layer_back",
    )(x, o, sb, g1, part, *consts)


def _stacked_params(norm_pre, norm_post, w_in, b_gate, pool_w, pool_scale, q_norm, w_uq,
                    kv_norm, w_ukv, v_norm_g, v_norm_b, w_s, b_s, w_oa, w_ob, w_oc, w_out):
    L = w_in.shape[0]
    zeros = lambda *shape: jnp.zeros(shape, bf16)
    kr_src = sum(IN_SPLITS[:4])
    assert kr_src == COL_KR
    w_head = w_in[..., :kr_src].astype(bf16)
    w_kr = jnp.concatenate([zeros(L, D_MODEL, ROPE_LANE0),
                            w_in[..., kr_src:kr_src + QK_ROPE_DIM].astype(bf16),
                            zeros(L, D_MODEL, HEAD_PAD - ROPE_LANE0 - QK_ROPE_DIM)], axis=-1)
    w_tail = w_in[..., kr_src + QK_ROPE_DIM:].astype(bf16)
    assert w_tail.shape[-1] == IN_WIDTH_PAD - COL_B_GATE

    uq = w_uq.astype(bf16).reshape(L, Q_LORA_RANK, MLA_HEADS, QK_NOPE_DIM + QK_ROPE_DIM)
    nope, x1, x2 = uq[..., :QK_NOPE_DIM], uq[..., QK_NOPE_DIM:QK_NOPE_DIM + HALF_ROPE], uq[..., QK_NOPE_DIM + HALF_ROPE:]
    pad = zeros(L, Q_LORA_RANK, MLA_HEADS, HEAD_PAD - QK_NOPE_DIM - QK_ROPE_DIM)
    w_uq_p = jnp.concatenate([nope, x1, x2, pad], axis=-1).reshape(L, Q_LORA_RANK, MLA_HEADS * HEAD_PAD)

    assert 2 * QK_NOPE_DIM == HEAD_PAD and V_HEAD_DIM == QK_NOPE_DIM
    ukv = w_ukv.astype(bf16).reshape(L, KV_LORA_RANK, MLA_HEADS, QK_NOPE_DIM + V_HEAD_DIM)
    k_cols = ukv[..., :QK_NOPE_DIM].reshape(L, KV_LORA_RANK, MLA_HEADS * QK_NOPE_DIM)
    v_cols = ukv[..., QK_NOPE_DIM:].reshape(L, KV_LORA_RANK, MLA_HEADS * V_HEAD_DIM)
    w_ukv_p = jnp.concatenate([k_cols, v_cols], axis=-1)

    row = lambda a: a.reshape(L, 1, -1)
    return {
        "norm_pre": row(norm_pre), "norm_post": row(norm_post),
        "w_head": w_head, "w_kr": w_kr, "w_tail": w_tail,
        "b_gate": row(b_gate), "pool_w": pool_w.astype(bf16), "pool_scale": row(pool_scale),
        "q_norm": row(q_norm), "w_uq": w_uq_p, "kv_norm": row(kv_norm), "w_ukv": w_ukv_p,
        "v_norm_g": row(v_norm_g), "v_norm_b": row(v_norm_b), "w_s": w_s.astype(bf16),
        "bs_full": jnp.repeat(jnp.swapaxes(b_s, 1, 2), GMLP_GROUP_DIM, axis=2),
        "w_oa": w_oa.astype(bf16), "w_ob": w_ob.astype(bf16), "w_oc": w_oc.astype(bf16),
        "w_out": w_out.astype(bf16),
    }


def kernel(x, positions, norm_pre, norm_post, w_in, b_gate, pool_w, pool_scale, q_norm, w_uq, kv_norm, w_ukv, v_norm_g, v_norm_b, w_s, b_s, w_oa, w_ob, w_oc, w_out):
    B, S, _ = x.shape
    cos_t, sin_t = _rope_tables(positions)
    cos_t = cos_t.reshape(B, S, HEAD_PAD)
    sin_t = sin_t.reshape(B, S, HEAD_PAD)
    p = _stacked_params(norm_pre, norm_post, w_in, b_gate, pool_w, pool_scale, q_norm, w_uq,
                        kv_norm, w_ukv, v_norm_g, v_norm_b, w_s, b_s, w_oa, w_ob, w_oc, w_out)
    for l in range(norm_pre.shape[0]):
        q, k, v, part, g1, sb = _front(x, cos_t, sin_t, p, l)
        o = _attention(q, k, v)
        x = _back(x, o, sb, g1, part, p, l)
    return x
```

```python
import math

import jax
import jax.numpy as jnp
from jax import lax
from jax.experimental import pallas as pl
from jax.experimental.pallas import tpu as pltpu

D_MODEL = 1024
POOL_WIDTH = 512
POOL_GROUPS = 4
POOL_GROUP_DIM = POOL_WIDTH // POOL_GROUPS
POOL_WINDOWS = (2, 4, 8, 16)
MLA_HEADS = 8
QK_NOPE_DIM = 64
QK_ROPE_DIM = 32
HALF_ROPE = QK_ROPE_DIM // 2
V_HEAD_DIM = 64
Q_LORA_RANK = 256
KV_LORA_RANK = 128
MLA_WIDTH = MLA_HEADS * V_HEAD_DIM
ROPE_THETA = 10000.0
GMLP_WIDTH = 512
GMLP_CHUNK = 128
GMLP_GROUPS = 8
GMLP_GROUP_DIM = GMLP_WIDTH // GMLP_GROUPS
N_BRANCHES = 3
EPS = 1e-6
IN_SPLITS = (POOL_WIDTH, POOL_WIDTH, Q_LORA_RANK, KV_LORA_RANK, QK_ROPE_DIM, MLA_WIDTH,
             GMLP_WIDTH, GMLP_WIDTH, GMLP_WIDTH, N_BRANCHES * D_MODEL)

V7X_LANES = 128
V7X_SUBLANES = 8
V7X_VMEM_BYTES = 64 * 1024 * 1024

HEAD_PAD = V7X_LANES
ROPE_LANE0 = QK_NOPE_DIM
ONES_LANE = V_HEAD_DIM
HALO = max(POOL_WINDOWS)
POOL_PAD = V7X_SUBLANES
POOL_TOP = POOL_PAD + HALO
assert all(w & (w - 1) == 0 and w <= HALO for w in POOL_WINDOWS)
POOL_LEVELS = sum(w.bit_length() - 2 for w in POOL_WINDOWS)

COL_A_IN = 0
COL_A_GATE = COL_A_IN + POOL_WIDTH
COL_CQ = COL_A_GATE + POOL_WIDTH
COL_CKV = COL_CQ + Q_LORA_RANK
COL_KR = COL_CKV + KV_LORA_RANK
COL_B_GATE = COL_KR + HEAD_PAD
COL_U = COL_B_GATE + MLA_WIDTH
COL_V = COL_U + GMLP_WIDTH
COL_C_GATE = COL_V + GMLP_WIDTH
COL_GATES = COL_C_GATE + GMLP_WIDTH
IN_WIDTH_PAD = COL_GATES + N_BRANCHES * D_MODEL

SOFTMAX_SCALE = (QK_NOPE_DIM + QK_ROPE_DIM) ** -0.5
LOG2E = math.log2(math.e)

FRONT_BLOCK = 512
FRONT_VMEM_BYTES = V7X_VMEM_BYTES * 7 // 8
ATTN_BLOCK = 1024
ATTN_VMEM_BYTES = V7X_VMEM_BYTES * 7 // 8
BACK_BLOCK = 1024
BACK_VMEM_BYTES = V7X_VMEM_BYTES * 3 // 4
ROPE_BLOCK = 1024
HEADS_PER_STEP = 2
SOFTMAX_ROWS = 64

bf16 = jnp.bfloat16
f32 = jnp.float32


def _sigmoid(z):
    return 0.5 * jnp.tanh(0.5 * z) + 0.5


def _silu(z):
    return z * _sigmoid(z)


def _rms(x, g):
    return x * lax.rsqrt(jnp.mean(x * x, axis=-1, keepdims=True) + EPS) * g


def _dot(a, b):
    return jnp.dot(a, b, preferred_element_type=f32)


def _rope_table_kernel(pos_even_ref, pos_odd_ref, invf_ref, cos_ref, sin_ref):
    half_rows = pos_even_ref.shape[0]
    lane = lax.broadcasted_iota(jnp.int32, (half_rows, HEAD_PAD), 1)
    invf = invf_ref[...]
    pos = jnp.where(lane >= ROPE_LANE0, pos_even_ref[...], pos_odd_ref[...])
    ang = pos.astype(f32) * invf
    c = jnp.cos(ang)
    s = jnp.sin(ang)
    rope = (lane >= ROPE_LANE0) & (lane < ROPE_LANE0 + QK_ROPE_DIM)
    first_half = lane < ROPE_LANE0 + HALF_ROPE
    for odd in range(2):
        c_t = pltpu.roll(c, ROPE_LANE0, 1) if odd else c
        s_t = pltpu.roll(s, ROPE_LANE0, 1) if odd else s
        rows = pl.ds(odd, half_rows, stride=2)
        cos_ref[rows, :] = jnp.where(lane < ROPE_LANE0, 1.0, jnp.where(rope, c_t, 0.0))
        sin_ref[rows, :] = jnp.where(rope, jnp.where(first_half, -s_t, s_t), 0.0)


def _rope_tables(positions):
    n = positions.size
    assert 2 * ROPE_LANE0 == HEAD_PAD and QK_ROPE_DIM <= ROPE_LANE0
    inv_freq = ROPE_THETA ** (-jnp.arange(0, QK_ROPE_DIM, 2, dtype=f32) / QK_ROPE_DIM)
    pad = jnp.zeros((ROPE_LANE0 - QK_ROPE_DIM,), f32)
    half_tile = jnp.concatenate([inv_freq, inv_freq, pad])
    invf = jnp.concatenate([half_tile, half_tile]).reshape(1, HEAD_PAD)
    pairs = positions.reshape(n // 2, 2)
    out = jax.ShapeDtypeStruct((n, HEAD_PAD), f32)
    half_block = ROPE_BLOCK // 2
    return pl.pallas_call(
        _rope_table_kernel,
        grid=(n // ROPE_BLOCK,),
        in_specs=[pl.BlockSpec((half_block, 1), lambda i: (i, 0)),
                  pl.BlockSpec((half_block, 1), lambda i: (i, 0)),
                  pl.BlockSpec((1, HEAD_PAD), lambda i: (0, 0))],
        out_specs=[pl.BlockSpec((ROPE_BLOCK, HEAD_PAD), lambda i: (i, 0))] * 2,
        out_shape=[out, out],
        name="rope_tables",
    )(pairs[:, 0:1], pairs[:, 1:2], invf)


def _front_kernel(x_ref, cos_ref, sin_ref, norm_pre_ref, w_head_ref, w_kr_ref, w_tail_ref, b_gate_ref, pool_w_ref,
                  pool_scale_ref, q_norm_ref, w_uq_ref, kv_norm_ref, w_ukv_ref, vn_g_ref,
                  vn_b_ref, w_s_ref, bs_ref, w_oa_ref, w_oc_ref,
                  q_ref, k_ref, v_ref, part_ref, g1_ref, sb_ref, abuf, lv_ref):
    tb = x_ref.shape[0]
    j = pl.program_id(1)

    hb = _rms(x_ref[...], norm_pre_ref[...]).astype(bf16)

    def proj(c0, width):
        if c0 + width <= COL_KR:
            w = w_head_ref[:, c0:c0 + width]
        elif c0 == COL_KR and width == HEAD_PAD:
            w = w_kr_ref[...]
        else:
            assert c0 >= COL_B_GATE
            w = w_tail_ref[:, c0 - COL_B_GATE:c0 - COL_B_GATE + width]
        return _dot(hb, w)

    @pl.when(j == 0)
    def _():
        abuf[0:POOL_TOP, :] = jnp.zeros((POOL_TOP, POOL_WIDTH), f32)
        lv_ref[:, 0:POOL_PAD, :] = jnp.zeros((lv_ref.shape[0], POOL_PAD, POOL_GROUP_DIM), f32)

    a_in = proj(COL_A_IN, POOL_WIDTH)
    abuf[POOL_TOP:POOL_TOP + tb, :] = a_in
    cq = proj(COL_CQ, Q_LORA_RANK)
    ckv = proj(COL_CKV, KV_LORA_RANK)
    kr = proj(COL_KR, HEAD_PAD)
    vv = proj(COL_V, GMLP_WIDTH)
    a_gate = proj(COL_A_GATE, POOL_WIDTH)
    b_gate = proj(COL_B_GATE, MLA_WIDTH)
    u = proj(COL_U, GMLP_WIDTH)
    c_gate = proj(COL_C_GATE, GMLP_WIDTH)

    t = j * tb + lax.broadcasted_iota(jnp.int32, (tb, 1), 0)
    ys = []
    level = 0
    for g, w in enumerate(POOL_WINDOWS):
        cols = slice(g * POOL_GROUP_DIM, (g + 1) * POOL_GROUP_DIM)
        src = lambda lo, n: abuf[lo:lo + n, cols]
        shift = 1
        while 2 * shift < w:
            lo, n = POOL_PAD, HALO + tb
            dst = level
            lv_ref[dst, lo:lo + n, :] = src(lo, n) + src(lo - shift, n)
            src = lambda lo, n, dst=dst: lv_ref[dst, lo:lo + n, :]
            level += 1
            shift *= 2
        win = src(POOL_TOP, tb) + src(POOL_TOP - shift, tb)
        count = jnp.minimum(t + 1, w).astype(f32)
        mixed = win / count - a_in[:, cols]
        ys.append(_dot(mixed.astype(bf16), pool_w_ref[g]))
    abuf[POOL_PAD:POOL_TOP, :] = abuf[tb + POOL_PAD:tb + POOL_TOP, :]

    cqn = _rms(cq, q_norm_ref[...]).astype(bf16)
    q2 = _dot(cqn, w_uq_ref[...])
    ckvn = _rms(ckv, kv_norm_ref[...]).astype(bf16)
    kv2 = _dot(ckvn, w_ukv_ref[...])

    vc = vv - jnp.mean(vv, axis=-1, keepdims=True)
    var = jnp.mean(vc * vc, axis=-1, keepdims=True)
    vb = (vc * lax.rsqrt(var + EPS) * vn_g_ref[...] + vn_b_ref[...]).astype(bf16)
    row = lax.broadcasted_iota(jnp.int32, (GMLP_CHUNK, GMLP_CHUNK), 0)
    col = lax.broadcasted_iota(jnp.int32, (GMLP_CHUNK, GMLP_CHUNK), 1)
    ws = [jnp.where(col <= row, w_s_ref[g], jnp.zeros((), bf16)) for g in range(GMLP_GROUPS)]
    groups_per_tile = V7X_LANES // GMLP_GROUP_DIM
    assert groups_per_tile == 2
    chunks = []
    for c in range(tb // GMLP_CHUNK):
        tiles = []
        for tile in range(GMLP_WIDTH // V7X_LANES):
            rhs = vb[c * GMLP_CHUNK:(c + 1) * GMLP_CHUNK, tile * V7X_LANES:(tile + 1) * V7X_LANES]
            lo = _dot(ws[2 * tile], rhs)
            hi = _dot(ws[2 * tile + 1], rhs)
            tiles.append(jnp.where(col < GMLP_GROUP_DIM, lo, hi))
        chunks.append(jnp.concatenate(tiles, axis=1) + bs_ref[...])
    sv = jnp.concatenate(chunks, axis=0)

    gate_pre = {i: proj(COL_GATES + i * D_MODEL, D_MODEL) for i in (0, 2)}

    cos_t = cos_ref[...]
    sin_t = sin_ref[...]
    lane = lax.broadcasted_iota(jnp.int32, (tb, HEAD_PAD), 1)
    first_half = lane < ROPE_LANE0 + HALF_ROPE

    def rotate(t):
        swapped = jnp.where(first_half, pltpu.roll(t, HEAD_PAD - HALF_ROPE, 1), pltpu.roll(t, HALF_ROPE, 1))
        return t * cos_t + swapped * sin_t

    for h in range(MLA_HEADS):
        q_ref[h] = (rotate(q2[:, h * HEAD_PAD:(h + 1) * HEAD_PAD]) * (SOFTMAX_SCALE * LOG2E)).astype(bf16)
    k_rope = rotate(kr)
    low = lane < QK_NOPE_DIM
    ones_lane = jnp.where(lane == ONES_LANE, 1.0, 0.0)
    v0 = MLA_HEADS * QK_NOPE_DIM
    for pair in range(MLA_HEADS // 2):
        k_pair = kv2[:, pair * HEAD_PAD:(pair + 1) * HEAD_PAD]
        v_pair = kv2[:, v0 + pair * HEAD_PAD:v0 + (pair + 1) * HEAD_PAD]
        for odd in range(2):
            k_h = pltpu.roll(k_pair, QK_NOPE_DIM, 1) if odd else k_pair
            v_h = pltpu.roll(v_pair, V_HEAD_DIM, 1) if odd else v_pair
            k_ref[2 * pair + odd] = (jnp.where(low, k_h, 0.0) + k_rope).astype(bf16)
            v_ref[2 * pair + odd] = jnp.where(low, v_h, ones_lane).astype(bf16)
    sb_ref[...] = _silu(b_gate).astype(sb_ref.dtype)

    y_a = jnp.concatenate(ys, axis=1) * pool_scale_ref[...] * _silu(a_gate)
    y_c = u * sv * _silu(c_gate)

    def gate(i):
        return _sigmoid(gate_pre[i] + b_gate_ref[:, i * D_MODEL:(i + 1) * D_MODEL])

    part_ref[...] = (gate(0) * _dot(y_a.astype(bf16), w_oa_ref[...])
                     + gate(2) * _dot(y_c.astype(bf16), w_oc_ref[...])).astype(part_ref.dtype)
    gate_pre[1] = proj(COL_GATES + D_MODEL, D_MODEL)
    g1_ref[...] = gate(1).astype(g1_ref.dtype)


def _const_spec(stacked, l):
    index = (l,) + (0,) * (stacked.ndim - 1)
    return pl.BlockSpec((None,) + stacked.shape[1:], lambda b, j: index, pipeline_mode=pl.Buffered(1))


def _front(x, cos_t, sin_t, p, l):
    B, S, _ = x.shape
    tb = FRONT_BLOCK
    tok = lambda width: pl.BlockSpec((None, tb, width), lambda b, j: (b, j, 0))
    heads = pl.BlockSpec((None, MLA_HEADS, tb, HEAD_PAD), lambda b, j: (b, 0, j, 0))
    consts = [p["norm_pre"], p["w_head"], p["w_kr"], p["w_tail"], p["b_gate"], p["pool_w"],
              p["pool_scale"], p["q_norm"],
              p["w_uq"], p["kv_norm"], p["w_ukv"], p["v_norm_g"], p["v_norm_b"], p["w_s"],
              p["bs_full"], p["w_oa"], p["w_oc"]]
    head_shape = jax.ShapeDtypeStruct((B, MLA_HEADS, S, HEAD_PAD), bf16)
    return pl.pallas_call(
        _front_kernel,
        grid=(B, S // tb),
        in_specs=[tok(D_MODEL), tok(HEAD_PAD), tok(HEAD_PAD)] + [_const_spec(c, l) for c in consts],
        out_specs=[heads, heads, heads, tok(D_MODEL), tok(D_MODEL), tok(MLA_WIDTH)],
        out_shape=[head_shape, head_shape, head_shape,
                   jax.ShapeDtypeStruct((B, S, D_MODEL), bf16),
                   jax.ShapeDtypeStruct((B, S, D_MODEL), bf16),
                   jax.ShapeDtypeStruct((B, S, MLA_WIDTH), bf16)],
        scratch_shapes=[pltpu.VMEM((POOL_TOP + tb, POOL_WIDTH), f32),
                        pltpu.VMEM((POOL_LEVELS, POOL_TOP + tb, POOL_GROUP_DIM), f32)],
        compiler_params=pltpu.CompilerParams(
            dimension_semantics=("arbitrary", "arbitrary"),
            vmem_limit_bytes=FRONT_VMEM_BYTES),
        name="layer_front",
    )(x, cos_t, sin_t, *consts)


def _attn_kernel(q_ref, k_ref, v_ref, o_ref, s_ref, p_ref, m_ref, acc_ref):
    hp, tq2, _ = q_ref.shape
    tq = tq2 // 2
    u = pl.program_id(2)
    lane = lax.broadcasted_iota(jnp.int32, (tq, HEAD_PAD), 1)
    n_tiles = tq // V7X_LANES
    half = tq // 2
    nt_dims = (((1,), (1,)), ((), ()))

    def init_state():
        m_ref[...] = jnp.full(m_ref.shape, -jnp.inf, f32)
        acc_ref[...] = jnp.zeros(acc_ref.shape, f32)

    def scores(q0, hh, j, slot, diagonal=False):
        start = pl.multiple_of(j * tq, tq)
        if diagonal:
            s_ref[slot, hh, 0:half, 0:half] = lax.dot_general(
                q_ref[hh, q0:q0 + half], k_ref[hh, pl.ds(start, half), :], nt_dims, preferred_element_type=f32)
            s_ref[slot, hh, half:tq, :] = lax.dot_general(
                q_ref[hh, q0 + half:q0 + tq], k_ref[hh, pl.ds(start, tq), :], nt_dims, preferred_element_type=f32)
        else:
            s_ref[slot, hh] = lax.dot_general(q_ref[hh, q0:q0 + tq], k_ref[hh, pl.ds(start, tq), :], nt_dims,
                                              preferred_element_type=f32)

    def softmax_pv(hh, j, slot, diagonal):
        for r in range(tq // SOFTMAX_ROWS):
            r0 = r * SOFTMAX_ROWS
            rows = slice(r0, r0 + SOFTMAX_ROWS)
            live = (r0 + SOFTMAX_ROWS - 1) // V7X_LANES + 1 if diagonal else n_tiles
            read = (half if r0 < half else tq) // V7X_LANES if diagonal else n_tiles

            def tile(c):
                s = s_ref[slot, hh, rows, c * V7X_LANES:(c + 1) * V7X_LANES]
                if diagonal and (c + 1) * V7X_LANES - 1 > r0:
                    row = r0 + lax.broadcasted_iota(jnp.int32, s.shape, 0)
                    col = c * V7X_LANES + lax.broadcasted_iota(jnp.int32, s.shape, 1)
                    s = jnp.where(col <= row, s, -jnp.inf)
                return s

            m_blk = tile(0)
            for c in range(1, live):
                m_blk = jnp.maximum(m_blk, tile(c))
            m_old = m_ref[hh, rows]
            m_new = jnp.maximum(m_old, jnp.max(m_blk, axis=-1, keepdims=True))
            m_ref[hh, rows] = m_new
            acc_ref[hh, rows] = jnp.exp2(m_old - m_new) * acc_ref[hh, rows]
            for c in range(live):
                p_ref[hh, rows, c * V7X_LANES:(c + 1) * V7X_LANES] = jnp.exp2(tile(c) - m_new).astype(bf16)
            if live < read:
                p_ref[hh, rows, live * V7X_LANES:read * V7X_LANES] = jnp.zeros(
                    (SOFTMAX_ROWS, (read - live) * V7X_LANES), bf16)
        start = pl.multiple_of(j * tq, tq)
        if diagonal:
            acc_ref[hh, 0:half] += _dot(p_ref[hh, 0:half, 0:half], v_ref[hh, pl.ds(start, half), :])
            acc_ref[hh, half:tq] += _dot(p_ref[hh, half:tq, :], v_ref[hh, pl.ds(start, tq), :])
        else:
            acc_ref[hh] += _dot(p_ref[hh], v_ref[hh, pl.ds(start, tq), :])

    def step(q0, j, slot, next_diagonal=False):
        for hh in range(hp):
            softmax_pv(hh, j, slot, False)
            scores(q0, hh, j + 1, 1 - slot, next_diagonal)

    def last_step(i, slot, next_pass_scores=None):
        for hh in range(hp):
            softmax_pv(hh, i, slot, True)
            if next_pass_scores is not None:
                next_pass_scores(hh)

    def finish(q0):
        outs = [acc_ref[hh] / acc_ref[hh, :, ONES_LANE:ONES_LANE + 1] for hh in range(hp)]
        o_ref[q0:q0 + tq, :] = jnp.where(lane < V_HEAD_DIM, outs[0],
                                         pltpu.roll(outs[1], V_HEAD_DIM, 1)).astype(o_ref.dtype)

    def pairs(q0, first_slot, trips):
        def body(t, carry):
            step(q0, 2 * t, first_slot)
            step(q0, 2 * t + 1, 1 - first_slot)
            return carry

        lax.fori_loop(0, trips, body, 0)

    i_even = 2 * u
    odd_first_scores = lambda hh: scores(tq, hh, 0, 1)
    init_state()

    @pl.when(u == 0)
    def _():
        for hh in range(hp):
            scores(0, hh, 0, 0, True)
        last_step(i_even, 0, odd_first_scores)

    @pl.when(u > 0)
    def _():
        for hh in range(hp):
            scores(0, hh, 0, 0)
        pairs(0, 0, u - 1)
        step(0, i_even - 2, 0)
        step(0, i_even - 1, 1, True)
        last_step(i_even, 0, odd_first_scores)

    finish(0)

    i_odd = i_even + 1
    init_state()
    pairs(tq, 1, u)
    step(tq, i_odd - 1, 1, True)
    last_step(i_odd, 0)
    finish(tq)


def _attention(q, k, v):
    B, H, S, _ = q.shape
    tq = ATTN_BLOCK
    hp = HEADS_PER_STEP
    assert hp * V_HEAD_DIM == V7X_LANES
    return pl.pallas_call(
        _attn_kernel,
        grid=(B, H // hp, S // (2 * tq)),
        in_specs=[pl.BlockSpec((None, hp, 2 * tq, HEAD_PAD), lambda b, g, u: (b, g, u, 0)),
                  pl.BlockSpec((None, hp, S, HEAD_PAD), lambda b, g, u: (b, g, 0, 0)),
                  pl.BlockSpec((None, hp, S, HEAD_PAD), lambda b, g, u: (b, g, 0, 0))],
        out_specs=pl.BlockSpec((None, 2 * tq, hp * V_HEAD_DIM), lambda b, g, u: (b, u, g)),
        out_shape=jax.ShapeDtypeStruct((B, S, H * V_HEAD_DIM), bf16),
        scratch_shapes=[pltpu.VMEM((2, hp, tq, tq), f32),
                        pltpu.VMEM((hp, tq, tq), bf16),
                        pltpu.VMEM((hp, tq, V7X_LANES), f32),
                        pltpu.VMEM((hp, tq, HEAD_PAD), f32)],
        compiler_params=pltpu.CompilerParams(
            dimension_semantics=("arbitrary", "arbitrary", "arbitrary"),
            vmem_limit_bytes=ATTN_VMEM_BYTES),
        name="mla_attention",
    )(q, k, v)


def _back_kernel(x_ref, o_ref, sb_ref, g1_ref, part_ref, w_ob_ref, w_out_ref, norm_post_ref,
                 out_ref):
    y_b = o_ref[...] * sb_ref[...]
    merged = part_ref[...].astype(f32) + g1_ref[...].astype(f32) * _dot(y_b, w_ob_ref[...])
    z = _dot(merged.astype(bf16), w_out_ref[...])
    out_ref[...] = x_ref[...] + _rms(z, norm_post_ref[...])


def _back(x, o, sb, g1, part, p, l):
    B, S, _ = x.shape
    tb = BACK_BLOCK
    tok = lambda width: pl.BlockSpec((None, tb, width), lambda b, j: (b, j, 0))
    consts = [p["w_ob"], p["w_out"], p["norm_post"]]
    return pl.pallas_call(
        _back_kernel,
        grid=(B, S // tb),
        in_specs=[tok(D_MODEL), tok(MLA_WIDTH), tok(MLA_WIDTH), tok(D_MODEL), tok(D_MODEL)]
        + [_const_spec(c, l) for c in consts],
        out_specs=tok(D_MODEL),
        out_shape=jax.ShapeDtypeStruct((B, S, D_MODEL), f32),
        compiler_params=pltpu.CompilerParams(
            dimension_semantics=("arbitrary", "arbitrary"),
            vmem_limit_bytes=BACK_VMEM_BYTES),
        name="layer_back",
    )(x, o, sb, g1, part, *consts)


def _stacked_params(norm_pre, norm_post, w_in, b_gate, pool_w, pool_scale, q_norm, w_uq,
                    kv_norm, w_ukv, v_norm_g, v_norm_b, w_s, b_s, w_oa, w_ob, w_oc, w_out):
    L = w_in.shape[0]
    zeros = lambda *shape: jnp.zeros(shape, bf16)
    kr_src = sum(IN_SPLITS[:4])
    assert kr_src == COL_KR
    w_head = w_in[..., :kr_src].astype(bf16)
    w_kr = jnp.concatenate([zeros(L, D_MODEL, ROPE_LANE0),
                            w_in[..., kr_src:kr_src + QK_ROPE_DIM].astype(bf16),
                            zeros(L, D_MODEL, HEAD_PAD - ROPE_LANE0 - QK_ROPE_DIM)], axis=-1)
    w_tail = w_in[..., kr_src + QK_ROPE_DIM:].astype(bf16)
    assert w_tail.shape[-1] == IN_WIDTH_PAD - COL_B_GATE

    uq = w_uq.astype(bf16).reshape(L, Q_LORA_RANK, MLA_HEADS, QK_NOPE_DIM + QK_ROPE_DIM)
    nope, x1, x2 = uq[..., :QK_NOPE_DIM], uq[..., QK_NOPE_DIM:QK_NOPE_DIM + HALF_ROPE], uq[..., QK_NOPE_DIM + HALF_ROPE:]
    pad = zeros(L, Q_LORA_RANK, MLA_HEADS, HEAD_PAD - QK_NOPE_DIM - QK_ROPE_DIM)
    w_uq_p = jnp.concatenate([nope, x1, x2, pad], axis=-1).reshape(L, Q_LORA_RANK, MLA_HEADS * HEAD_PAD)

    assert 2 * QK_NOPE_DIM == HEAD_PAD and V_HEAD_DIM == QK_NOPE_DIM
    ukv = w_ukv.astype(bf16).reshape(L, KV_LORA_RANK, MLA_HEADS, QK_NOPE_DIM + V_HEAD_DIM)
    k_cols = ukv[..., :QK_NOPE_DIM].reshape(L, KV_LORA_RANK, MLA_HEADS * QK_NOPE_DIM)
    v_cols = ukv[..., QK_NOPE_DIM:].reshape(L, KV_LORA_RANK, MLA_HEADS * V_HEAD_DIM)
    w_ukv_p = jnp.concatenate([k_cols, v_cols], axis=-1)

    row = lambda a: a.reshape(L, 1, -1)
    return {
        "norm_pre": row(norm_pre), "norm_post": row(norm_post),
        "w_head": w_head, "w_kr": w_kr, "w_tail": w_tail,
        "b_gate": row(b_gate), "pool_w": pool_w.astype(bf16), "pool_scale": row(pool_scale),
        "q_norm": row(q_norm), "w_uq": w_uq_p, "kv_norm": row(kv_norm), "w_ukv": w_ukv_p,
        "v_norm_g": row(v_norm_g), "v_norm_b": row(v_norm_b), "w_s": w_s.astype(bf16),
        "bs_full": jnp.repeat(jnp.swapaxes(b_s, 1, 2), GMLP_GROUP_DIM, axis=2),
        "w_oa": w_oa.astype(bf16), "w_ob": w_ob.astype(bf16), "w_oc": w_oc.astype(bf16),
        "w_out": w_out.astype(bf16),
    }


def kernel(x, positions, norm_pre, norm_post, w_in, b_gate, pool_w, pool_scale, q_norm, w_uq, kv_norm, w_ukv, v_norm_g, v_norm_b, w_s, b_s, w_oa, w_ob, w_oc, w_out):
    B, S, _ = x.shape
    cos_t, sin_t = _rope_tables(positions)
    cos_t = cos_t.reshape(B, S, HEAD_PAD)
    sin_t = sin_t.reshape(B, S, HEAD_PAD)
    p = _stacked_params(norm_pre, norm_post, w_in, b_gate, pool_w, pool_scale, q_norm, w_uq,
                        kv_norm, w_ukv, v_norm_g, v_norm_b, w_s, b_s, w_oa, w_ob, w_oc, w_out)
    for l in range(norm_pre.shape[0]):
        q, k, v, part, g1, sb = _front(x, cos_t, sin_t, p, l)
        o = _attention(q, k, v)
        x = _back(x, o, sb, g1, part, p, l)
    return x
```

```python
import math

import jax
import jax.numpy as jnp
from jax import lax
from jax.experimental import pallas as pl
from jax.experimental.pallas import tpu as pltpu

D_MODEL = 1024
POOL_WIDTH = 512
POOL_GROUPS = 4
POOL_GROUP_DIM = POOL_WIDTH // POOL_GROUPS
POOL_WINDOWS = (2, 4, 8, 16)
MLA_HEADS = 8
QK_NOPE_DIM = 64
QK_ROPE_DIM = 32
HALF_ROPE = QK_ROPE_DIM // 2
V_HEAD_DIM = 64
Q_LORA_RANK = 256
KV_LORA_RANK = 128
MLA_WIDTH = MLA_HEADS * V_HEAD_DIM
ROPE_THETA = 10000.0
GMLP_WIDTH = 512
GMLP_CHUNK = 128
GMLP_GROUPS = 8
GMLP_GROUP_DIM = GMLP_WIDTH // GMLP_GROUPS
N_BRANCHES = 3
EPS = 1e-6
IN_SPLITS = (POOL_WIDTH, POOL_WIDTH, Q_LORA_RANK, KV_LORA_RANK, QK_ROPE_DIM, MLA_WIDTH,
             GMLP_WIDTH, GMLP_WIDTH, GMLP_WIDTH, N_BRANCHES * D_MODEL)

V7X_LANES = 128
V7X_SUBLANES = 8
V7X_VMEM_BYTES = 64 * 1024 * 1024

HEAD_PAD = V7X_LANES
ROPE_LANE0 = QK_NOPE_DIM
ONES_LANE = V_HEAD_DIM
HALO = max(POOL_WINDOWS)
POOL_PAD = V7X_SUBLANES
POOL_TOP = POOL_PAD + HALO
assert all(w & (w - 1) == 0 and w <= HALO for w in POOL_WINDOWS)
POOL_LEVELS = sum(w.bit_length() - 2 for w in POOL_WINDOWS)

COL_A_IN = 0
COL_A_GATE = COL_A_IN + POOL_WIDTH
COL_CQ = COL_A_GATE + POOL_WIDTH
COL_CKV = COL_CQ + Q_LORA_RANK
COL_KR = COL_CKV + KV_LORA_RANK
COL_B_GATE = COL_KR + HEAD_PAD
COL_U = COL_B_GATE + MLA_WIDTH
COL_V = COL_U + GMLP_WIDTH
COL_C_GATE = COL_V + GMLP_WIDTH
COL_GATES = COL_C_GATE + GMLP_WIDTH
IN_WIDTH_PAD = COL_GATES + N_BRANCHES * D_MODEL

SOFTMAX_SCALE = (QK_NOPE_DIM + QK_ROPE_DIM) ** -0.5
LOG2E = math.log2(math.e)

FRONT_BLOCK = 512
FRONT_VMEM_BYTES = V7X_VMEM_BYTES * 7 // 8
ATTN_BLOCK = 1024
ATTN_VMEM_BYTES = V7X_VMEM_BYTES * 7 // 8
BACK_BLOCK = 1024
BACK_VMEM_BYTES = V7X_VMEM_BYTES * 3 // 4
ROPE_BLOCK = 1024
HEADS_PER_STEP = 2
SOFTMAX_ROWS = 64

bf16 = jnp.bfloat16
f32 = jnp.float32


def _sigmoid(z):
    return 0.5 * jnp.tanh(0.5 * z) + 0.5


def _silu(z):
    return z * _sigmoid(z)


def _rms(x, g):
    return x * lax.rsqrt(jnp.mean(x * x, axis=-1, keepdims=True) + EPS) * g


def _dot(a, b):
    return jnp.dot(a, b, preferred_element_type=f32)


def _rope_table_kernel(pos_ref, invf_ref, cos_ref, sin_ref):
    half_rows = pos_ref.shape[0]
    lane = lax.broadcasted_iota(jnp.int32, (half_rows, HEAD_PAD), 1)
    invf = invf_ref[...]
    pos = jnp.where(lane >= ROPE_LANE0, pos_ref[:, 0:1], pos_ref[:, 1:2])
    ang = pos.astype(f32) * invf
    c = jnp.cos(ang)
    s = jnp.sin(ang)
    rope = (lane >= ROPE_LANE0) & (lane < ROPE_LANE0 + QK_ROPE_DIM)
    first_half = lane < ROPE_LANE0 + HALF_ROPE
    for odd in range(2):
        c_t = pltpu.roll(c, ROPE_LANE0, 1) if odd else c
        s_t = pltpu.roll(s, ROPE_LANE0, 1) if odd else s
        rows = pl.ds(odd, half_rows, stride=2)
        cos_ref[rows, :] = jnp.where(lane < ROPE_LANE0, 1.0, jnp.where(rope, c_t, 0.0))
        sin_ref[rows, :] = jnp.where(rope, jnp.where(first_half, -s_t, s_t), 0.0)


def _rope_tables(positions):
    n = positions.size
    assert 2 * ROPE_LANE0 == HEAD_PAD and QK_ROPE_DIM <= ROPE_LANE0
    inv_freq = ROPE_THETA ** (-jnp.arange(0, QK_ROPE_DIM, 2, dtype=f32) / QK_ROPE_DIM)
    pad = jnp.zeros((ROPE_LANE0 - QK_ROPE_DIM,), f32)
    half_tile = jnp.concatenate([inv_freq, inv_freq, pad])
    invf = jnp.concatenate([half_tile, half_tile]).reshape(1, HEAD_PAD)
    out = jax.ShapeDtypeStruct((n, HEAD_PAD), f32)
    return pl.pallas_call(
        _rope_table_kernel,
        grid=(n // ROPE_BLOCK,),
        in_specs=[pl.BlockSpec((ROPE_BLOCK // 2, 2), lambda i: (i, 0)),
                  pl.BlockSpec((1, HEAD_PAD), lambda i: (0, 0))],
        out_specs=[pl.BlockSpec((ROPE_BLOCK, HEAD_PAD), lambda i: (i, 0))] * 2,
        out_shape=[out, out],
        name="rope_tables",
    )(positions.reshape(n // 2, 2), invf)


def _front_kernel(x_ref, cos_ref, sin_ref, norm_pre_ref, w_head_ref, w_kr_ref, w_tail_ref, b_gate_ref, pool_w_ref,
                  pool_scale_ref, q_norm_ref, w_uq_ref, kv_norm_ref, w_ukv_ref, vn_g_ref,
                  vn_b_ref, w_s_ref, bs_ref, w_oa_ref, w_oc_ref,
                  q_ref, k_ref, v_ref, part_ref, g1_ref, sb_ref, abuf, lv_ref):
    tb = x_ref.shape[0]
    j = pl.program_id(1)

    hb = _rms(x_ref[...], norm_pre_ref[...]).astype(bf16)

    def proj(c0, width):
        if c0 + width <= COL_KR:
            w = w_head_ref[:, c0:c0 + width]
        elif c0 == COL_KR and width == HEAD_PAD:
            w = w_kr_ref[...]
        else:
            assert c0 >= COL_B_GATE
            w = w_tail_ref[:, c0 - COL_B_GATE:c0 - COL_B_GATE + width]
        return _dot(hb, w)

    @pl.when(j == 0)
    def _():
        abuf[0:POOL_TOP, :] = jnp.zeros((POOL_TOP, POOL_WIDTH), f32)
        lv_ref[:, 0:POOL_PAD, :] = jnp.zeros((lv_ref.shape[0], POOL_PAD, POOL_GROUP_DIM), f32)

    a_in = proj(COL_A_IN, POOL_WIDTH)
    abuf[POOL_TOP:POOL_TOP + tb, :] = a_in
    cq = proj(COL_CQ, Q_LORA_RANK)
    ckv = proj(COL_CKV, KV_LORA_RANK)
    kr = proj(COL_KR, HEAD_PAD)
    vv = proj(COL_V, GMLP_WIDTH)
    a_gate = proj(COL_A_GATE, POOL_WIDTH)
    b_gate = proj(COL_B_GATE, MLA_WIDTH)
    u = proj(COL_U, GMLP_WIDTH)
    c_gate = proj(COL_C_GATE, GMLP_WIDTH)

    t = j * tb + lax.broadcasted_iota(jnp.int32, (tb, 1), 0)
    ys = []
    level = 0
    for g, w in enumerate(POOL_WINDOWS):
        cols = slice(g * POOL_GROUP_DIM, (g + 1) * POOL_GROUP_DIM)
        src = lambda lo, n: abuf[lo:lo + n, cols]
        shift = 1
        while 2 * shift < w:
            lo, n = POOL_PAD, HALO + tb
            dst = level
            lv_ref[dst, lo:lo + n, :] = src(lo, n) + src(lo - shift, n)
            src = lambda lo, n, dst=dst: lv_ref[dst, lo:lo + n, :]
            level += 1
            shift *= 2
        win = src(POOL_TOP, tb) + src(POOL_TOP - shift, tb)
        count = jnp.minimum(t + 1, w).astype(f32)
        mixed = win / count - a_in[:, cols]
        ys.append(_dot(mixed.astype(bf16), pool_w_ref[g]))
    abuf[POOL_PAD:POOL_TOP, :] = abuf[tb + POOL_PAD:tb + POOL_TOP, :]

    cqn = _rms(cq, q_norm_ref[...]).astype(bf16)
    q2 = _dot(cqn, w_uq_ref[...])
    ckvn = _rms(ckv, kv_norm_ref[...]).astype(bf16)
    kv2 = _dot(ckvn, w_ukv_ref[...])

    vc = vv - jnp.mean(vv, axis=-1, keepdims=True)
    var = jnp.mean(vc * vc, axis=-1, keepdims=True)
    vb = (vc * lax.rsqrt(var + EPS) * vn_g_ref[...] + vn_b_ref[...]).astype(bf16)
    row = lax.broadcasted_iota(jnp.int32, (GMLP_CHUNK, GMLP_CHUNK), 0)
    col = lax.broadcasted_iota(jnp.int32, (GMLP_CHUNK, GMLP_CHUNK), 1)
    ws = [jnp.where(col <= row, w_s_ref[g], jnp.zeros((), bf16)) for g in range(GMLP_GROUPS)]
    groups_per_tile = V7X_LANES // GMLP_GROUP_DIM
    assert groups_per_tile == 2
    chunks = []
    for c in range(tb // GMLP_CHUNK):
        tiles = []
        for tile in range(GMLP_WIDTH // V7X_LANES):
            rhs = vb[c * GMLP_CHUNK:(c + 1) * GMLP_CHUNK, tile * V7X_LANES:(tile + 1) * V7X_LANES]
            lo = _dot(ws[2 * tile], rhs)
            hi = _dot(ws[2 * tile + 1], rhs)
            tiles.append(jnp.where(col < GMLP_GROUP_DIM, lo, hi))
        chunks.append(jnp.concatenate(tiles, axis=1) + bs_ref[...])
    sv = jnp.concatenate(chunks, axis=0)

    gate_pre = {i: proj(COL_GATES + i * D_MODEL, D_MODEL) for i in (0, 2)}

    cos_t = cos_ref[...]
    sin_t = sin_ref[...]
    lane = lax.broadcasted_iota(jnp.int32, (tb, HEAD_PAD), 1)
    first_half = lane < ROPE_LANE0 + HALF_ROPE

    def rotate(t):
        swapped = jnp.where(first_half, pltpu.roll(t, HEAD_PAD - HALF_ROPE, 1), pltpu.roll(t, HALF_ROPE, 1))
        return t * cos_t + swapped * sin_t

    for h in range(MLA_HEADS):
        q_ref[h] = (rotate(q2[:, h * HEAD_PAD:(h + 1) * HEAD_PAD]) * (SOFTMAX_SCALE * LOG2E)).astype(bf16)
    k_rope = rotate(kr)
    low = lane < QK_NOPE_DIM
    ones_lane = jnp.where(lane == ONES_LANE, 1.0, 0.0)
    v0 = MLA_HEADS * QK_NOPE_DIM
    for pair in range(MLA_HEADS // 2):
        k_pair = kv2[:, pair * HEAD_PAD:(pair + 1) * HEAD_PAD]
        v_pair = kv2[:, v0 + pair * HEAD_PAD:v0 + (pair + 1) * HEAD_PAD]
        for odd in range(2):
            k_h = pltpu.roll(k_pair, QK_NOPE_DIM, 1) if odd else k_pair
            v_h = pltpu.roll(v_pair, V_HEAD_DIM, 1) if odd else v_pair
            k_ref[2 * pair + odd] = (jnp.where(low, k_h, 0.0) + k_rope).astype(bf16)
            v_ref[2 * pair + odd] = jnp.where(low, v_h, ones_lane).astype(bf16)
    sb_ref[...] = _silu(b_gate).astype(sb_ref.dtype)

    y_a = jnp.concatenate(ys, axis=1) * pool_scale_ref[...] * _silu(a_gate)
    y_c = u * sv * _silu(c_gate)

    def gate(i):
        return _sigmoid(gate_pre[i] + b_gate_ref[:, i * D_MODEL:(i + 1) * D_MODEL])

    part_ref[...] = (gate(0) * _dot(y_a.astype(bf16), w_oa_ref[...])
                     + gate(2) * _dot(y_c.astype(bf16), w_oc_ref[...])).astype(part_ref.dtype)
    gate_pre[1] = proj(COL_GATES + D_MODEL, D_MODEL)
    g1_ref[...] = gate(1).astype(g1_ref.dtype)


def _const_spec(stacked, l):
    index = (l,) + (0,) * (stacked.ndim - 1)
    return pl.BlockSpec((None,) + stacked.shape[1:], lambda b, j: index, pipeline_mode=pl.Buffered(1))


def _front(x, cos_t, sin_t, p, l):
    B, S, _ = x.shape
    tb = FRONT_BLOCK
    tok = lambda width: pl.BlockSpec((None, tb, width), lambda b, j: (b, j, 0))
    heads = pl.BlockSpec((None, MLA_HEADS, tb, HEAD_PAD), lambda b, j: (b, 0, j, 0))
    consts = [p["norm_pre"], p["w_head"], p["w_kr"], p["w_tail"], p["b_gate"], p["pool_w"],
              p["pool_scale"], p["q_norm"],
              p["w_uq"], p["kv_norm"], p["w_ukv"], p["v_norm_g"], p["v_norm_b"], p["w_s"],
              p["bs_full"], p["w_oa"], p["w_oc"]]
    head_shape = jax.ShapeDtypeStruct((B, MLA_HEADS, S, HEAD_PAD), bf16)
    return pl.pallas_call(
        _front_kernel,
        grid=(B, S // tb),
        in_specs=[tok(D_MODEL), tok(HEAD_PAD), tok(HEAD_PAD)] + [_const_spec(c, l) for c in consts],
        out_specs=[heads, heads, heads, tok(D_MODEL), tok(D_MODEL), tok(MLA_WIDTH)],
        out_shape=[head_shape, head_shape, head_shape,
                   jax.ShapeDtypeStruct((B, S, D_MODEL), bf16),
                   jax.ShapeDtypeStruct((B, S, D_MODEL), bf16),
                   jax.ShapeDtypeStruct((B, S, MLA_WIDTH), bf16)],
        scratch_shapes=[pltpu.VMEM((POOL_TOP + tb, POOL_WIDTH), f32),
                        pltpu.VMEM((POOL_LEVELS, POOL_TOP + tb, POOL_GROUP_DIM), f32)],
        compiler_params=pltpu.CompilerParams(
            dimension_semantics=("arbitrary", "arbitrary"),
            vmem_limit_bytes=FRONT_VMEM_BYTES),
        name="layer_front",
    )(x, cos_t, sin_t, *consts)


def _attn_kernel(q_ref, k_ref, v_ref, o_ref, s_ref, p_ref, m_ref, acc_ref):
    hp, tq2, _ = q_ref.shape
    tq = tq2 // 2
    u = pl.program_id(2)
    lane = lax.broadcasted_iota(jnp.int32, (tq, HEAD_PAD), 1)
    n_tiles = tq // V7X_LANES
    half = tq // 2
    nt_dims = (((1,), (1,)), ((), ()))

    def init_state():
        m_ref[...] = jnp.full(m_ref.shape, -jnp.inf, f32)
        acc_ref[...] = jnp.zeros(acc_ref.shape, f32)

    def scores(q0, hh, j, slot, diagonal=False):
        start = pl.multiple_of(j * tq, tq)
        if diagonal:
            s_ref[slot, hh, 0:half, 0:half] = lax.dot_general(
                q_ref[hh, q0:q0 + half], k_ref[hh, pl.ds(start, half), :], nt_dims, preferred_element_type=f32)
            s_ref[slot, hh, half:tq, :] = lax.dot_general(
                q_ref[hh, q0 + half:q0 + tq], k_ref[hh, pl.ds(start, tq), :], nt_dims, preferred_element_type=f32)
        else:
            s_ref[slot, hh] = lax.dot_general(q_ref[hh, q0:q0 + tq], k_ref[hh, pl.ds(start, tq), :], nt_dims,
                                              preferred_element_type=f32)

    def softmax_pv(hh, j, slot, diagonal):
        for r in range(tq // SOFTMAX_ROWS):
            r0 = r * SOFTMAX_ROWS
            rows = slice(r0, r0 + SOFTMAX_ROWS)
            live = (r0 + SOFTMAX_ROWS - 1) // V7X_LANES + 1 if diagonal else n_tiles
            read = (half if r0 < half else tq) // V7X_LANES if diagonal else n_tiles

            def tile(c):
                s = s_ref[slot, hh, rows, c * V7X_LANES:(c + 1) * V7X_LANES]
                if diagonal and (c + 1) * V7X_LANES - 1 > r0:
                    row = r0 + lax.broadcasted_iota(jnp.int32, s.shape, 0)
                    col = c * V7X_LANES + lax.broadcasted_iota(jnp.int32, s.shape, 1)
                    s = jnp.where(col <= row, s, -jnp.inf)
                return s

            m_blk = tile(0)
            for c in range(1, live):
                m_blk = jnp.maximum(m_blk, tile(c))
            m_old = m_ref[hh, rows]
            m_new = jnp.maximum(m_old, jnp.max(m_blk, axis=-1, keepdims=True))
            m_ref[hh, rows] = m_new
            acc_ref[hh, rows] = jnp.exp2(m_old - m_new) * acc_ref[hh, rows]
            for c in range(live):
                p_ref[hh, rows, c * V7X_LANES:(c + 1) * V7X_LANES] = jnp.exp2(tile(c) - m_new).astype(bf16)
            if live < read:
                p_ref[hh, rows, live * V7X_LANES:read * V7X_LANES] = jnp.zeros(
                    (SOFTMAX_ROWS, (read - live) * V7X_LANES), bf16)
        start = pl.multiple_of(j * tq, tq)
        if diagonal:
            acc_ref[hh, 0:half] += _dot(p_ref[hh, 0:half, 0:half], v_ref[hh, pl.ds(start, half), :])
            acc_ref[hh, half:tq] += _dot(p_ref[hh, half:tq, :], v_ref[hh, pl.ds(start, tq), :])
        else:
            acc_ref[hh] += _dot(p_ref[hh], v_ref[hh, pl.ds(start, tq), :])

    def step(q0, j, slot, next_diagonal=False):
        for hh in range(hp):
            softmax_pv(hh, j, slot, False)
            scores(q0, hh, j + 1, 1 - slot, next_diagonal)

    def last_step(i, slot, next_pass_scores=None):
        for hh in range(hp):
            softmax_pv(hh, i, slot, True)
            if next_pass_scores is not None:
                next_pass_scores(hh)

    def finish(q0):
        outs = [acc_ref[hh] / acc_ref[hh, :, ONES_LANE:ONES_LANE + 1] for hh in range(hp)]
        o_ref[q0:q0 + tq, :] = jnp.where(lane < V_HEAD_DIM, outs[0],
                                         pltpu.roll(outs[1], V_HEAD_DIM, 1)).astype(o_ref.dtype)

    def pairs(q0, first_slot, trips):
        def body(t, carry):
            step(q0, 2 * t, first_slot)
            step(q0, 2 * t + 1, 1 - first_slot)
            return carry

        lax.fori_loop(0, trips, body, 0)

    i_even = 2 * u
    odd_first_scores = lambda hh: scores(tq, hh, 0, 1)
    init_state()

    @pl.when(u == 0)
    def _():
        for hh in range(hp):
            scores(0, hh, 0, 0, True)
        last_step(i_even, 0, odd_first_scores)

    @pl.when(u > 0)
    def _():
        for hh in range(hp):
            scores(0, hh, 0, 0)
        pairs(0, 0, u - 1)
        step(0, i_even - 2, 0)
        step(0, i_even - 1, 1, True)
        last_step(i_even, 0, odd_first_scores)

    finish(0)

    i_odd = i_even + 1
    init_state()
    pairs(tq, 1, u)
    step(tq, i_odd - 1, 1, True)
    last_step(i_odd, 0)
    finish(tq)


def _attention(q, k, v):
    B, H, S, _ = q.shape
    tq = ATTN_BLOCK
    hp = HEADS_PER_STEP
    assert hp * V_HEAD_DIM == V7X_LANES
    return pl.pallas_call(
        _attn_kernel,
        grid=(B, H // hp, S // (2 * tq)),
        in_specs=[pl.BlockSpec((None, hp, 2 * tq, HEAD_PAD), lambda b, g, u: (b, g, u, 0)),
                  pl.BlockSpec((None, hp, S, HEAD_PAD), lambda b, g, u: (b, g, 0, 0)),
                  pl.BlockSpec((None, hp, S, HEAD_PAD), lambda b, g, u: (b, g, 0, 0))],
        out_specs=pl.BlockSpec((None, 2 * tq, hp * V_HEAD_DIM), lambda b, g, u: (b, u, g)),
        out_shape=jax.ShapeDtypeStruct((B, S, H * V_HEAD_DIM), bf16),
        scratch_shapes=[pltpu.VMEM((2, hp, tq, tq), f32),
                        pltpu.VMEM((hp, tq, tq), bf16),
                        pltpu.VMEM((hp, tq, V7X_LANES), f32),
                        pltpu.VMEM((hp, tq, HEAD_PAD), f32)],
        compiler_params=pltpu.CompilerParams(
            dimension_semantics=("arbitrary", "arbitrary", "arbitrary"),
            vmem_limit_bytes=ATTN_VMEM_BYTES),
        name="mla_attention",
    )(q, k, v)


def _back_kernel(x_ref, o_ref, sb_ref, g1_ref, part_ref, w_ob_ref, w_out_ref, norm_post_ref,
                 out_ref):
    y_b = o_ref[...] * sb_ref[...]
    merged = part_ref[...].astype(f32) + g1_ref[...].astype(f32) * _dot(y_b, w_ob_ref[...])
    z = _dot(merged.astype(bf16), w_out_ref[...])
    out_ref[...] = x_ref[...] + _rms(z, norm_post_ref[...])


def _back(x, o, sb, g1, part, p, l):
    B, S, _ = x.shape
    tb = BACK_BLOCK
    tok = lambda width: pl.BlockSpec((None, tb, width), lambda b, j: (b, j, 0))
    consts = [p["w_ob"], p["w_out"], p["norm_post"]]
    return pl.pallas_call(
        _back_kernel,
        grid=(B, S // tb),
        in_specs=[tok(D_MODEL), tok(MLA_WIDTH), tok(MLA_WIDTH), tok(D_MODEL), tok(D_MODEL)]
        + [_const_spec(c, l) for c in consts],
        out_specs=tok(D_MODEL),
        out_shape=jax.ShapeDtypeStruct((B, S, D_MODEL), f32),
        compiler_params=pltpu.CompilerParams(
            dimension_semantics=("arbitrary", "arbitrary"),
            vmem_limit_bytes=BACK_VMEM_BYTES),
        name="layer_back",
    )(x, o, sb, g1, part, *consts)


def _stacked_params(norm_pre, norm_post, w_in, b_gate, pool_w, pool_scale, q_norm, w_uq,
                    kv_norm, w_ukv, v_norm_g, v_norm_b, w_s, b_s, w_oa, w_ob, w_oc, w_out):
    L = w_in.shape[0]
    zeros = lambda *shape: jnp.zeros(shape, bf16)
    kr_src = sum(IN_SPLITS[:4])
    assert kr_src == COL_KR
    w_head = w_in[..., :kr_src].astype(bf16)
    w_kr = jnp.concatenate([zeros(L, D_MODEL, ROPE_LANE0),
                            w_in[..., kr_src:kr_src + QK_ROPE_DIM].astype(bf16),
                            zeros(L, D_MODEL, HEAD_PAD - ROPE_LANE0 - QK_ROPE_DIM)], axis=-1)
    w_tail = w_in[..., kr_src + QK_ROPE_DIM:].astype(bf16)
    assert w_tail.shape[-1] == IN_WIDTH_PAD - COL_B_GATE

    uq = w_uq.astype(bf16).reshape(L, Q_LORA_RANK, MLA_HEADS, QK_NOPE_DIM + QK_ROPE_DIM)
    nope, x1, x2 = uq[..., :QK_NOPE_DIM], uq[..., QK_NOPE_DIM:QK_NOPE_DIM + HALF_ROPE], uq[..., QK_NOPE_DIM + HALF_ROPE:]
    pad = zeros(L, Q_LORA_RANK, MLA_HEADS, HEAD_PAD - QK_NOPE_DIM - QK_ROPE_DIM)
    w_uq_p = jnp.concatenate([nope, x1, x2, pad], axis=-1).reshape(L, Q_LORA_RANK, MLA_HEADS * HEAD_PAD)

    assert 2 * QK_NOPE_DIM == HEAD_PAD and V_HEAD_DIM == QK_NOPE_DIM
    ukv = w_ukv.astype(bf16).reshape(L, KV_LORA_RANK, MLA_HEADS, QK_NOPE_DIM + V_HEAD_DIM)
    k_cols = ukv[..., :QK_NOPE_DIM].reshape(L, KV_LORA_RANK, MLA_HEADS * QK_NOPE_DIM)
    v_cols = ukv[..., QK_NOPE_DIM:].reshape(L, KV_LORA_RANK, MLA_HEADS * V_HEAD_DIM)
    w_ukv_p = jnp.concatenate([k_cols, v_cols], axis=-1)

    row = lambda a: a.reshape(L, 1, -1)
    return {
        "norm_pre": row(norm_pre), "norm_post": row(norm_post),
        "w_head": w_head, "w_kr": w_kr, "w_tail": w_tail,
        "b_gate": row(b_gate), "pool_w": pool_w.astype(bf16), "pool_scale": row(pool_scale),
        "q_norm": row(q_norm), "w_uq": w_uq_p, "kv_norm": row(kv_norm), "w_ukv": w_ukv_p,
        "v_norm_g": row(v_norm_g), "v_norm_b": row(v_norm_b), "w_s": w_s.astype(bf16),
        "bs_full": jnp.repeat(jnp.swapaxes(b_s, 1, 2), GMLP_GROUP_DIM, axis=2),
        "w_oa": w_oa.astype(bf16), "w_ob": w_ob.astype(bf16), "w_oc": w_oc.astype(bf16),
        "w_out": w_out.astype(bf16),
    }


def kernel(x, positions, norm_pre, norm_post, w_in, b_gate, pool_w, pool_scale, q_norm, w_uq, kv_norm, w_ukv, v_norm_g, v_norm_b, w_s, b_s, w_oa, w_ob, w_oc, w_out):
    B, S, _ = x.shape
    cos_t, sin_t = _rope_tables(positions)
    cos_t = cos_t.reshape(B, S, HEAD_PAD)
    sin_t = sin_t.reshape(B, S, HEAD_PAD)
    p = _stacked_params(norm_pre, norm_post, w_in, b_gate, pool_w, pool_scale, q_norm, w_uq,
                        kv_norm, w_ukv, v_norm_g, v_norm_b, w_s, b_s, w_oa, w_ob, w_oc, w_out)
    for l in range(norm_pre.shape[0]):
        q, k, v, part, g1, sb = _front(x, cos_t, sin_t, p, l)
        o = _attention(q, k, v)
        x = _back(x, o, sb, g1, part, p, l)
    return x
```

```python
import math

import jax
import jax.numpy as jnp
from jax import lax
from jax.experimental import pallas as pl
from jax.experimental.pallas import tpu as pltpu

D_MODEL = 1024
POOL_WIDTH = 512
POOL_GROUPS = 4
POOL_GROUP_DIM = POOL_WIDTH // POOL_GROUPS
POOL_WINDOWS = (2, 4, 8, 16)
MLA_HEADS = 8
QK_NOPE_DIM = 64
QK_ROPE_DIM = 32
HALF_ROPE = QK_ROPE_DIM // 2
V_HEAD_DIM = 64
Q_LORA_RANK = 256
KV_LORA_RANK = 128
MLA_WIDTH = MLA_HEADS * V_HEAD_DIM
ROPE_THETA = 10000.0
GMLP_WIDTH = 512
GMLP_CHUNK = 128
GMLP_GROUPS = 8
GMLP_GROUP_DIM = GMLP_WIDTH // GMLP_GROUPS
N_BRANCHES = 3
EPS = 1e-6
IN_SPLITS = (POOL_WIDTH, POOL_WIDTH, Q_LORA_RANK, KV_LORA_RANK, QK_ROPE_DIM, MLA_WIDTH,
             GMLP_WIDTH, GMLP_WIDTH, GMLP_WIDTH, N_BRANCHES * D_MODEL)

V7X_LANES = 128
V7X_SUBLANES = 8
V7X_VMEM_BYTES = 64 * 1024 * 1024

HEAD_PAD = V7X_LANES
ROPE_LANE0 = QK_NOPE_DIM
ONES_LANE = V_HEAD_DIM
HALO = max(POOL_WINDOWS)
POOL_PAD = V7X_SUBLANES
POOL_TOP = POOL_PAD + HALO
assert all(w & (w - 1) == 0 and w <= HALO for w in POOL_WINDOWS)
POOL_LEVELS = sum(w.bit_length() - 2 for w in POOL_WINDOWS)

COL_A_IN = 0
COL_A_GATE = COL_A_IN + POOL_WIDTH
COL_CQ = COL_A_GATE + POOL_WIDTH
COL_CKV = COL_CQ + Q_LORA_RANK
COL_KR = COL_CKV + KV_LORA_RANK
COL_B_GATE = COL_KR + HEAD_PAD
COL_U = COL_B_GATE + MLA_WIDTH
COL_V = COL_U + GMLP_WIDTH
COL_C_GATE = COL_V + GMLP_WIDTH
COL_GATES = COL_C_GATE + GMLP_WIDTH
IN_WIDTH_PAD = COL_GATES + N_BRANCHES * D_MODEL

SOFTMAX_SCALE = (QK_NOPE_DIM + QK_ROPE_DIM) ** -0.5
LOG2E = math.log2(math.e)

FRONT_BLOCK = 512
FRONT_VMEM_BYTES = V7X_VMEM_BYTES * 7 // 8
ATTN_BLOCK = 1024
ATTN_VMEM_BYTES = V7X_VMEM_BYTES * 7 // 8
BACK_BLOCK = 1024
BACK_VMEM_BYTES = V7X_VMEM_BYTES * 3 // 4
ROPE_BLOCK = 1024
HEADS_PER_STEP = 2
SOFTMAX_ROWS = 64

bf16 = jnp.bfloat16
f32 = jnp.float32


def _sigmoid(z):
    return 0.5 * jnp.tanh(0.5 * z) + 0.5


def _silu(z):
    return z * _sigmoid(z)


def _rms(x, g):
    return x * lax.rsqrt(jnp.mean(x * x, axis=-1, keepdims=True) + EPS) * g


def _dot(a, b):
    return jnp.dot(a, b, preferred_element_type=f32)


def _rope_table_kernel(pos_ref, invf_ref, cos_ref, sin_ref):
    half_rows = pos_ref.shape[0]
    lane = lax.broadcasted_iota(jnp.int32, (half_rows, HEAD_PAD), 1)
    invf = invf_ref[...]
    pos = jnp.where(lane >= ROPE_LANE0, pos_ref[:, 0:1], pos_ref[:, 1:2])
    ang = pos.astype(f32) * invf
    c = jnp.cos(ang)
    s = jnp.sin(ang)
    rope = (lane >= ROPE_LANE0) & (lane < ROPE_LANE0 + QK_ROPE_DIM)
    first_half = lane < ROPE_LANE0 + HALF_ROPE
    for odd in range(2):
        c_t = pltpu.roll(c, ROPE_LANE0, 1) if odd else c
        s_t = pltpu.roll(s, ROPE_LANE0, 1) if odd else s
        rows = pl.ds(odd, half_rows, stride=2)
        cos_ref[rows, :] = jnp.where(lane < ROPE_LANE0, 1.0, jnp.where(rope, c_t, 0.0))
        sin_ref[rows, :] = jnp.where(rope, jnp.where(first_half, -s_t, s_t), 0.0)


def _rope_tables(positions):
    n = positions.size
    assert 2 * ROPE_LANE0 == HEAD_PAD and QK_ROPE_DIM <= ROPE_LANE0
    inv_freq = ROPE_THETA ** (-jnp.arange(0, QK_ROPE_DIM, 2, dtype=f32) / QK_ROPE_DIM)
    pad = jnp.zeros((ROPE_LANE0 - QK_ROPE_DIM,), f32)
    half_tile = jnp.concatenate([inv_freq, inv_freq, pad])
    invf = jnp.concatenate([half_tile, half_tile]).reshape(1, HEAD_PAD)
    out = jax.ShapeDtypeStruct((n, HEAD_PAD), f32)
    return pl.pallas_call(
        _rope_table_kernel,
        grid=(n // ROPE_BLOCK,),
        in_specs=[pl.BlockSpec((ROPE_BLOCK // 2, 2), lambda i: (i, 0)),
                  pl.BlockSpec((1, HEAD_PAD), lambda i: (0, 0))],
        out_specs=[pl.BlockSpec((ROPE_BLOCK, HEAD_PAD), lambda i: (i, 0))] * 2,
        out_shape=[out, out],
        name="rope_tables",
    )(positions.reshape(n // 2, 2), invf)


def _front_kernel(x_ref, cos_ref, sin_ref, norm_pre_ref, w_head_ref, w_kr_ref, w_tail_ref, b_gate_ref, pool_w_ref,
                  pool_scale_ref, q_norm_ref, w_uq_ref, kv_norm_ref, w_ukv_ref, vn_g_ref,
                  vn_b_ref, w_s_ref, bs_ref, w_oa_ref, w_oc_ref,
                  q_ref, k_ref, v_ref, part_ref, g1_ref, sb_ref, abuf, lv_ref):
    tb = x_ref.shape[0]
    j = pl.program_id(1)

    hb = x_ref[...] if x_ref.dtype == bf16 else _rms(x_ref[...], norm_pre_ref[...]).astype(bf16)

    def proj(c0, width):
        if c0 + width <= COL_KR:
            w = w_head_ref[:, c0:c0 + width]
        elif c0 == COL_KR and width == HEAD_PAD:
            w = w_kr_ref[...]
        else:
            assert c0 >= COL_B_GATE
            w = w_tail_ref[:, c0 - COL_B_GATE:c0 - COL_B_GATE + width]
        return _dot(hb, w)

    @pl.when(j == 0)
    def _():
        abuf[0:POOL_TOP, :] = jnp.zeros((POOL_TOP, POOL_WIDTH), f32)
        lv_ref[:, 0:POOL_PAD, :] = jnp.zeros((lv_ref.shape[0], POOL_PAD, POOL_GROUP_DIM), f32)

    a_in = proj(COL_A_IN, POOL_WIDTH)
    abuf[POOL_TOP:POOL_TOP + tb, :] = a_in
    cq = proj(COL_CQ, Q_LORA_RANK)
    ckv = proj(COL_CKV, KV_LORA_RANK)
    kr = proj(COL_KR, HEAD_PAD)
    vv = proj(COL_V, GMLP_WIDTH)
    a_gate = proj(COL_A_GATE, POOL_WIDTH)
    b_gate = proj(COL_B_GATE, MLA_WIDTH)
    u = proj(COL_U, GMLP_WIDTH)
    c_gate = proj(COL_C_GATE, GMLP_WIDTH)

    t = j * tb + lax.broadcasted_iota(jnp.int32, (tb, 1), 0)
    ys = []
    level = 0
    for g, w in enumerate(POOL_WINDOWS):
        cols = slice(g * POOL_GROUP_DIM, (g + 1) * POOL_GROUP_DIM)
        src = lambda lo, n: abuf[lo:lo + n, cols]
        shift = 1
        while 2 * shift < w:
            lo, n = POOL_PAD, HALO + tb
            dst = level
            lv_ref[dst, lo:lo + n, :] = src(lo, n) + src(lo - shift, n)
            src = lambda lo, n, dst=dst: lv_ref[dst, lo:lo + n, :]
            level += 1
            shift *= 2
        win = src(POOL_TOP, tb) + src(POOL_TOP - shift, tb)
        count = jnp.minimum(t + 1, w).astype(f32)
        mixed = win / count - a_in[:, cols]
        ys.append(_dot(mixed.astype(bf16), pool_w_ref[g]))
    abuf[POOL_PAD:POOL_TOP, :] = abuf[tb + POOL_PAD:tb + POOL_TOP, :]

    cqn = _rms(cq, q_norm_ref[...]).astype(bf16)
    q2 = _dot(cqn, w_uq_ref[...])
    ckvn = _rms(ckv, kv_norm_ref[...]).astype(bf16)
    kv2 = _dot(ckvn, w_ukv_ref[...])

    vc = vv - jnp.mean(vv, axis=-1, keepdims=True)
    var = jnp.mean(vc * vc, axis=-1, keepdims=True)
    vb = (vc * lax.rsqrt(var + EPS) * vn_g_ref[...] + vn_b_ref[...]).astype(bf16)
    row = lax.broadcasted_iota(jnp.int32, (GMLP_CHUNK, GMLP_CHUNK), 0)
    col = lax.broadcasted_iota(jnp.int32, (GMLP_CHUNK, GMLP_CHUNK), 1)
    ws = [jnp.where(col <= row, w_s_ref[g], jnp.zeros((), bf16)) for g in range(GMLP_GROUPS)]
    groups_per_tile = V7X_LANES // GMLP_GROUP_DIM
    assert groups_per_tile == 2
    chunks = []
    for c in range(tb // GMLP_CHUNK):
        tiles = []
        for tile in range(GMLP_WIDTH // V7X_LANES):
            rhs = vb[c * GMLP_CHUNK:(c + 1) * GMLP_CHUNK, tile * V7X_LANES:(tile + 1) * V7X_LANES]
            lo = _dot(ws[2 * tile], rhs)
            hi = _dot(ws[2 * tile + 1], rhs)
            tiles.append(jnp.where(col < GMLP_GROUP_DIM, lo, hi))
        chunks.append(jnp.concatenate(tiles, axis=1) + bs_ref[...])
    sv = jnp.concatenate(chunks, axis=0)

    gate_pre = {i: proj(COL_GATES + i * D_MODEL, D_MODEL) for i in (0, 2)}

    cos_t = cos_ref[...]
    sin_t = sin_ref[...]
    lane = lax.broadcasted_iota(jnp.int32, (tb, HEAD_PAD), 1)
    first_half = lane < ROPE_LANE0 + HALF_ROPE

    def rotate(t):
        swapped = jnp.where(first_half, pltpu.roll(t, HEAD_PAD - HALF_ROPE, 1), pltpu.roll(t, HALF_ROPE, 1))
        return t * cos_t + swapped * sin_t

    for h in range(MLA_HEADS):
        q_ref[h] = (rotate(q2[:, h * HEAD_PAD:(h + 1) * HEAD_PAD]) * (SOFTMAX_SCALE * LOG2E)).astype(bf16)
    k_rope = rotate(kr)
    low = lane < QK_NOPE_DIM
    ones_lane = jnp.where(lane == ONES_LANE, 1.0, 0.0)
    v0 = MLA_HEADS * QK_NOPE_DIM
    for pair in range(MLA_HEADS // 2):
        k_pair = kv2[:, pair * HEAD_PAD:(pair + 1) * HEAD_PAD]
        v_pair = kv2[:, v0 + pair * HEAD_PAD:v0 + (pair + 1) * HEAD_PAD]
        for odd in range(2):
            k_h = pltpu.roll(k_pair, QK_NOPE_DIM, 1) if odd else k_pair
            v_h = pltpu.roll(v_pair, V_HEAD_DIM, 1) if odd else v_pair
            k_ref[2 * pair + odd] = (jnp.where(low, k_h, 0.0) + k_rope).astype(bf16)
            v_ref[2 * pair + odd] = jnp.where(low, v_h, ones_lane).astype(bf16)
    sb_ref[...] = _silu(b_gate).astype(sb_ref.dtype)

    y_a = jnp.concatenate(ys, axis=1) * pool_scale_ref[...] * _silu(a_gate)
    y_c = u * sv * _silu(c_gate)

    def gate(i):
        return _sigmoid(gate_pre[i] + b_gate_ref[:, i * D_MODEL:(i + 1) * D_MODEL])

    part_ref[...] = (gate(0) * _dot(y_a.astype(bf16), w_oa_ref[...])
                     + gate(2) * _dot(y_c.astype(bf16), w_oc_ref[...])).astype(part_ref.dtype)
    gate_pre[1] = proj(COL_GATES + D_MODEL, D_MODEL)
    g1_ref[...] = gate(1).astype(g1_ref.dtype)


def _const_spec(stacked, l):
    index = (l,) + (0,) * (stacked.ndim - 1)
    return pl.BlockSpec((None,) + stacked.shape[1:], lambda b, j: index, pipeline_mode=pl.Buffered(1))


def _front(x, cos_t, sin_t, p, l):
    B, S, _ = x.shape
    tb = FRONT_BLOCK
    tok = lambda width: pl.BlockSpec((None, tb, width), lambda b, j: (b, j, 0))
    heads = pl.BlockSpec((None, MLA_HEADS, tb, HEAD_PAD), lambda b, j: (b, 0, j, 0))
    consts = [p["norm_pre"], p["w_head"], p["w_kr"], p["w_tail"], p["b_gate"], p["pool_w"],
              p["pool_scale"], p["q_norm"],
              p["w_uq"], p["kv_norm"], p["w_ukv"], p["v_norm_g"], p["v_norm_b"], p["w_s"],
              p["bs_full"], p["w_oa"], p["w_oc"]]
    head_shape = jax.ShapeDtypeStruct((B, MLA_HEADS, S, HEAD_PAD), bf16)
    return pl.pallas_call(
        _front_kernel,
        grid=(B, S // tb),
        in_specs=[tok(D_MODEL), tok(HEAD_PAD), tok(HEAD_PAD)] + [_const_spec(c, l) for c in consts],
        out_specs=[heads, heads, heads, tok(D_MODEL), tok(D_MODEL), tok(MLA_WIDTH)],
        out_shape=[head_shape, head_shape, head_shape,
                   jax.ShapeDtypeStruct((B, S, D_MODEL), bf16),
                   jax.ShapeDtypeStruct((B, S, D_MODEL), bf16),
                   jax.ShapeDtypeStruct((B, S, MLA_WIDTH), bf16)],
        scratch_shapes=[pltpu.VMEM((POOL_TOP + tb, POOL_WIDTH), f32),
                        pltpu.VMEM((POOL_LEVELS, POOL_TOP + tb, POOL_GROUP_DIM), f32)],
        compiler_params=pltpu.CompilerParams(
            dimension_semantics=("arbitrary", "arbitrary"),
            vmem_limit_bytes=FRONT_VMEM_BYTES),
        name="layer_front",
    )(x, cos_t, sin_t, *consts)


def _attn_kernel(q_ref, k_ref, v_ref, o_ref, s_ref, p_ref, m_ref, acc_ref):
    hp, tq2, _ = q_ref.shape
    tq = tq2 // 2
    u = pl.program_id(2)
    lane = lax.broadcasted_iota(jnp.int32, (tq, HEAD_PAD), 1)
    n_tiles = tq // V7X_LANES
    half = tq // 2
    nt_dims = (((1,), (1,)), ((), ()))

    def init_state():
        m_ref[...] = jnp.full(m_ref.shape, -jnp.inf, f32)
        acc_ref[...] = jnp.zeros(acc_ref.shape, f32)

    def scores(q0, hh, j, slot, diagonal=False):
        start = pl.multiple_of(j * tq, tq)
        if diagonal:
            s_ref[slot, hh, 0:half, 0:half] = lax.dot_general(
                q_ref[hh, q0:q0 + half], k_ref[hh, pl.ds(start, half), :], nt_dims, preferred_element_type=f32)
            s_ref[slot, hh, half:tq, :] = lax.dot_general(
                q_ref[hh, q0 + half:q0 + tq], k_ref[hh, pl.ds(start, tq), :], nt_dims, preferred_element_type=f32)
        else:
            s_ref[slot, hh] = lax.dot_general(q_ref[hh, q0:q0 + tq], k_ref[hh, pl.ds(start, tq), :], nt_dims,
                                              preferred_element_type=f32)

    def softmax_pv(hh, j, slot, diagonal):
        for r in range(tq // SOFTMAX_ROWS):
            r0 = r * SOFTMAX_ROWS
            rows = slice(r0, r0 + SOFTMAX_ROWS)
            live = (r0 + SOFTMAX_ROWS - 1) // V7X_LANES + 1 if diagonal else n_tiles
            read = (half if r0 < half else tq) // V7X_LANES if diagonal else n_tiles

            def tile(c):
                s = s_ref[slot, hh, rows, c * V7X_LANES:(c + 1) * V7X_LANES]
                if diagonal and (c + 1) * V7X_LANES - 1 > r0:
                    row = r0 + lax.broadcasted_iota(jnp.int32, s.shape, 0)
                    col = c * V7X_LANES + lax.broadcasted_iota(jnp.int32, s.shape, 1)
                    s = jnp.where(col <= row, s, -jnp.inf)
                return s

            m_blk = tile(0)
            for c in range(1, live):
                m_blk = jnp.maximum(m_blk, tile(c))
            m_old = m_ref[hh, rows]
            m_new = jnp.maximum(m_old, jnp.max(m_blk, axis=-1, keepdims=True))
            m_ref[hh, rows] = m_new
            acc_ref[hh, rows] = jnp.exp2(m_old - m_new) * acc_ref[hh, rows]
            for c in range(live):
                p_ref[hh, rows, c * V7X_LANES:(c + 1) * V7X_LANES] = jnp.exp2(tile(c) - m_new).astype(bf16)
            if live < read:
                p_ref[hh, rows, live * V7X_LANES:read * V7X_LANES] = jnp.zeros(
                    (SOFTMAX_ROWS, (read - live) * V7X_LANES), bf16)
        start = pl.multiple_of(j * tq, tq)
        if diagonal:
            acc_ref[hh, 0:half] += _dot(p_ref[hh, 0:half, 0:half], v_ref[hh, pl.ds(start, half), :])
            acc_ref[hh, half:tq] += _dot(p_ref[hh, half:tq, :], v_ref[hh, pl.ds(start, tq), :])
        else:
            acc_ref[hh] += _dot(p_ref[hh], v_ref[hh, pl.ds(start, tq), :])

    def step(q0, j, slot, next_diagonal=False):
        for hh in range(hp):
            softmax_pv(hh, j, slot, False)
            scores(q0, hh, j + 1, 1 - slot, next_diagonal)

    def last_step(i, slot, next_pass_scores=None):
        for hh in range(hp):
            softmax_pv(hh, i, slot, True)
            if next_pass_scores is not None:
                next_pass_scores(hh)

    def finish(q0):
        outs = [acc_ref[hh] / acc_ref[hh, :, ONES_LANE:ONES_LANE + 1] for hh in range(hp)]
        o_ref[q0:q0 + tq, :] = jnp.where(lane < V_HEAD_DIM, outs[0],
                                         pltpu.roll(outs[1], V_HEAD_DIM, 1)).astype(o_ref.dtype)

    def pairs(q0, first_slot, trips):
        def body(t, carry):
            step(q0, 2 * t, first_slot)
            step(q0, 2 * t + 1, 1 - first_slot)
            return carry

        lax.fori_loop(0, trips, body, 0)

    i_even = 2 * u
    odd_first_scores = lambda hh: scores(tq, hh, 0, 1)
    init_state()

    @pl.when(u == 0)
    def _():
        for hh in range(hp):
            scores(0, hh, 0, 0, True)
        last_step(i_even, 0, odd_first_scores)

    @pl.when(u > 0)
    def _():
        for hh in range(hp):
            scores(0, hh, 0, 0)
        pairs(0, 0, u - 1)
        step(0, i_even - 2, 0)
        step(0, i_even - 1, 1, True)
        last_step(i_even, 0, odd_first_scores)

    finish(0)

    i_odd = i_even + 1
    init_state()
    pairs(tq, 1, u)
    step(tq, i_odd - 1, 1, True)
    last_step(i_odd, 0)
    finish(tq)


def _attention(q, k, v):
    B, H, S, _ = q.shape
    tq = ATTN_BLOCK
    hp = HEADS_PER_STEP
    assert hp * V_HEAD_DIM == V7X_LANES
    return pl.pallas_call(
        _attn_kernel,
        grid=(B, H // hp, S // (2 * tq)),
        in_specs=[pl.BlockSpec((None, hp, 2 * tq, HEAD_PAD), lambda b, g, u: (b, g, u, 0)),
                  pl.BlockSpec((None, hp, S, HEAD_PAD), lambda b, g, u: (b, g, 0, 0)),
                  pl.BlockSpec((None, hp, S, HEAD_PAD), lambda b, g, u: (b, g, 0, 0))],
        out_specs=pl.BlockSpec((None, 2 * tq, hp * V_HEAD_DIM), lambda b, g, u: (b, u, g)),
        out_shape=jax.ShapeDtypeStruct((B, S, H * V_HEAD_DIM), bf16),
        scratch_shapes=[pltpu.VMEM((2, hp, tq, tq), f32),
                        pltpu.VMEM((hp, tq, tq), bf16),
                        pltpu.VMEM((hp, tq, V7X_LANES), f32),
                        pltpu.VMEM((hp, tq, HEAD_PAD), f32)],
        compiler_params=pltpu.CompilerParams(
            dimension_semantics=("arbitrary", "arbitrary", "arbitrary"),
            vmem_limit_bytes=ATTN_VMEM_BYTES),
        name="mla_attention",
    )(q, k, v)


def _back_kernel(x_ref, o_ref, sb_ref, g1_ref, part_ref, w_ob_ref, w_out_ref, norm_post_ref,
                 norm_next_ref, out_ref, *hb_next_ref):
    y_b = o_ref[...] * sb_ref[...]
    merged = part_ref[...].astype(f32) + g1_ref[...].astype(f32) * _dot(y_b, w_ob_ref[...])
    z = _dot(merged.astype(bf16), w_out_ref[...])
    out = x_ref[...] + _rms(z, norm_post_ref[...])
    out_ref[...] = out
    if hb_next_ref:
        hb_next_ref[0][...] = _rms(out, norm_next_ref[...]).astype(bf16)


def _back(x, o, sb, g1, part, p, l, last):
    B, S, _ = x.shape
    tb = BACK_BLOCK
    tok = lambda width: pl.BlockSpec((None, tb, width), lambda b, j: (b, j, 0))
    consts = [p["w_ob"], p["w_out"], p["norm_post"]]
    out_shape = jax.ShapeDtypeStruct((B, S, D_MODEL), f32)
    hb_shape = jax.ShapeDtypeStruct((B, S, D_MODEL), bf16)
    return pl.pallas_call(
        _back_kernel,
        grid=(B, S // tb),
        in_specs=[tok(D_MODEL), tok(MLA_WIDTH), tok(MLA_WIDTH), tok(D_MODEL), tok(D_MODEL)]
        + [_const_spec(c, l) for c in consts] + [_const_spec(p["norm_pre"], l if last else l + 1)],
        out_specs=[tok(D_MODEL)] if last else [tok(D_MODEL), tok(D_MODEL)],
        out_shape=[out_shape] if last else [out_shape, hb_shape],
        compiler_params=pltpu.CompilerParams(
            dimension_semantics=("arbitrary", "arbitrary"),
            vmem_limit_bytes=BACK_VMEM_BYTES),
        name="layer_back",
    )(x, o, sb, g1, part, *consts, p["norm_pre"])


def _stacked_params(norm_pre, norm_post, w_in, b_gate, pool_w, pool_scale, q_norm, w_uq,
                    kv_norm, w_ukv, v_norm_g, v_norm_b, w_s, b_s, w_oa, w_ob, w_oc, w_out):
    L = w_in.shape[0]
    zeros = lambda *shape: jnp.zeros(shape, bf16)
    kr_src = sum(IN_SPLITS[:4])
    assert kr_src == COL_KR
    w_head = w_in[..., :kr_src].astype(bf16)
    w_kr = jnp.concatenate([zeros(L, D_MODEL, ROPE_LANE0),
                            w_in[..., kr_src:kr_src + QK_ROPE_DIM].astype(bf16),
                            zeros(L, D_MODEL, HEAD_PAD - ROPE_LANE0 - QK_ROPE_DIM)], axis=-1)
    w_tail = w_in[..., kr_src + QK_ROPE_DIM:].astype(bf16)
    assert w_tail.shape[-1] == IN_WIDTH_PAD - COL_B_GATE

    uq = w_uq.astype(bf16).reshape(L, Q_LORA_RANK, MLA_HEADS, QK_NOPE_DIM + QK_ROPE_DIM)
    nope, x1, x2 = uq[..., :QK_NOPE_DIM], uq[..., QK_NOPE_DIM:QK_NOPE_DIM + HALF_ROPE], uq[..., QK_NOPE_DIM + HALF_ROPE:]
    pad = zeros(L, Q_LORA_RANK, MLA_HEADS, HEAD_PAD - QK_NOPE_DIM - QK_ROPE_DIM)
    w_uq_p = jnp.concatenate([nope, x1, x2, pad], axis=-1).reshape(L, Q_LORA_RANK, MLA_HEADS * HEAD_PAD)

    assert 2 * QK_NOPE_DIM == HEAD_PAD and V_HEAD_DIM == QK_NOPE_DIM
    ukv = w_ukv.astype(bf16).reshape(L, KV_LORA_RANK, MLA_HEADS, QK_NOPE_DIM + V_HEAD_DIM)
    k_cols = ukv[..., :QK_NOPE_DIM].reshape(L, KV_LORA_RANK, MLA_HEADS * QK_NOPE_DIM)
    v_cols = ukv[..., QK_NOPE_DIM:].reshape(L, KV_LORA_RANK, MLA_HEADS * V_HEAD_DIM)
    w_ukv_p = jnp.concatenate([k_cols, v_cols], axis=-1)

    row = lambda a: a.reshape(L, 1, -1)
    return {
        "norm_pre": row(norm_pre), "norm_post": row(norm_post),
        "w_head": w_head, "w_kr": w_kr, "w_tail": w_tail,
        "b_gate": row(b_gate), "pool_w": pool_w.astype(bf16), "pool_scale": row(pool_scale),
        "q_norm": row(q_norm), "w_uq": w_uq_p, "kv_norm": row(kv_norm), "w_ukv": w_ukv_p,
        "v_norm_g": row(v_norm_g), "v_norm_b": row(v_norm_b), "w_s": w_s.astype(bf16),
        "bs_full": jnp.repeat(jnp.swapaxes(b_s, 1, 2), GMLP_GROUP_DIM, axis=2),
        "w_oa": w_oa.astype(bf16), "w_ob": w_ob.astype(bf16), "w_oc": w_oc.astype(bf16),
        "w_out": w_out.astype(bf16),
    }


def kernel(x, positions, norm_pre, norm_post, w_in, b_gate, pool_w, pool_scale, q_norm, w_uq, kv_norm, w_ukv, v_norm_g, v_norm_b, w_s, b_s, w_oa, w_ob, w_oc, w_out):
    B, S, _ = x.shape
    cos_t, sin_t = _rope_tables(positions)
    cos_t = cos_t.reshape(B, S, HEAD_PAD)
    sin_t = sin_t.reshape(B, S, HEAD_PAD)
    p = _stacked_params(norm_pre, norm_post, w_in, b_gate, pool_w, pool_scale, q_norm, w_uq,
                        kv_norm, w_ukv, v_norm_g, v_norm_b, w_s, b_s, w_oa, w_ob, w_oc, w_out)
    depth = norm_pre.shape[0]
    front_in = x
    for l in range(depth):
        q, k, v, part, g1, sb = _front(front_in, cos_t, sin_t, p, l)
        o = _attention(q, k, v)
        x, *hb_next = _back(x, o, sb, g1, part, p, l, last=l + 1 == depth)
        if hb_next:
            front_in = hb_next[0]
    return x
```
